```python
import jax, jax.numpy as jnp
from jax import lax
import numpy as np

D_MODEL = 1024
BATCH = 8
SEQ = 2048
DEPTH = 1
DEC_BATCH = 128
DEC_SEQ = 4
PAST_LEN = 16384
PAGE_SIZE = 128

A_WIDTH = D_MODEL // 2
A_HEAD_K = 128
A_HEADS = A_WIDTH // A_HEAD_K
A_HEAD_V = A_WIDTH // A_HEADS
A_CHUNK = 64
B_WIDTH = D_MODEL // 2
B_HEAD = 64
B_HEADS = B_WIDTH // B_HEAD
DECAY_LORA = 64
AAA_LORA = 64
GATE_LORA = 128
D_FF = 4 * D_MODEL
NORM_EPS = 1e-6
HGRN_NORM_EPS = 1e-5
GN_EPS = 64e-5
DECAY_SCALE = 0.6065306597126334
A_COLS = 4 * A_WIDTH
B_COLS = 3 * B_WIDTH + DECAY_LORA + AAA_LORA + GATE_LORA
GATE_COLS = 2 * D_MODEL
IN_COLS = A_COLS + B_COLS + GATE_COLS

kernel_name = "hgrn2_rwkv7_gated_hybrid_step"


def _rmsnorm(x, g, eps=NORM_EPS):
    xf = x.astype(jnp.float32)
    var = jnp.mean(jnp.square(xf), axis=-1, keepdims=True)
    return (xf * lax.rsqrt(var + eps) * g.astype(jnp.float32)).astype(x.dtype)


def _chunk_len(T):
    for c in (A_CHUNK, 32, 16, 8, 4, 2, 1):
        if T % c == 0:
            return c
    return 1


def _gla_chunked(q, k, v, logf, s0):
    B, T, H, dk = q.shape
    dv = v.shape[-1]
    C = _chunk_len(T)
    n = T // C

    def to_chunks(a):
        return a.reshape(B, n, C, H, a.shape[-1]).transpose(1, 0, 3, 2, 4)

    mask = jnp.tril(jnp.ones((C, C), dtype=bool))[:, :, None]

    def step(S, xs):
        qc, kc, vc, gc = xs
        b = jnp.cumsum(gc, axis=2)
        inter = jnp.einsum('bhtk,bhkv->bhtv', qc * jnp.exp(b), S)
        diff = b[:, :, :, None, :] - b[:, :, None, :, :]
        decay = jnp.exp(jnp.where(mask, diff, -jnp.inf))
        att = jnp.einsum('bhtk,bhtsk,bhsk->bhts', qc, decay, kc)
        intra = jnp.einsum('bhts,bhsv->bhtv', att, vc)
        b_last = b[:, :, -1:, :]
        S_new = S * jnp.exp(b_last[:, :, 0, :, None]) + jnp.einsum(
            'bhsk,bhsv->bhkv', kc * jnp.exp(b_last - b), vc)
        return S_new, inter + intra

    S, o = lax.scan(step, s0, (to_chunks(q), to_chunks(k), to_chunks(v), to_chunks(logf)))
    o = o.transpose(1, 0, 3, 2, 4).reshape(B, T, H, dv)
    return o, S


def _hgrn2_branch(u_a, lb, hgrn_norm_w, s0):
    B, T, _ = u_a.shape
    uf = u_a.astype(jnp.float32)
    q, fz, i, g = jnp.split(uf, 4, axis=-1)
    lbh = lb.reshape(A_HEADS, A_HEAD_K)
    fz = fz.reshape(B, T, A_HEADS, A_HEAD_K)
    logf = jnp.log(lbh + (1.0 - lbh) * jax.nn.sigmoid(fz))
    k = (1.0 - lbh) * jax.nn.sigmoid(-fz)
    q = jax.nn.silu(q).reshape(B, T, A_HEADS, A_HEAD_K)
    i = i.reshape(B, T, A_HEADS, A_HEAD_V)
    o, S = _gla_chunked(q, k, i, logf, s0.astype(jnp.float32))
    o = o * lax.rsqrt(jnp.mean(jnp.square(o), axis=-1, keepdims=True) + HGRN_NORM_EPS)
    o = o * hgrn_norm_w.astype(jnp.float32).reshape(A_HEADS, A_HEAD_V)
    o = o.reshape(B, T, A_WIDTH) * jax.nn.silu(g)
    return o, S


def _rwkv7_branch(u_b, u_b_prev, s0, mu_shift, w_decay0, w_decay_up, a0, w_aaa_up,
                  w_gate_up, k_k, k_a, r_k, ln_x_w, ln_x_b):
    B, T, _ = u_b.shape
    f32 = jnp.float32
    ub = u_b.astype(f32)
    up = jnp.concatenate([u_b_prev.astype(f32)[:, None, :], ub[:, :-1]], axis=1)
    xm = ub + (up - ub) * mu_shift.astype(f32)
    s1, s2, s3 = B_WIDTH, 2 * B_WIDTH, 3 * B_WIDTH
    s4, s5 = s3 + DECAY_LORA, s3 + DECAY_LORA + AAA_LORA
    r, k, v = xm[..., :s1], xm[..., s1:s2], xm[..., s2:s3]
    wd, ad, gd = xm[..., s3:s4], xm[..., s4:s5], xm[..., s5:]
    logw = -DECAY_SCALE * jax.nn.sigmoid(w_decay0.astype(f32) + jnp.tanh(wd) @ w_decay_up.astype(f32))
    a = jax.nn.sigmoid(a0.astype(f32) + ad @ w_aaa_up.astype(f32))
    g = jax.nn.sigmoid(gd) @ w_gate_up.astype(f32)
    hs = (B, T, B_HEADS, B_HEAD)
    kk = (k * k_k.astype(f32)).reshape(hs)
    kk = kk / jnp.maximum(jnp.sqrt(jnp.sum(jnp.square(kk), axis=-1, keepdims=True)), 1e-12)
    k = k * (1.0 + (a - 1.0) * k_a.astype(f32))
    r, k, v, w, a = (t.reshape(hs) for t in (r, k, v, jnp.exp(logw), a))

    def step(S, xs):
        rt, wt, kt, vt, kkt, at = xs
        sk = jnp.einsum('bhij,bhj->bhi', S, kkt)
        S = S * wt[:, :, None, :] - sk[..., None] * (kkt * at)[:, :, None, :] + vt[..., None] * kt[:, :, None, :]
        return S, jnp.einsum('bhij,bhj->bhi', S, rt)

    tr = lambda t: jnp.swapaxes(t, 0, 1)
    S, y = lax.scan(step, s0.astype(f32), (tr(r), tr(w), tr(k), tr(v), tr(kk), tr(a)))
    y = tr(y)
    mean = jnp.mean(y, axis=-1, keepdims=True)
    var = jnp.mean(jnp.square(y - mean), axis=-1, keepdims=True)
    yn = (y - mean) * lax.rsqrt(var + GN_EPS)
    yn = yn * ln_x_w.astype(f32).reshape(B_HEADS, B_HEAD) + ln_x_b.astype(f32).reshape(B_HEADS, B_HEAD)
    bonus = jnp.sum(r * k * r_k.astype(f32), axis=-1, keepdims=True) * v
    o = (yn + bonus).reshape(B, T, B_WIDTH) * g
    return o, S


def _layer(x, h_prev, s_a, s_b, lb, norm_mix_g, w_in, mu_shift, w_decay0, w_decay_up, a0,
           w_aaa_up, w_gate_up, k_k, k_a, r_k, ln_x_w, ln_x_b, hgrn_norm_w, w_a_out, w_b_out,
           w_out, norm_mlp_g, w_up, w_down):
    h = _rmsnorm(x, norm_mix_g)
    u = jnp.einsum('btd,dc->btc', h, w_in)
    u_a = u[..., :A_COLS]
    u_b = u[..., A_COLS:A_COLS + B_COLS]
    u_g = u[..., A_COLS + B_COLS:]
    u_b_prev = h_prev @ w_in[:, A_COLS:A_COLS + B_COLS]
    o_a, s_a_new = _hgrn2_branch(u_a, lb, hgrn_norm_w, s_a)
    o_b, s_b_new = _rwkv7_branch(u_b, u_b_prev, s_b, mu_shift, w_decay0, w_decay_up, a0,
                                 w_aaa_up, w_gate_up, k_k, k_a, r_k, ln_x_w, ln_x_b)
    y_a = o_a.astype(x.dtype) @ w_a_out
    y_b = o_b.astype(x.dtype) @ w_b_out
    gate_a = jax.nn.sigmoid(u_g[..., :D_MODEL])
    gate_b = jax.nn.sigmoid(u_g[..., D_MODEL:])
    x = x + (gate_a * y_a + gate_b * y_b) @ w_out
    h2 = _rmsnorm(x, norm_mlp_g)
    x = x + jnp.square(jax.nn.relu(h2 @ w_up)) @ w_down
    return x, h[:, -1, :], s_a_new, s_b_new


def _trunk(x, shift0, sa0, sb0, norm_mix_g, w_in, mu_shift, w_decay0, w_decay_up, a0, w_aaa_up,
           w_gate_up, k_k, k_a, r_k, ln_x_w, ln_x_b, lb_logits, hgrn_norm_w, w_a_out, w_b_out,
           w_out, norm_mlp_g, w_up, w_down, norm_final_g, state_dtype):
    lb_all = jnp.cumsum(jax.nn.softmax(lb_logits.astype(jnp.float32), axis=0), axis=0)
    shifts, sas, sbs = [], [], []
    for l in range(DEPTH):
        x, sh, sa, sb = _layer(
            x, shift0[l], sa0[l], sb0[l], lb_all[l], norm_mix_g[l], w_in[l], mu_shift[l],
            w_decay0[l], w_decay_up[l], a0[l], w_aaa_up[l], w_gate_up[l], k_k[l], k_a[l], r_k[l],
            ln_x_w[l], ln_x_b[l], hgrn_norm_w[l], w_a_out[l], w_b_out[l], w_out[l],
            norm_mlp_g[l], w_up[l], w_down[l])
        shifts.append(sh)
        sas.append(sa.astype(state_dtype))
        sbs.append(sb.astype(state_dtype))
    y = _rmsnorm(x, norm_final_g)
    return y, jnp.stack(sas, axis=0), jnp.stack(sbs, axis=0), jnp.stack(shifts, axis=0)


def setup_inputs(seed: int = 0) -> dict:
    key = jax.random.key(seed)
    ks = jax.random.split(key, 32)
    f32 = jnp.float32
    nrm = lambda k, shape, s: jax.random.normal(k, shape, f32) * s
    L = DEPTH
    return {
        "x_prompt": nrm(ks[0], (BATCH, SEQ, D_MODEL), 1.0),
        "x_sample": nrm(ks[1], (DEC_BATCH, DEC_SEQ, D_MODEL), 1.0),
        "state_hgrn": nrm(ks[2], (L, DEC_BATCH, A_HEADS, A_HEAD_K, A_HEAD_V), 0.3),
        "state_wkv": nrm(ks[3], (L, DEC_BATCH, B_HEADS, B_HEAD, B_HEAD), 0.3),
        "state_shift": nrm(ks[4], (L, DEC_BATCH, D_MODEL), 1.0),
        "norm_mix_g": 1.0 + nrm(ks[5], (L, D_MODEL), 0.02),
        "w_in": nrm(ks[6], (L, D_MODEL, IN_COLS), D_MODEL ** -0.5),
        "mu_shift": jax.random.uniform(ks[7], (L, B_COLS), f32),
        "w_decay0": nrm(ks[8], (L, B_WIDTH), 0.5),
        "w_decay_up": nrm(ks[9], (L, DECAY_LORA, B_WIDTH), DECAY_LORA ** -0.5),
        "a0": nrm(ks[10], (L, B_WIDTH), 0.3),
        "w_aaa_up": nrm(ks[11], (L, AAA_LORA, B_WIDTH), AAA_LORA ** -0.5),
        "w_gate_up": nrm(ks[12], (L, GATE_LORA, B_WIDTH), GATE_LORA ** -0.5),
        "k_k": 0.85 + nrm(ks[13], (L, B_WIDTH), 0.05),
        "k_a": 1.0 + nrm(ks[14], (L, B_WIDTH), 0.05),
        "r_k": nrm(ks[15], (L, B_HEADS, B_HEAD), 0.1),
        "ln_x_w": 1.0 + nrm(ks[16], (L, B_WIDTH), 0.02),
        "ln_x_b": nrm(ks[17], (L, B_WIDTH), 0.02),
        "lb_logits": nrm(ks[18], (L + 1, A_WIDTH), 0.5),
        "hgrn_norm_w": 1.0 + nrm(ks[19], (L, A_WIDTH), 0.02),
        "w_a_out": nrm(ks[20], (L, A_WIDTH, D_MODEL), A_WIDTH ** -0.5),
        "w_b_out": nrm(ks[21], (L, B_WIDTH, D_MODEL), B_WIDTH ** -0.5),
        "w_out": nrm(ks[22], (L, D_MODEL, D_MODEL), D_MODEL ** -0.5),
        "norm_mlp_g": 1.0 + nrm(ks[23], (L, D_MODEL), 0.02),
        "w_up": nrm(ks[24], (L, D_MODEL, D_FF), D_MODEL ** -0.5),
        "w_down": nrm(ks[25], (L, D_FF, D_MODEL), D_FF ** -0.5),
        "norm_final_g": 1.0 + nrm(ks[26], (D_MODEL,), 0.02),
    }


def reference(x_prompt, x_sample, state_hgrn, state_wkv, state_shift, norm_mix_g, w_in, mu_shift,
              w_decay0, w_decay_up, a0, w_aaa_up, w_gate_up, k_k, k_a, r_k, ln_x_w, ln_x_b,
              lb_logits, hgrn_norm_w, w_a_out, w_b_out, w_out, norm_mlp_g, w_up, w_down,
              norm_final_g):
    weights = (norm_mix_g, w_in, mu_shift, w_decay0, w_decay_up, a0, w_aaa_up, w_gate_up, k_k,
               k_a, r_k, ln_x_w, ln_x_b, lb_logits, hgrn_norm_w, w_a_out, w_b_out, w_out,
               norm_mlp_g, w_up, w_down, norm_final_g)
    sdt = state_hgrn.dtype
    bp = x_prompt.shape[0]
    shift0 = jnp.zeros((DEPTH, bp, D_MODEL), x_prompt.dtype)
    sa0 = jnp.zeros((DEPTH, bp, A_HEADS, A_HEAD_K, A_HEAD_V), jnp.float32)
    sb0 = jnp.zeros((DEPTH, bp, B_HEADS, B_HEAD, B_HEAD), jnp.float32)
    y_prompt, hgrn_p, wkv_p, shift_p = _trunk(x_prompt, shift0, sa0, sb0, *weights, sdt)
    y_sample, hgrn_s, wkv_s, shift_s = _trunk(x_sample, state_shift, state_hgrn, state_wkv,
                                              *weights, sdt)
    return (y_prompt, y_sample, hgrn_p, wkv_p, shift_p, hgrn_s, wkv_s, shift_s)
```

```python
import functools

import jax
import jax.numpy as jnp
from jax import lax
from jax.experimental import pallas as pl
from jax.experimental.pallas import tpu as pltpu

F32 = jnp.float32
BF16 = jnp.bfloat16

D_MODEL = 1024
A_WIDTH = 512
A_HEADS = 4
A_HEAD = 128
B_WIDTH = 512
B_HEADS = 8
B_HEAD = 64
DECAY_LORA = 64
AAA_LORA = 64
GATE_LORA = 128
LORA_COLS = DECAY_LORA + AAA_LORA + GATE_LORA
D_FF = 4 * D_MODEL
A_COLS = 4 * A_WIDTH
B_COLS = 3 * B_WIDTH + LORA_COLS
GATE_COLS = 2 * D_MODEL
NORM_EPS = 1e-6
HGRN_NORM_EPS = 1e-5
GN_EPS = 64e-5
DECAY_SCALE = 0.6065306597126334

V7X_LANES = 128
V7X_SUBLANES = 8
V7X_VMEM_LIMIT_BYTES = 56 * 1024 * 1024

TOKEN_TILE = 256
TIME_BLOCK = 64
ROWS_PER_GROUP = 64


def _params(*semantics):
    return pltpu.CompilerParams(dimension_semantics=semantics,
                                vmem_limit_bytes=V7X_VMEM_LIMIT_BYTES)


def _full(shape):
    return pl.BlockSpec(shape, lambda *_: (0,) * len(shape))


def _rmsnorm(x, g):
    return x * lax.rsqrt(jnp.mean(x * x, axis=-1, keepdims=True) + NORM_EPS) * g


def _bdot(a, w):
    return jnp.dot(a.astype(BF16), w, preferred_element_type=F32)


def _segment_sum(a, ones_bd):
    hi = a.astype(BF16)
    lo = (a - hi.astype(F32)).astype(BF16)
    return (jnp.dot(hi, ones_bd, preferred_element_type=F32)
            + jnp.dot(lo, ones_bd, preferred_element_type=F32))


def _norm_proj_kernel(x_ref, g_ref, wa_ref, wb_ref, wg_ref, ua_ref, ub_ref, gate_ref):
    hb = _rmsnorm(x_ref[...], g_ref[...]).astype(BF16)
    ua_ref[...] = jnp.dot(hb, wa_ref[...], preferred_element_type=F32)
    ub_ref[...] = jnp.dot(hb, wb_ref[...], preferred_element_type=F32)
    gate_ref[...] = jax.nn.sigmoid(jnp.dot(hb, wg_ref[...], preferred_element_type=F32))


def _norm_proj(x, g, wa, wb, wg):
    n = x.shape[0]
    tm = TOKEN_TILE
    row = lambda c: pl.BlockSpec((tm, c), lambda i: (i, 0))
    return pl.pallas_call(
        _norm_proj_kernel,
        grid=(n // tm,),
        in_specs=[row(D_MODEL), _full((1, D_MODEL)), _full((D_MODEL, A_COLS)),
                  _full((D_MODEL, B_COLS)), _full((D_MODEL, GATE_COLS))],
        out_specs=[row(A_COLS), row(B_COLS), row(GATE_COLS)],
        out_shape=[jax.ShapeDtypeStruct((n, A_COLS), F32),
                   jax.ShapeDtypeStruct((n, B_COLS), F32),
                   jax.ShapeDtypeStruct((n, GATE_COLS), F32)],
        compiler_params=_params("arbitrary"),
        name="norm_proj",
    )(x, g, wa, wb, wg)


def _norm_rows_kernel(x_ref, g_ref, o_ref):
    o_ref[...] = _rmsnorm(x_ref[...], g_ref[...])


def _norm_rows(x, g):
    return pl.pallas_call(
        _norm_rows_kernel,
        out_shape=jax.ShapeDtypeStruct(x.shape, F32),
        name="norm_rows",
    )(x, g)


def _prev_proj_kernel(h_ref, w_ref, o_ref):
    o_ref[...] = _bdot(h_ref[...], w_ref[...])


def _prev_proj(h_prev, wb):
    return pl.pallas_call(
        _prev_proj_kernel,
        out_shape=jax.ShapeDtypeStruct((h_prev.shape[0], B_COLS), F32),
        compiler_params=_params(),
        name="prev_proj",
    )(h_prev, wb)


def _rwkv_prep_kernel(ub_ref, p0_ref, mu_ref, wlora_ref, wd0_ref, a0_ref, kk_w_ref, ka_ref,
                      rk_ref, ones_ref, r_ref, w_ref, k_ref, v_ref, kk_ref, b_ref, g_ref,
                      bonus_ref, carry_ref, *, rows, tiles_per_seq, time_stride):
    ub = ub_ref[...]
    if time_stride == 1:
        i = pl.program_id(0)
        seq = i // tiles_per_seq
        first = jnp.where(i % tiles_per_seq == 0, p0_ref[pl.ds(seq, 1), :], carry_ref[...])
        row_id = lax.broadcasted_iota(jnp.int32, ub.shape, 0)
        up = jnp.where(row_id == 0, first, pltpu.roll(ub, 1, 0))
        carry_ref[...] = ub[rows - 1:rows, :]
    else:
        up = jnp.concatenate([p0_ref[...], ub[:rows - time_stride, :]], axis=0)
    xm = ub + (up - ub) * mu_ref[...]
    r = xm[:, 0:B_WIDTH]
    k = xm[:, B_WIDTH:2 * B_WIDTH]
    v = xm[:, 2 * B_WIDTH:3 * B_WIDTH]
    lo = xm[:, 3 * B_WIDTH:]
    col = lax.broadcasted_iota(jnp.int32, lo.shape, 1)
    act = jnp.where(col < DECAY_LORA, jnp.tanh(lo),
                    jnp.where(col < DECAY_LORA + AAA_LORA, lo, jax.nn.sigmoid(lo)))
    up_proj = _bdot(act, wlora_ref[...])
    logw = -DECAY_SCALE * jax.nn.sigmoid(wd0_ref[...] + up_proj[:, 0:B_WIDTH])
    a = jax.nn.sigmoid(a0_ref[...] + up_proj[:, B_WIDTH:2 * B_WIDTH])
    ones_bd = ones_ref[...]
    kk = k * kk_w_ref[...]
    kk = kk / jnp.maximum(jnp.sqrt(_segment_sum(kk * kk, ones_bd)), 1e-12)
    k2 = k * (1.0 + (a - 1.0) * ka_ref[...])
    r_ref[...] = r
    w_ref[...] = jnp.exp(logw)
    k_ref[...] = k2
    v_ref[...] = v
    kk_ref[...] = kk
    b_ref[...] = kk * a
    g_ref[...] = up_proj[:, 2 * B_WIDTH:]
    bonus_ref[...] = _segment_sum(r * k2 * rk_ref[...], ones_bd) * v


def _rwkv_prep(ub, p0, mu, wlora, wd0, a0, kk_w, ka, rk, ones_bd, *, seq_len, time_stride):
    n = ub.shape[0]
    if time_stride == 1:
        rows = TOKEN_TILE
        tiles_per_seq = seq_len // rows
    else:
        rows = n
        tiles_per_seq = 1
    row = lambda c: pl.BlockSpec((rows, c), lambda i: (i, 0))
    vec = _full((1, B_WIDTH))
    return pl.pallas_call(
        functools.partial(_rwkv_prep_kernel, rows=rows, tiles_per_seq=tiles_per_seq,
                          time_stride=time_stride),
        grid=(n // rows,),
        in_specs=[row(B_COLS), _full(p0.shape), _full((1, B_COLS)),
                  _full((LORA_COLS, 3 * B_WIDTH)), vec, vec, vec, vec, vec,
                  _full((B_WIDTH, B_WIDTH))],
        out_specs=[row(B_WIDTH)] * 8,
        out_shape=[jax.ShapeDtypeStruct((n, B_WIDTH), F32)] * 8,
        scratch_shapes=[pltpu.VMEM((1, B_COLS), F32)],
        compiler_params=_params("arbitrary"),
        name="rwkv_prep",
    )(ub, p0, mu, wlora, wd0, a0, kk_w, ka, rk, ones_bd)


def _hgrn_prep_kernel(q_ref, f_ref, lbl_ref, qs_ref, fg_ref, kc_ref):
    logits = lbl_ref[...]
    e = jnp.exp(logits - jnp.max(logits, axis=0, keepdims=True))
    lb = e[0:1, :] / jnp.sum(e, axis=0, keepdims=True)
    fz = f_ref[...]
    fg_ref[...] = lb + (1.0 - lb) * jax.nn.sigmoid(fz)
    kc_ref[...] = (1.0 - lb) * jax.nn.sigmoid(-fz)
    qs_ref[...] = jax.nn.silu(q_ref[...])


def _hgrn_prep(ua, lb_logits):
    n = ua.shape[0]
    tm = TOKEN_TILE
    col = lambda j: pl.BlockSpec((tm, A_WIDTH), lambda i: (i, j))
    return pl.pallas_call(
        _hgrn_prep_kernel,
        grid=(n // tm,),
        in_specs=[col(0), col(1), _full(lb_logits.shape)],
        out_specs=[col(0)] * 3,
        out_shape=[jax.ShapeDtypeStruct((n, A_WIDTH), F32)] * 3,
        compiler_params=_params("arbitrary"),
        name="hgrn_prep",
    )(ua, ua, lb_logits)


def _bcast_row(ref, t, j):
    return jnp.broadcast_to(ref[t, pl.ds(j, 1), :], (V7X_SUBLANES, V7X_LANES))


def _delta_rule_kernel(kk0_ref, kkn_ref, w_ref, b_ref, k_ref, r_ref, v_ref, s0_ref,
                       y_ref, s_ref, sk_ref, *, nib, nj, tb):
    @pl.when(pl.program_id(1) == 0)
    def _():
        s_ref[...] = s0_ref[...]
        for ib in range(nib):
            acc = None
            for j in range(nj):
                kk0 = jnp.broadcast_to(kk0_ref[pl.ds(j, 1), :], (V7X_SUBLANES, V7X_LANES))
                term = s0_ref[ib, j] * kk0
                acc = term if acc is None else acc + term
            sk_ref[ib] = acc

    def step(t, carry):
        sk = [sk_ref[ib] for ib in range(nib)]
        vv = [v_ref[t, ib] for ib in range(nib)]
        yacc = [None] * nib
        skn = [None] * nib
        for j in range(nj):
            wj = _bcast_row(w_ref, t, j)
            bj = _bcast_row(b_ref, t, j)
            kj = _bcast_row(k_ref, t, j)
            rj = _bcast_row(r_ref, t, j)
            nj_kk = _bcast_row(kkn_ref, t, j)
            for ib in range(nib):
                s = s_ref[ib, j] * wj - sk[ib] * bj + vv[ib] * kj
                s_ref[ib, j] = s
                yt = s * rj
                st = s * nj_kk
                yacc[ib] = yt if yacc[ib] is None else yacc[ib] + yt
                skn[ib] = st if skn[ib] is None else skn[ib] + st
        for ib in range(nib):
            y_ref[t, ib] = yacc[ib]
            sk_ref[ib] = skn[ib]
        return carry

    lax.fori_loop(0, tb, step, 0)


def _decay_rule_kernel(w_ref, k_ref, r_ref, v_ref, s0_ref, y_ref, s_ref, *, nib, nj, tb):
    @pl.when(pl.program_id(1) == 0)
    def _():
        s_ref[...] = s0_ref[...]

    def step(t, carry):
        vv = [v_ref[t, ib] for ib in range(nib)]
        yacc = [None] * nib
        for j in range(nj):
            wj = _bcast_row(w_ref, t, j)
            kj = _bcast_row(k_ref, t, j)
            rj = _bcast_row(r_ref, t, j)
            for ib in range(nib):
                s = s_ref[ib, j] * wj + vv[ib] * kj
                s_ref[ib, j] = s
                yt = s * rj
                yacc[ib] = yt if yacc[ib] is None else yacc[ib] + yt
        for ib in range(nib):
            y_ref[t, ib] = yacc[ib]
        return carry

    lax.fori_loop(0, tb, step, 0)


def _recurrence(coefs, v, s0, *, kk0=None, groups_per_coef):
    t_len, q, nib = v.shape[0], v.shape[1], v.shape[2]
    nj = coefs[0].shape[2]
    tb = min(TIME_BLOCK, t_len)
    coef_spec = pl.BlockSpec((tb, None, nj, V7X_LANES),
                             lambda g, t: (t, g // groups_per_coef, 0, 0))
    row_spec = pl.BlockSpec((tb, None, nib, V7X_SUBLANES, V7X_LANES),
                            lambda g, t: (t, g, 0, 0, 0))
    state_spec = pl.BlockSpec((None, nib, nj, V7X_SUBLANES, V7X_LANES),
                              lambda g, t: (g, 0, 0, 0, 0))
    delta = kk0 is not None
    if delta:
        body = functools.partial(_delta_rule_kernel, nib=nib, nj=nj, tb=tb)
        kk0_spec = pl.BlockSpec((None, nj, V7X_LANES), lambda g, t: (g // groups_per_coef, 0, 0))
        in_specs = [kk0_spec] + [coef_spec] * 5 + [row_spec, state_spec]
        args = (kk0,) + tuple(coefs) + (v, s0)
        scratch = [pltpu.VMEM((nib, V7X_SUBLANES, V7X_LANES), F32)]
    else:
        body = functools.partial(_decay_rule_kernel, nib=nib, nj=nj, tb=tb)
        in_specs = [coef_spec] * 3 + [row_spec, state_spec]
        args = tuple(coefs) + (v, s0)
        scratch = []
    return pl.pallas_call(
        body,
        grid=(q, t_len // tb),
        in_specs=in_specs,
        out_specs=[row_spec, state_spec],
        out_shape=[jax.ShapeDtypeStruct(v.shape, F32), jax.ShapeDtypeStruct(s0.shape, F32)],
        scratch_shapes=scratch,
        compiler_params=_params("arbitrary", "arbitrary"),
        name="delta_rule" if delta else "decay_rule",
    )(*args)


def _mix_kernel(x_ref, oa_ref, ga_ref, y_ref, bonus_ref, gb_ref, gate_ref, hw_ref, lnw_ref,
                lnb_ref, onesa_ref, onesb_ref, wa_ref, wb_ref, wo_ref, x1_ref):
    oa = oa_ref[...]
    ms = _segment_sum(oa * oa, onesa_ref[...]) * (1.0 / A_HEAD)
    oa = oa * lax.rsqrt(ms + HGRN_NORM_EPS) * hw_ref[...] * jax.nn.silu(ga_ref[...])
    y = y_ref[...]
    ones_b = onesb_ref[...]
    d = y - _segment_sum(y, ones_b) * (1.0 / B_HEAD)
    var = _segment_sum(d * d, ones_b) * (1.0 / B_HEAD)
    yn = d * lax.rsqrt(var + GN_EPS) * lnw_ref[...] + lnb_ref[...]
    ob = (yn + bonus_ref[...]) * gb_ref[...]
    ya = _bdot(oa, wa_ref[...])
    yb = _bdot(ob, wb_ref[...])
    gate = gate_ref[...]
    merged = gate[:, :D_MODEL] * ya + gate[:, D_MODEL:] * yb
    x1_ref[...] = x_ref[...] + _bdot(merged, wo_ref[...])


def _mix(x, oa, ua, y, bonus, gb, gate, hw, lnw, lnb, ones_a, ones_b, wa, wb, wo):
    n = x.shape[0]
    tm = TOKEN_TILE
    row = lambda c: pl.BlockSpec((tm, c), lambda i: (i, 0))
    vec = _full((1, A_WIDTH))
    return pl.pallas_call(
        _mix_kernel,
        grid=(n // tm,),
        in_specs=[row(D_MODEL), row(A_WIDTH), pl.BlockSpec((tm, A_WIDTH), lambda i: (i, 3)),
                  row(B_WIDTH), row(B_WIDTH), row(B_WIDTH), row(GATE_COLS), vec, vec, vec,
                  _full((A_WIDTH, A_WIDTH)), _full((B_WIDTH, B_WIDTH)),
                  _full((A_WIDTH, D_MODEL)), _full((B_WIDTH, D_MODEL)),
                  _full((D_MODEL, D_MODEL))],
        out_specs=row(D_MODEL),
        out_shape=jax.ShapeDtypeStruct((n, D_MODEL), F32),
        compiler_params=_params("arbitrary"),
        name="mix",
    )(x, oa, ua, y, bonus, gb, gate, hw, lnw, lnb, ones_a, ones_b, wa, wb, wo)


def _mlp_kernel(x_ref, g_ref, wu_ref, wd_ref, gf_ref, o_ref):
    x1 = x_ref[...]
    up = _bdot(_rmsnorm(x1, g_ref[...]), wu_ref[...])
    x2 = x1 + _bdot(jnp.square(jnp.maximum(up, 0.0)), wd_ref[...])
    o_ref[...] = _rmsnorm(x2, gf_ref[...])


def _mlp(x1, g, wu, wd, gf):
    n = x1.shape[0]
    tm = TOKEN_TILE
    row = pl.BlockSpec((tm, D_MODEL), lambda i: (i, 0))
    return pl.pallas_call(
        _mlp_kernel,
        grid=(n // tm,),
        in_specs=[row, _full((1, D_MODEL)), _full((D_MODEL, D_FF)), _full((D_FF, D_MODEL)),
                  _full((1, D_MODEL))],
        out_specs=row,
        out_shape=jax.ShapeDtypeStruct((n, D_MODEL), F32),
        compiler_params=_params("arbitrary"),
        name="mlp",
    )(x1, g, wu, wd, gf)


class _ChainLayout:
    def __init__(self, batch, seq, heads, nj, ni, time_major):
        self.b, self.t, self.h, self.nj, self.ni = batch, seq, heads, nj, ni
        self.time_major = time_major
        if time_major:
            assert batch == V7X_LANES
            self.parts = 1
            self.rows = ROWS_PER_GROUP
            self.groups_per_coef = ni // self.rows
        else:
            self.parts = V7X_LANES // (batch * heads)
            assert self.parts * batch * heads == V7X_LANES and ni % (self.parts * V7X_SUBLANES) == 0
            self.rows = ni // self.parts
            self.groups_per_coef = 1
        self.nib = self.rows // V7X_SUBLANES

    def coef(self, x):
        b, t, h, nj, p = self.b, self.t, self.h, self.nj, self.parts
        if self.time_major:
            return x.reshape(t, b, h, nj).transpose(0, 2, 3, 1)
        x = x.reshape(b, t, h, nj).transpose(1, 3, 0, 2)
        return jnp.broadcast_to(x[:, :, None], (t, nj, p, b, h)).reshape(t, 1, nj, V7X_LANES)

    def rows_in(self, v):
        b, t, h, ni, p = self.b, self.t, self.h, self.ni, self.parts
        if self.time_major:
            v = v.reshape(t, b, h, ni).transpose(0, 2, 3, 1)
            return v.reshape(t, h * self.groups_per_coef, self.nib, V7X_SUBLANES, V7X_LANES)
        v = v.reshape(b, t, h, p, self.rows).transpose(1, 4, 3, 0, 2)
        return v.reshape(t, 1, self.nib, V7X_SUBLANES, V7X_LANES)

    def rows_out(self, y):
        b, t, h, ni, p = self.b, self.t, self.h, self.ni, self.parts
        if self.time_major:
            return y.reshape(t, h, ni, b).transpose(0, 3, 1, 2).reshape(t * b, h * ni)
        return y.reshape(t, self.rows, p, b, h).transpose(3, 0, 4, 2, 1).reshape(b * t, h * ni)

    def state_in(self, s, rows_last):
        b, h, ni, nj, p = self.b, self.h, self.ni, self.nj, self.parts
        if not rows_last:
            s = s.transpose(0, 1, 3, 2)
        if self.time_major:
            s = s.reshape(b, h, nj, self.groups_per_coef, self.nib, V7X_SUBLANES)
            s = s.transpose(1, 3, 4, 2, 5, 0)
            return s.reshape(h * self.groups_per_coef, self.nib, nj, V7X_SUBLANES, V7X_LANES)
        s = s.reshape(b, h, nj, p, self.nib, V7X_SUBLANES).transpose(4, 2, 5, 3, 0, 1)
        return s.reshape(1, self.nib, nj, V7X_SUBLANES, V7X_LANES)

    def state_out(self, s, rows_last):
        b, h, ni, nj, p = self.b, self.h, self.ni, self.nj, self.parts
        if self.time_major:
            s = s.reshape(h, self.groups_per_coef, self.nib, nj, V7X_SUBLANES, b)
            s = s.transpose(5, 0, 3, 1, 2, 4).reshape(b, h, nj, ni)
        else:
            s = s.reshape(self.nib, nj, V7X_SUBLANES, p, b, h)
            s = s.transpose(4, 5, 1, 3, 0, 2).reshape(b, h, nj, ni)
        return s if rows_last else s.transpose(0, 1, 3, 2)


def _block_diag_ones(width, head):
    idx = jnp.arange(width) // head
    return (idx[:, None] == idx[None, :]).astype(BF16)


def _trunk(x, shift0, state_a, state_b, wts, *, time_major):
    batch, seq, _ = x.shape
    n = batch * seq
    if time_major:
        x2 = x.transpose(1, 0, 2).reshape(n, D_MODEL)
        time_stride = batch
    else:
        x2 = x.reshape(n, D_MODEL)
        time_stride = 1

    ua, ub, gate = _norm_proj(x2, wts["norm_mix_g"], wts["w_in_a"], wts["w_in_b"], wts["w_in_g"])
    p0 = _prev_proj(shift0, wts["w_in_b"])
    r, w, k2, v, kk, bco, gb, bonus = _rwkv_prep(
        ub, p0, wts["mu_shift"], wts["w_lora"], wts["w_decay0"], wts["a0"], wts["k_k"],
        wts["k_a"], wts["r_k"], wts["ones_b"], seq_len=seq, time_stride=time_stride)
    qs, fg, kc = _hgrn_prep(ua, wts["lb_logits"])

    lay_b = _ChainLayout(batch, seq, B_HEADS, B_HEAD, B_HEAD, time_major)
    kk_c = lay_b.coef(kk)
    kk_next = jnp.concatenate([kk_c[1:], jnp.zeros_like(kk_c[:1])], axis=0)
    y_c, sb_c = _recurrence(
        (kk_next, lay_b.coef(w), lay_b.coef(bco), lay_b.coef(k2), lay_b.coef(r)),
        lay_b.rows_in(v), lay_b.state_in(state_b, rows_last=False), kk0=kk_c[0],
        groups_per_coef=lay_b.groups_per_coef)
    y_b = lay_b.rows_out(y_c)
    new_wkv = lay_b.state_out(sb_c, rows_last=False)

    lay_a = _ChainLayout(batch, seq, A_HEADS, A_HEAD, A_HEAD, time_major)
    o_c, sa_c = _recurrence(
        (lay_a.coef(fg), lay_a.coef(kc), lay_a.coef(qs)),
        lay_a.rows_in(ua[:, 2 * A_WIDTH:3 * A_WIDTH]), lay_a.state_in(state_a, rows_last=True),
        groups_per_coef=lay_a.groups_per_coef)
    o_a = lay_a.rows_out(o_c)
    new_hgrn = lay_a.state_out(sa_c, rows_last=True)

    x1 = _mix(x2, o_a, ua, y_b, bonus, gb, gate, wts["hgrn_norm_w"], wts["ln_x_w"],
              wts["ln_x_b"], wts["ones_a"], wts["ones_b"], wts["w_a_out"], wts["w_b_out"],
              wts["w_out"])
    y = _mlp(x1, wts["norm_mlp_g"], wts["w_up"], wts["w_down"], wts["norm_final_g"])
    if time_major:
        y = y.reshape(seq, batch, D_MODEL).transpose(1, 0, 2)
    else:
        y = y.reshape(batch, seq, D_MODEL)
    new_shift = _norm_rows(x[:, -1, :], wts["norm_mix_g"])
    return y, new_hgrn[None], new_wkv[None], new_shift[None]


def kernel(x_prompt, x_sample, state_hgrn, state_wkv, state_shift, norm_mix_g, w_in, mu_shift,
           w_decay0, w_decay_up, a0, w_aaa_up, w_gate_up, k_k, k_a, r_k, ln_x_w, ln_x_b,
           lb_logits, hgrn_norm_w, w_a_out, w_b_out, w_out, norm_mlp_g, w_up, w_down,
           norm_final_g):
    assert w_in.shape[0] == 1, "single-layer stack"
    w_in0 = w_in[0].astype(BF16)
    w_lora = jnp.zeros((LORA_COLS, 3 * B_WIDTH), F32)
    w_lora = w_lora.at[:DECAY_LORA, :B_WIDTH].set(w_decay_up[0])
    w_lora = w_lora.at[DECAY_LORA:DECAY_LORA + AAA_LORA, B_WIDTH:2 * B_WIDTH].set(w_aaa_up[0])
    w_lora = w_lora.at[DECAY_LORA + AAA_LORA:, 2 * B_WIDTH:].set(w_gate_up[0])
    row = lambda a: a.reshape(1, -1).astype(F32)
    wts = {
        "norm_mix_g": row(norm_mix_g[0]),
        "w_in_a": w_in0[:, :A_COLS],
        "w_in_b": w_in0[:, A_COLS:A_COLS + B_COLS],
        "w_in_g": w_in0[:, A_COLS + B_COLS:],
        "mu_shift": row(mu_shift[0]),
        "w_lora": w_lora.astype(BF16),
        "w_decay0": row(w_decay0[0]),
        "a0": row(a0[0]),
        "k_k": row(k_k[0]),
        "k_a": row(k_a[0]),
        "r_k": row(r_k[0]),
        "ln_x_w": row(ln_x_w[0]),
        "ln_x_b": row(ln_x_b[0]),
        "lb_logits": lb_logits.astype(F32),
        "hgrn_norm_w": row(hgrn_norm_w[0]),
        "w_a_out": w_a_out[0].astype(BF16),
        "w_b_out": w_b_out[0].astype(BF16),
        "w_out": w_out[0].astype(BF16),
        "norm_mlp_g": row(norm_mlp_g[0]),
        "w_up": w_up[0].astype(BF16),
        "w_down": w_down[0].astype(BF16),
        "norm_final_g": row(norm_final_g),
        "ones_a": _block_diag_ones(A_WIDTH, A_HEAD),
        "ones_b": _block_diag_ones(B_WIDTH, B_HEAD),
    }
    bp = x_prompt.shape[0]
    y_p, hgrn_p, wkv_p, shift_p = _trunk(
        x_prompt, jnp.zeros((bp, D_MODEL), F32),
        jnp.zeros((bp, A_HEADS, A_HEAD, A_HEAD), F32),
        jnp.zeros((bp, B_HEADS, B_HEAD, B_HEAD), F32), wts, time_major=False)
    y_s, hgrn_s, wkv_s, shift_s = _trunk(
        x_sample, state_shift[0], state_hgrn[0], state_wkv[0], wts, time_major=True)
    return (y_p, y_s, hgrn_p, wkv_p, shift_p, hgrn_s, wkv_s, shift_s)
```

```python
import functools

import jax
import jax.numpy as jnp
import numpy as np
from jax import lax
from jax.experimental import pallas as pl
from jax.experimental.pallas import tpu as pltpu

F32 = jnp.float32
BF16 = jnp.bfloat16

D_MODEL = 1024
A_WIDTH = 512
A_HEADS = 4
A_HEAD = 128
B_WIDTH = 512
B_HEADS = 8
B_HEAD = 64
DECAY_LORA = 64
AAA_LORA = 64
GATE_LORA = 128
LORA_COLS = DECAY_LORA + AAA_LORA + GATE_LORA
D_FF = 4 * D_MODEL
A_COLS = 4 * A_WIDTH
B_COLS = 3 * B_WIDTH + LORA_COLS
GATE_COLS = 2 * D_MODEL
NORM_EPS = 1e-6
HGRN_NORM_EPS = 1e-5
GN_EPS = 64e-5
DECAY_SCALE = 0.6065306597126334

V7X_LANES = 128
V7X_SUBLANES = 8
V7X_VMEM_LIMIT_BYTES = 56 * 1024 * 1024

TOKEN_TILE = 256
TIME_BLOCK = 64
ROWS_PER_GROUP = 64


def _params(*semantics):
    return pltpu.CompilerParams(dimension_semantics=semantics,
                                vmem_limit_bytes=V7X_VMEM_LIMIT_BYTES)


def _full(shape):
    return pl.BlockSpec(shape, lambda *_: (0,) * len(shape))


def _rmsnorm(x, g):
    return x * lax.rsqrt(jnp.mean(x * x, axis=-1, keepdims=True) + NORM_EPS) * g


def _bdot(a, w):
    return jnp.dot(a.astype(BF16), w, preferred_element_type=F32)


def _segment_sum(a, ones_bd):
    hi = a.astype(BF16)
    lo = (a - hi.astype(F32)).astype(BF16)
    return (jnp.dot(hi, ones_bd, preferred_element_type=F32)
            + jnp.dot(lo, ones_bd, preferred_element_type=F32))


def _norm_proj_kernel(x_ref, g_ref, wa_ref, wb_ref, wg_ref, ua_ref, ub_ref, gate_ref):
    hb = _rmsnorm(x_ref[...], g_ref[...]).astype(BF16)
    ua_ref[...] = jnp.dot(hb, wa_ref[...], preferred_element_type=F32)
    ub_ref[...] = jnp.dot(hb, wb_ref[...], preferred_element_type=F32)
    gate_ref[...] = jax.nn.sigmoid(jnp.dot(hb, wg_ref[...], preferred_element_type=F32))


def _norm_proj(x, g, wa, wb, wg):
    n = x.shape[0]
    tm = TOKEN_TILE
    row = lambda c: pl.BlockSpec((tm, c), lambda i: (i, 0))
    return pl.pallas_call(
        _norm_proj_kernel,
        grid=(n // tm,),
        in_specs=[row(D_MODEL), _full((1, D_MODEL)), _full((D_MODEL, A_COLS)),
                  _full((D_MODEL, B_COLS)), _full((D_MODEL, GATE_COLS))],
        out_specs=[row(A_COLS), row(B_COLS), row(GATE_COLS)],
        out_shape=[jax.ShapeDtypeStruct((n, A_COLS), F32),
                   jax.ShapeDtypeStruct((n, B_COLS), F32),
                   jax.ShapeDtypeStruct((n, GATE_COLS), F32)],
        compiler_params=_params("arbitrary"),
        name="norm_proj",
    )(x, g, wa, wb, wg)


def _norm_rows_kernel(x_ref, g_ref, o_ref):
    o_ref[...] = _rmsnorm(x_ref[...], g_ref[...])


def _norm_rows(x, g):
    return pl.pallas_call(
        _norm_rows_kernel,
        out_shape=jax.ShapeDtypeStruct(x.shape, F32),
        name="norm_rows",
    )(x, g)


def _prev_proj_kernel(h_ref, w_ref, o_ref):
    o_ref[...] = _bdot(h_ref[...], w_ref[...])


def _prev_proj(h_prev, wb):
    return pl.pallas_call(
        _prev_proj_kernel,
        out_shape=jax.ShapeDtypeStruct((h_prev.shape[0], B_COLS), F32),
        compiler_params=_params(),
        name="prev_proj",
    )(h_prev, wb)


def _rwkv_prep_kernel(ub_ref, p0_ref, mu_ref, wlora_ref, wd0_ref, a0_ref, kk_w_ref, ka_ref,
                      rk_ref, ones_ref, r_ref, w_ref, k_ref, v_ref, kk_ref, b_ref, g_ref,
                      bonus_ref, carry_ref, *, rows, tiles_per_seq, time_stride):
    ub = ub_ref[...]
    if time_stride == 1:
        i = pl.program_id(0)
        seq = i // tiles_per_seq
        first = jnp.where(i % tiles_per_seq == 0, p0_ref[pl.ds(seq, 1), :], carry_ref[...])
        row_id = lax.broadcasted_iota(jnp.int32, ub.shape, 0)
        up = jnp.where(row_id == 0, first, pltpu.roll(ub, 1, 0))
        carry_ref[...] = ub[rows - 1:rows, :]
    else:
        up = jnp.concatenate([p0_ref[...], ub[:rows - time_stride, :]], axis=0)
    xm = ub + (up - ub) * mu_ref[...]
    r = xm[:, 0:B_WIDTH]
    k = xm[:, B_WIDTH:2 * B_WIDTH]
    v = xm[:, 2 * B_WIDTH:3 * B_WIDTH]
    lo = xm[:, 3 * B_WIDTH:]
    col = lax.broadcasted_iota(jnp.int32, lo.shape, 1)
    act = jnp.where(col < DECAY_LORA, jnp.tanh(lo),
                    jnp.where(col < DECAY_LORA + AAA_LORA, lo, jax.nn.sigmoid(lo)))
    up_proj = _bdot(act, wlora_ref[...])
    logw = -DECAY_SCALE * jax.nn.sigmoid(wd0_ref[...] + up_proj[:, 0:B_WIDTH])
    a = jax.nn.sigmoid(a0_ref[...] + up_proj[:, B_WIDTH:2 * B_WIDTH])
    ones_bd = ones_ref[...]
    kk = k * kk_w_ref[...]
    kk = kk / jnp.maximum(jnp.sqrt(_segment_sum(kk * kk, ones_bd)), 1e-12)
    k2 = k * (1.0 + (a - 1.0) * ka_ref[...])
    r_ref[...] = r
    w_ref[...] = jnp.exp(logw)
    k_ref[...] = k2
    v_ref[...] = v
    kk_ref[...] = kk
    b_ref[...] = kk * a
    g_ref[...] = up_proj[:, 2 * B_WIDTH:]
    bonus_ref[...] = _segment_sum(r * k2 * rk_ref[...], ones_bd) * v


def _rwkv_prep(ub, p0, mu, wlora, wd0, a0, kk_w, ka, rk, ones_bd, *, seq_len, time_stride):
    n = ub.shape[0]
    if time_stride == 1:
        rows = TOKEN_TILE
        tiles_per_seq = seq_len // rows
    else:
        rows = n
        tiles_per_seq = 1
    row = lambda c: pl.BlockSpec((rows, c), lambda i: (i, 0))
    vec = _full((1, B_WIDTH))
    return pl.pallas_call(
        functools.partial(_rwkv_prep_kernel, rows=rows, tiles_per_seq=tiles_per_seq,
                          time_stride=time_stride),
        grid=(n // rows,),
        in_specs=[row(B_COLS), _full(p0.shape), _full((1, B_COLS)),
                  _full((LORA_COLS, 3 * B_WIDTH)), vec, vec, vec, vec, vec,
                  _full((B_WIDTH, B_WIDTH))],
        out_specs=[row(B_WIDTH)] * 8,
        out_shape=[jax.ShapeDtypeStruct((n, B_WIDTH), F32)] * 8,
        scratch_shapes=[pltpu.VMEM((1, B_COLS), F32)],
        compiler_params=_params("arbitrary"),
        name="rwkv_prep",
    )(ub, p0, mu, wlora, wd0, a0, kk_w, ka, rk, ones_bd)


def _hgrn_prep_kernel(q_ref, f_ref, lbl_ref, qs_ref, fg_ref, kc_ref):
    logits = lbl_ref[...]
    e = jnp.exp(logits - jnp.max(logits, axis=0, keepdims=True))
    lb = e[0:1, :] / jnp.sum(e, axis=0, keepdims=True)
    fz = f_ref[...]
    fg_ref[...] = lb + (1.0 - lb) * jax.nn.sigmoid(fz)
    kc_ref[...] = (1.0 - lb) * jax.nn.sigmoid(-fz)
    qs_ref[...] = jax.nn.silu(q_ref[...])


def _hgrn_prep(ua, lb_logits):
    n = ua.shape[0]
    tm = TOKEN_TILE
    col = lambda j: pl.BlockSpec((tm, A_WIDTH), lambda i: (i, j))
    return pl.pallas_call(
        _hgrn_prep_kernel,
        grid=(n // tm,),
        in_specs=[col(0), col(1), _full(lb_logits.shape)],
        out_specs=[col(0)] * 3,
        out_shape=[jax.ShapeDtypeStruct((n, A_WIDTH), F32)] * 3,
        compiler_params=_params("arbitrary"),
        name="hgrn_prep",
    )(ua, ua, lb_logits)


HGRN_CHUNK = 64


def _hgrn_tables(chunk):
    levels = chunk.bit_length() - 1
    t = np.arange(chunk)
    u, tt = t[None, :], t[:, None]
    rows = [u <= tt, u > tt]
    masks = []
    for level in range(levels):
        m = 1 << level
        anchor = (t // (2 * m)) * (2 * m) + m - 1
        right = (t % (2 * m)) >= m
        rows.append((u > anchor[:, None]) & (u <= tt) & right[:, None])
        rows.append((u > tt) & (u <= anchor[:, None]) & ~right[:, None])
        masks.append((tt // (2 * m) == u // (2 * m)) & right[:, None] & ~right[None, :])
    return (jnp.asarray(np.concatenate(rows, 0), BF16),
            jnp.asarray(np.stack(masks), F32), levels)


def _split3(x):
    hi = x.astype(BF16)
    r1 = x - hi.astype(F32)
    mid = r1.astype(BF16)
    lo = (r1 - mid.astype(F32)).astype(BF16)
    return hi, mid, lo


def _dot_nt(a, b):
    return lax.dot_general(a.astype(BF16), b.astype(BF16), (((1,), (1,)), ((), ())),
                           preferred_element_type=F32)


def _dot_tn(a, b):
    return lax.dot_general(a.astype(BF16), b.astype(BF16), (((0,), (0,)), ((), ())),
                           preferred_element_type=F32)


def _hgrn_chunk_kernel(q_ref, f_ref, i_ref, lbl_ref, sums_ref, mask_ref, s0_ref, o_ref,
                       s_out_ref, st_ref, *, chunk, n_chunks, levels):
    @pl.when(pl.program_id(1) == 0)
    def _():
        for h in range(A_HEADS):
            st_ref[h] = s0_ref[h].T

    logits = lbl_ref[...]
    e = jnp.exp(logits - jnp.max(logits, axis=0, keepdims=True))
    lb = e[0:1, :] / jnp.sum(e, axis=0, keepdims=True)
    sums = sums_ref[...]
    for c in range(n_chunks):
        rows = pl.ds(c * chunk, chunk)
        fz = f_ref[rows, :]
        logf = jnp.log(lb + (1.0 - lb) * jax.nn.sigmoid(fz))
        kc = (1.0 - lb) * jax.nn.sigmoid(-fz)
        qs = jax.nn.silu(q_ref[rows, :])
        vi = i_ref[rows, :]
        decay = jnp.exp(sum(jnp.dot(sums, part, preferred_element_type=F32)
                            for part in _split3(logf)))
        seg = lambda r: slice(r * chunk, (r + 1) * chunk)
        for h in range(A_HEADS):
            hs = slice(h * A_HEAD, (h + 1) * A_HEAD)
            qh, kh, vh = qs[:, hs], kc[:, hs], vi[:, hs]
            att = None
            for level in range(levels):
                ql = qh * decay[seg(2 + 2 * level), hs]
                kl = kh * decay[seg(3 + 2 * level), hs]
                term = mask_ref[level] * _dot_nt(ql, kl)
                att = term if att is None else att + term
            st = st_ref[h]
            o = (_bdot(att, vh.astype(BF16))
                 + jnp.sum(qh * kh, axis=-1, keepdims=True) * vh
                 + _dot_nt(qh * decay[seg(0), hs], st))
            o_ref[rows, hs] = o
            total = decay[chunk - 1:chunk, hs]
            st_ref[h] = st * total + _dot_tn(vh, kh * decay[seg(1), hs])

    @pl.when(pl.program_id(1) == pl.num_programs(1) - 1)
    def _():
        for h in range(A_HEADS):
            s_out_ref[h] = st_ref[h].T


def _hgrn_chunked(ua, lb_logits, s0, *, batch, seq):
    chunk = HGRN_CHUNK
    tile = TOKEN_TILE
    sums, masks, levels = _hgrn_tables(chunk)
    tiles_per_seq = seq // tile
    col = lambda j: pl.BlockSpec((tile, A_WIDTH), lambda b, t: (b * tiles_per_seq + t, j))
    state_spec = pl.BlockSpec((None, A_HEADS, A_HEAD, A_HEAD), lambda b, t: (b, 0, 0, 0))
    return pl.pallas_call(
        functools.partial(_hgrn_chunk_kernel, chunk=chunk, n_chunks=tile // chunk, levels=levels),
        grid=(batch, tiles_per_seq),
        in_specs=[col(0), col(1), col(2), _full(lb_logits.shape), _full(sums.shape),
                  _full(masks.shape), state_spec],
        out_specs=[col(0), state_spec],
        out_shape=[jax.ShapeDtypeStruct((batch * seq, A_WIDTH), F32),
                   jax.ShapeDtypeStruct(s0.shape, F32)],
        scratch_shapes=[pltpu.VMEM((A_HEADS, A_HEAD, A_HEAD), F32)],
        compiler_params=_params("arbitrary", "arbitrary"),
        name="hgrn_chunked",
    )(ua, ua, ua, lb_logits, sums, masks, s0)


def _bcast_row(ref, t, j):
    return jnp.broadcast_to(ref[t, pl.ds(j, 1), :], (V7X_SUBLANES, V7X_LANES))


def _delta_rule_kernel(kk0_ref, kkn_ref, w_ref, b_ref, k_ref, r_ref, v_ref, s0_ref,
                       y_ref, s_ref, sk_ref, *, nib, nj, tb):
    @pl.when(pl.program_id(1) == 0)
    def _():
        s_ref[...] = s0_ref[...]
        for ib in range(nib):
            acc = None
            for j in range(nj):
                kk0 = jnp.broadcast_to(kk0_ref[pl.ds(j, 1), :], (V7X_SUBLANES, V7X_LANES))
                term = s0_ref[ib, j] * kk0
                acc = term if acc is None else acc + term
            sk_ref[ib] = acc

    def step(t, carry):
        sk = [sk_ref[ib] for ib in range(nib)]
        vv = [v_ref[t, ib] for ib in range(nib)]
        yacc = [None] * nib
        skn = [None] * nib
        for j in range(nj):
            wj = _bcast_row(w_ref, t, j)
            bj = _bcast_row(b_ref, t, j)
            kj = _bcast_row(k_ref, t, j)
            rj = _bcast_row(r_ref, t, j)
            nj_kk = _bcast_row(kkn_ref, t, j)
            for ib in range(nib):
                s = s_ref[ib, j] * wj - sk[ib] * bj + vv[ib] * kj
                s_ref[ib, j] = s
                yt = s * rj
                st = s * nj_kk
                yacc[ib] = yt if yacc[ib] is None else yacc[ib] + yt
                skn[ib] = st if skn[ib] is None else skn[ib] + st
        for ib in range(nib):
            y_ref[t, ib] = yacc[ib]
            sk_ref[ib] = skn[ib]
        return carry

    lax.fori_loop(0, tb, step, 0)


def _decay_rule_kernel(w_ref, k_ref, r_ref, v_ref, s0_ref, y_ref, s_ref, *, nib, nj, tb):
    @pl.when(pl.program_id(1) == 0)
    def _():
        s_ref[...] = s0_ref[...]

    def step(t, carry):
        vv = [v_ref[t, ib] for ib in range(nib)]
        yacc = [None] * nib
        for j in range(nj):
            wj = _bcast_row(w_ref, t, j)
            kj = _bcast_row(k_ref, t, j)
            rj = _bcast_row(r_ref, t, j)
            for ib in range(nib):
                s = s_ref[ib, j] * wj + vv[ib] * kj
                s_ref[ib, j] = s
                yt = s * rj
                yacc[ib] = yt if yacc[ib] is None else yacc[ib] + yt
        for ib in range(nib):
            y_ref[t, ib] = yacc[ib]
        return carry

    lax.fori_loop(0, tb, step, 0)


def _recurrence(coefs, v, s0, *, kk0=None, groups_per_coef):
    t_len, q, nib = v.shape[0], v.shape[1], v.shape[2]
    nj = coefs[0].shape[2]
    tb = min(TIME_BLOCK, t_len)
    coef_spec = pl.BlockSpec((tb, None, nj, V7X_LANES),
                             lambda g, t: (t, g // groups_per_coef, 0, 0))
    row_spec = pl.BlockSpec((tb, None, nib, V7X_SUBLANES, V7X_LANES),
                            lambda g, t: (t, g, 0, 0, 0))
    state_spec = pl.BlockSpec((None, nib, nj, V7X_SUBLANES, V7X_LANES),
                              lambda g, t: (g, 0, 0, 0, 0))
    delta = kk0 is not None
    if delta:
        body = functools.partial(_delta_rule_kernel, nib=nib, nj=nj, tb=tb)
        kk0_spec = pl.BlockSpec((None, nj, V7X_LANES), lambda g, t: (g // groups_per_coef, 0, 0))
        in_specs = [kk0_spec] + [coef_spec] * 5 + [row_spec, state_spec]
        args = (kk0,) + tuple(coefs) + (v, s0)
        scratch = [pltpu.VMEM((nib, V7X_SUBLANES, V7X_LANES), F32)]
    else:
        body = functools.partial(_decay_rule_kernel, nib=nib, nj=nj, tb=tb)
        in_specs = [coef_spec] * 3 + [row_spec, state_spec]
        args = tuple(coefs) + (v, s0)
        scratch = []
    return pl.pallas_call(
        body,
        grid=(q, t_len // tb),
        in_specs=in_specs,
        out_specs=[row_spec, state_spec],
        out_shape=[jax.ShapeDtypeStruct(v.shape, F32), jax.ShapeDtypeStruct(s0.shape, F32)],
        scratch_shapes=scratch,
        compiler_params=_params("arbitrary", "arbitrary"),
        name="delta_rule" if delta else "decay_rule",
    )(*args)


def _mix_kernel(x_ref, oa_ref, ga_ref, y_ref, bonus_ref, gb_ref, gate_ref, hw_ref, lnw_ref,
                lnb_ref, onesa_ref, onesb_ref, wa_ref, wb_ref, wo_ref, x1_ref):
    oa = oa_ref[...]
    ms = _segment_sum(oa * oa, onesa_ref[...]) * (1.0 / A_HEAD)
    oa = oa * lax.rsqrt(ms + HGRN_NORM_EPS) * hw_ref[...] * jax.nn.silu(ga_ref[...])
    y = y_ref[...]
    ones_b = onesb_ref[...]
    d = y - _segment_sum(y, ones_b) * (1.0 / B_HEAD)
    var = _segment_sum(d * d, ones_b) * (1.0 / B_HEAD)
    yn = d * lax.rsqrt(var + GN_EPS) * lnw_ref[...] + lnb_ref[...]
    ob = (yn + bonus_ref[...]) * gb_ref[...]
    ya = _bdot(oa, wa_ref[...])
    yb = _bdot(ob, wb_ref[...])
    gate = gate_ref[...]
    merged = gate[:, :D_MODEL] * ya + gate[:, D_MODEL:] * yb
    x1_ref[...] = x_ref[...] + _bdot(merged, wo_ref[...])


def _mix(x, oa, ua, y, bonus, gb, gate, hw, lnw, lnb, ones_a, ones_b, wa, wb, wo):
    n = x.shape[0]
    tm = TOKEN_TILE
    row = lambda c: pl.BlockSpec((tm, c), lambda i: (i, 0))
    vec = _full((1, A_WIDTH))
    return pl.pallas_call(
        _mix_kernel,
        grid=(n // tm,),
        in_specs=[row(D_MODEL), row(A_WIDTH), pl.BlockSpec((tm, A_WIDTH), lambda i: (i, 3)),
                  row(B_WIDTH), row(B_WIDTH), row(B_WIDTH), row(GATE_COLS), vec, vec, vec,
                  _full((A_WIDTH, A_WIDTH)), _full((B_WIDTH, B_WIDTH)),
                  _full((A_WIDTH, D_MODEL)), _full((B_WIDTH, D_MODEL)),
                  _full((D_MODEL, D_MODEL))],
        out_specs=row(D_MODEL),
        out_shape=jax.ShapeDtypeStruct((n, D_MODEL), F32),
        compiler_params=_params("arbitrary"),
        name="mix",
    )(x, oa, ua, y, bonus, gb, gate, hw, lnw, lnb, ones_a, ones_b, wa, wb, wo)


def _mlp_kernel(x_ref, g_ref, wu_ref, wd_ref, gf_ref, o_ref):
    x1 = x_ref[...]
    up = _bdot(_rmsnorm(x1, g_ref[...]), wu_ref[...])
    x2 = x1 + _bdot(jnp.square(jnp.maximum(up, 0.0)), wd_ref[...])
    o_ref[...] = _rmsnorm(x2, gf_ref[...])


def _mlp(x1, g, wu, wd, gf):
    n = x1.shape[0]
    tm = TOKEN_TILE
    row = pl.BlockSpec((tm, D_MODEL), lambda i: (i, 0))
    return pl.pallas_call(
        _mlp_kernel,
        grid=(n // tm,),
        in_specs=[row, _full((1, D_MODEL)), _full((D_MODEL, D_FF)), _full((D_FF, D_MODEL)),
                  _full((1, D_MODEL))],
        out_specs=row,
        out_shape=jax.ShapeDtypeStruct((n, D_MODEL), F32),
        compiler_params=_params("arbitrary"),
        name="mlp",
    )(x1, g, wu, wd, gf)


class _ChainLayout:
    def __init__(self, batch, seq, heads, nj, ni, time_major):
        self.b, self.t, self.h, self.nj, self.ni = batch, seq, heads, nj, ni
        self.time_major = time_major
        if time_major:
            assert batch == V7X_LANES
            self.parts = 1
            self.rows = ROWS_PER_GROUP
            self.groups_per_coef = ni // self.rows
        else:
            self.parts = V7X_LANES // (batch * heads)
            assert self.parts * batch * heads == V7X_LANES and ni % (self.parts * V7X_SUBLANES) == 0
            self.rows = ni // self.parts
            self.groups_per_coef = 1
        self.nib = self.rows // V7X_SUBLANES

    def coef(self, x):
        b, t, h, nj, p = self.b, self.t, self.h, self.nj, self.parts
        if self.time_major:
            return x.reshape(t, b, h, nj).transpose(0, 2, 3, 1)
        x = x.reshape(b, t, h, nj).transpose(1, 3, 0, 2)
        return jnp.broadcast_to(x[:, :, None], (t, nj, p, b, h)).reshape(t, 1, nj, V7X_LANES)

    def rows_in(self, v):
        b, t, h, ni, p = self.b, self.t, self.h, self.ni, self.parts
        if self.time_major:
            v = v.reshape(t, b, h, ni).transpose(0, 2, 3, 1)
            return v.reshape(t, h * self.groups_per_coef, self.nib, V7X_SUBLANES, V7X_LANES)
        v = v.reshape(b, t, h, p, self.rows).transpose(1, 4, 3, 0, 2)
        return v.reshape(t, 1, self.nib, V7X_SUBLANES, V7X_LANES)

    def rows_out(self, y):
        b, t, h, ni, p = self.b, self.t, self.h, self.ni, self.parts
        if self.time_major:
            return y.reshape(t, h, ni, b).transpose(0, 3, 1, 2).reshape(t * b, h * ni)
        return y.reshape(t, self.rows, p, b, h).transpose(3, 0, 4, 2, 1).reshape(b * t, h * ni)

    def state_in(self, s, rows_last):
        b, h, ni, nj, p = self.b, self.h, self.ni, self.nj, self.parts
        if not rows_last:
            s = s.transpose(0, 1, 3, 2)
        if self.time_major:
            s = s.reshape(b, h, nj, self.groups_per_coef, self.nib, V7X_SUBLANES)
            s = s.transpose(1, 3, 4, 2, 5, 0)
            return s.reshape(h * self.groups_per_coef, self.nib, nj, V7X_SUBLANES, V7X_LANES)
        s = s.reshape(b, h, nj, p, self.nib, V7X_SUBLANES).transpose(4, 2, 5, 3, 0, 1)
        return s.reshape(1, self.nib, nj, V7X_SUBLANES, V7X_LANES)

    def state_out(self, s, rows_last):
        b, h, ni, nj, p = self.b, self.h, self.ni, self.nj, self.parts
        if self.time_major:
            s = s.reshape(h, self.groups_per_coef, self.nib, nj, V7X_SUBLANES, b)
            s = s.transpose(5, 0, 3, 1, 2, 4).reshape(b, h, nj, ni)
        else:
            s = s.reshape(self.nib, nj, V7X_SUBLANES, p, b, h)
            s = s.transpose(4, 5, 1, 3, 0, 2).reshape(b, h, nj, ni)
        return s if rows_last else s.transpose(0, 1, 3, 2)


def _block_diag_ones(width, head):
    idx = jnp.arange(width) // head
    return (idx[:, None] == idx[None, :]).astype(BF16)


def _trunk(x, shift0, state_a, state_b, wts, *, time_major):
    batch, seq, _ = x.shape
    n = batch * seq
    if time_major:
        x2 = x.transpose(1, 0, 2).reshape(n, D_MODEL)
        time_stride = batch
    else:
        x2 = x.reshape(n, D_MODEL)
        time_stride = 1

    ua, ub, gate = _norm_proj(x2, wts["norm_mix_g"], wts["w_in_a"], wts["w_in_b"], wts["w_in_g"])
    p0 = _prev_proj(shift0, wts["w_in_b"])
    r, w, k2, v, kk, bco, gb, bonus = _rwkv_prep(
        ub, p0, wts["mu_shift"], wts["w_lora"], wts["w_decay0"], wts["a0"], wts["k_k"],
        wts["k_a"], wts["r_k"], wts["ones_b"], seq_len=seq, time_stride=time_stride)

    lay_b = _ChainLayout(batch, seq, B_HEADS, B_HEAD, B_HEAD, time_major)
    kk_c = lay_b.coef(kk)
    kk_next = jnp.concatenate([kk_c[1:], jnp.zeros_like(kk_c[:1])], axis=0)
    y_c, sb_c = _recurrence(
        (kk_next, lay_b.coef(w), lay_b.coef(bco), lay_b.coef(k2), lay_b.coef(r)),
        lay_b.rows_in(v), lay_b.state_in(state_b, rows_last=False), kk0=kk_c[0],
        groups_per_coef=lay_b.groups_per_coef)
    y_b = lay_b.rows_out(y_c)
    new_wkv = lay_b.state_out(sb_c, rows_last=False)

    if time_major:
        qs, fg, kc = _hgrn_prep(ua, wts["lb_logits"])
        lay_a = _ChainLayout(batch, seq, A_HEADS, A_HEAD, A_HEAD, time_major)
        o_c, sa_c = _recurrence(
            (lay_a.coef(fg), lay_a.coef(kc), lay_a.coef(qs)),
            lay_a.rows_in(ua[:, 2 * A_WIDTH:3 * A_WIDTH]),
            lay_a.state_in(state_a, rows_last=True), groups_per_coef=lay_a.groups_per_coef)
        o_a = lay_a.rows_out(o_c)
        new_hgrn = lay_a.state_out(sa_c, rows_last=True)
    else:
        o_a, new_hgrn = _hgrn_chunked(ua, wts["lb_logits"], state_a, batch=batch, seq=seq)

    x1 = _mix(x2, o_a, ua, y_b, bonus, gb, gate, wts["hgrn_norm_w"], wts["ln_x_w"],
              wts["ln_x_b"], wts["ones_a"], wts["ones_b"], wts["w_a_out"], wts["w_b_out"],
              wts["w_out"])
    y = _mlp(x1, wts["norm_mlp_g"], wts["w_up"], wts["w_down"], wts["norm_final_g"])
    if time_major:
        y = y.reshape(seq, batch, D_MODEL).transpose(1, 0, 2)
    else:
        y = y.reshape(batch, seq, D_MODEL)
    new_shift = _norm_rows(x[:, -1, :], wts["norm_mix_g"])
    return y, new_hgrn[None], new_wkv[None], new_shift[None]


def kernel(x_prompt, x_sample, state_hgrn, state_wkv, state_shift, norm_mix_g, w_in, mu_shift,
           w_decay0, w_decay_up, a0, w_aaa_up, w_gate_up, k_k, k_a, r_k, ln_x_w, ln_x_b,
           lb_logits, hgrn_norm_w, w_a_out, w_b_out, w_out, norm_mlp_g, w_up, w_down,
           norm_final_g):
    assert w_in.shape[0] == 1, "single-layer stack"
    w_in0 = w_in[0].astype(BF16)
    w_lora = jnp.zeros((LORA_COLS, 3 * B_WIDTH), F32)
    w_lora = w_lora.at[:DECAY_LORA, :B_WIDTH].set(w_decay_up[0])
    w_lora = w_lora.at[DECAY_LORA:DECAY_LORA + AAA_LORA, B_WIDTH:2 * B_WIDTH].set(w_aaa_up[0])
    w_lora = w_lora.at[DECAY_LORA + AAA_LORA:, 2 * B_WIDTH:].set(w_gate_up[0])
    row = lambda a: a.reshape(1, -1).astype(F32)
    wts = {
        "norm_mix_g": row(norm_mix_g[0]),
        "w_in_a": w_in0[:, :A_COLS],
        "w_in_b": w_in0[:, A_COLS:A_COLS + B_COLS],
        "w_in_g": w_in0[:, A_COLS + B_COLS:],
        "mu_shift": row(mu_shift[0]),
        "w_lora": w_lora.astype(BF16),
        "w_decay0": row(w_decay0[0]),
        "a0": row(a0[0]),
        "k_k": row(k_k[0]),
        "k_a": row(k_a[0]),
        "r_k": row(r_k[0]),
        "ln_x_w": row(ln_x_w[0]),
        "ln_x_b": row(ln_x_b[0]),
        "lb_logits": lb_logits.astype(F32),
        "hgrn_norm_w": row(hgrn_norm_w[0]),
        "w_a_out": w_a_out[0].astype(BF16),
        "w_b_out": w_b_out[0].astype(BF16),
        "w_out": w_out[0].astype(BF16),
        "norm_mlp_g": row(norm_mlp_g[0]),
        "w_up": w_up[0].astype(BF16),
        "w_down": w_down[0].astype(BF16),
        "norm_final_g": row(norm_final_g),
        "ones_a": _block_diag_ones(A_WIDTH, A_HEAD),
        "ones_b": _block_diag_ones(B_WIDTH, B_HEAD),
    }
    bp = x_prompt.shape[0]
    y_p, hgrn_p, wkv_p, shift_p = _trunk(
        x_prompt, jnp.zeros((bp, D_MODEL), F32),
        jnp.zeros((bp, A_HEADS, A_HEAD, A_HEAD), F32),
        jnp.zeros((bp, B_HEADS, B_HEAD, B_HEAD), F32), wts, time_major=False)
    y_s, hgrn_s, wkv_s, shift_s = _trunk(
        x_sample, state_shift[0], state_hgrn[0], state_wkv[0], wts, time_major=True)
    return (y_p, y_s, hgrn_p, wkv_p, shift_p, hgrn_s, wkv_s, shift_s)
```

```python
import functools

import jax
import jax.numpy as jnp
import numpy as np
from jax import lax
from jax.experimental import pallas as pl
from jax.experimental.pallas import tpu as pltpu

F32 = jnp.float32
BF16 = jnp.bfloat16

D_MODEL = 1024
A_WIDTH = 512
A_HEADS = 4
A_HEAD = 128
B_WIDTH = 512
B_HEADS = 8
B_HEAD = 64
DECAY_LORA = 64
AAA_LORA = 64
GATE_LORA = 128
LORA_COLS = DECAY_LORA + AAA_LORA + GATE_LORA
D_FF = 4 * D_MODEL
A_COLS = 4 * A_WIDTH
B_COLS = 3 * B_WIDTH + LORA_COLS
GATE_COLS = 2 * D_MODEL
NORM_EPS = 1e-6
HGRN_NORM_EPS = 1e-5
GN_EPS = 64e-5
DECAY_SCALE = 0.6065306597126334

V7X_LANES = 128
V7X_SUBLANES = 8
V7X_VMEM_LIMIT_BYTES = 56 * 1024 * 1024

TOKEN_TILE = 256
TIME_BLOCK = 64
ROWS_PER_GROUP = 64


def _params(*semantics):
    return pltpu.CompilerParams(dimension_semantics=semantics,
                                vmem_limit_bytes=V7X_VMEM_LIMIT_BYTES)


def _full(shape):
    return pl.BlockSpec(shape, lambda *_: (0,) * len(shape))


def _rmsnorm(x, g):
    return x * lax.rsqrt(jnp.mean(x * x, axis=-1, keepdims=True) + NORM_EPS) * g


def _bdot(a, w):
    return jnp.dot(a.astype(BF16), w, preferred_element_type=F32)


def _segment_sum(a, ones_bd):
    hi = a.astype(BF16)
    lo = (a - hi.astype(F32)).astype(BF16)
    return (jnp.dot(hi, ones_bd, preferred_element_type=F32)
            + jnp.dot(lo, ones_bd, preferred_element_type=F32))


def _norm_proj_kernel(x_ref, g_ref, wa_ref, wb_ref, wg_ref, ua_ref, ub_ref, gate_ref):
    hb = _rmsnorm(x_ref[...], g_ref[...]).astype(BF16)
    ua_ref[...] = jnp.dot(hb, wa_ref[...], preferred_element_type=F32)
    ub_ref[...] = jnp.dot(hb, wb_ref[...], preferred_element_type=F32)
    gate_ref[...] = jax.nn.sigmoid(jnp.dot(hb, wg_ref[...], preferred_element_type=F32))


def _norm_proj(x, g, wa, wb, wg):
    n = x.shape[0]
    tm = TOKEN_TILE
    row = lambda c: pl.BlockSpec((tm, c), lambda i: (i, 0))
    return pl.pallas_call(
        _norm_proj_kernel,
        grid=(n // tm,),
        in_specs=[row(D_MODEL), _full((1, D_MODEL)), _full((D_MODEL, A_COLS)),
                  _full((D_MODEL, B_COLS)), _full((D_MODEL, GATE_COLS))],
        out_specs=[row(A_COLS), row(B_COLS), row(GATE_COLS)],
        out_shape=[jax.ShapeDtypeStruct((n, A_COLS), F32),
                   jax.ShapeDtypeStruct((n, B_COLS), F32),
                   jax.ShapeDtypeStruct((n, GATE_COLS), F32)],
        compiler_params=_params("arbitrary"),
        name="norm_proj",
    )(x, g, wa, wb, wg)


def _norm_rows_kernel(x_ref, g_ref, o_ref):
    o_ref[...] = _rmsnorm(x_ref[...], g_ref[...])


def _norm_rows(x, g):
    return pl.pallas_call(
        _norm_rows_kernel,
        out_shape=jax.ShapeDtypeStruct(x.shape, F32),
        name="norm_rows",
    )(x, g)


def _prev_proj_kernel(h_ref, w_ref, o_ref):
    o_ref[...] = _bdot(h_ref[...], w_ref[...])


def _prev_proj(h_prev, wb):
    return pl.pallas_call(
        _prev_proj_kernel,
        out_shape=jax.ShapeDtypeStruct((h_prev.shape[0], B_COLS), F32),
        compiler_params=_params(),
        name="prev_proj",
    )(h_prev, wb)


def _rwkv_coefficients(ub, up, mu_ref, wlora_ref, wd0_ref, a0_ref, kk_w_ref, ka_ref, rk_ref,
                       ones_ref):
    xm = ub + (up - ub) * mu_ref[...]
    r = xm[:, 0:B_WIDTH]
    k = xm[:, B_WIDTH:2 * B_WIDTH]
    v = xm[:, 2 * B_WIDTH:3 * B_WIDTH]
    lo = xm[:, 3 * B_WIDTH:]
    col = lax.broadcasted_iota(jnp.int32, lo.shape, 1)
    act = jnp.where(col < DECAY_LORA, jnp.tanh(lo),
                    jnp.where(col < DECAY_LORA + AAA_LORA, lo, jax.nn.sigmoid(lo)))
    up_proj = _bdot(act, wlora_ref[...])
    logw = -DECAY_SCALE * jax.nn.sigmoid(wd0_ref[...] + up_proj[:, 0:B_WIDTH])
    a = jax.nn.sigmoid(a0_ref[...] + up_proj[:, B_WIDTH:2 * B_WIDTH])
    ones_bd = ones_ref[...]
    kk = k * kk_w_ref[...]
    kk = kk / jnp.maximum(jnp.sqrt(_segment_sum(kk * kk, ones_bd)), 1e-12)
    k2 = k * (1.0 + (a - 1.0) * ka_ref[...])
    g = up_proj[:, 2 * B_WIDTH:]
    bonus = _segment_sum(r * k2 * rk_ref[...], ones_bd) * v
    return (kk, jnp.exp(logw), kk * a, k2, r, v), g, bonus


def _rwkv_prep_kernel(ub_ref, p0_ref, mu_ref, wlora_ref, wd0_ref, a0_ref, kk_w_ref, ka_ref,
                      rk_ref, ones_ref, kk_ref, w_ref, b_ref, k_ref, r_ref, v_ref, g_ref,
                      bonus_ref, *, time_stride):
    ub = ub_ref[...]
    up = jnp.concatenate([p0_ref[...], ub[:ub.shape[0] - time_stride, :]], axis=0)
    coefs, g, bonus = _rwkv_coefficients(ub, up, mu_ref, wlora_ref, wd0_ref, a0_ref, kk_w_ref,
                                         ka_ref, rk_ref, ones_ref)
    for ref, val in zip((kk_ref, w_ref, b_ref, k_ref, r_ref, v_ref), coefs):
        ref[...] = val
    g_ref[...] = g
    bonus_ref[...] = bonus


def _rwkv_prep(ub, p0, prm, *, time_stride):
    n = ub.shape[0]
    return pl.pallas_call(
        functools.partial(_rwkv_prep_kernel, time_stride=time_stride),
        out_shape=[jax.ShapeDtypeStruct((n, B_WIDTH), F32)] * 8,
        compiler_params=_params(),
        name="rwkv_prep",
    )(ub, p0, *prm)


CHAIN_TILE = 128
N_COEF = 5


def _rwkv_prep_chain_kernel(ub_ref, p0_ref, mu_ref, wlora_ref, wd0_ref, a0_ref, kk_w_ref, ka_ref,
                            rk_ref, ones_ref, coef_ref, v_ref, g_ref, bonus_ref, carry_ref,
                            xt_ref, *, batch, parts):
    tile = pl.program_id(0)
    which = pl.program_id(1)
    half_rows = B_HEAD // parts

    @pl.when(which == 0)
    def _():
        def per_seq(b, carry):
            ub = ub_ref[b]
            first = jnp.where(tile == 0, p0_ref[pl.ds(b, 1), :], carry_ref[pl.ds(b, 1), :])
            row_id = lax.broadcasted_iota(jnp.int32, ub.shape, 0)
            up = jnp.where(row_id == 0, first, pltpu.roll(ub, 1, 0))
            carry_ref[pl.ds(b, 1), :] = ub[CHAIN_TILE - 1:CHAIN_TILE, :]
            coefs, g, bonus = _rwkv_coefficients(ub, up, mu_ref, wlora_ref, wd0_ref, a0_ref,
                                                 kk_w_ref, ka_ref, rk_ref, ones_ref)
            g_ref[b] = g
            bonus_ref[b] = bonus
            for idx, val in enumerate(coefs):
                xt_ref[idx, b] = val.T
            return carry

        lax.fori_loop(0, batch, per_seq, 0)
        for i in range(half_rows):
            m = jnp.concatenate(
                [xt_ref[N_COEF, :, pl.ds((p * half_rows + i) * B_HEADS, B_HEADS), :]
                 .reshape(batch * B_HEADS, CHAIN_TILE) for p in range(parts)], axis=0)
            v_ref[:, i, :] = m.T

    for j in range(B_HEAD):
        m = xt_ref[which, :, pl.ds(j * B_HEADS, B_HEADS), :].reshape(batch * B_HEADS, CHAIN_TILE)
        coef_ref[j] = jnp.concatenate([m] * parts, axis=0).T


def _rwkv_prep_chain(ub, p0, prm, *, batch, seq):
    parts = V7X_LANES // (batch * B_HEADS)
    n_tiles = seq // CHAIN_TILE
    tok = pl.BlockSpec((batch, CHAIN_TILE, B_WIDTH), lambda t, a: (0, t, 0))
    return pl.pallas_call(
        functools.partial(_rwkv_prep_chain_kernel, batch=batch, parts=parts),
        grid=(n_tiles, N_COEF),
        in_specs=[pl.BlockSpec((batch, CHAIN_TILE, B_COLS), lambda t, a: (0, t, 0)),
                  _full(p0.shape)] + [_full(p.shape) for p in prm],
        out_specs=[pl.BlockSpec((None, B_HEAD, CHAIN_TILE, V7X_LANES), lambda t, a: (a, 0, t, 0)),
                   pl.BlockSpec((CHAIN_TILE, B_HEAD // parts, V7X_LANES), lambda t, a: (t, 0, 0)),
                   tok, tok],
        out_shape=[jax.ShapeDtypeStruct((N_COEF, B_HEAD, seq, V7X_LANES), F32),
                   jax.ShapeDtypeStruct((seq, B_HEAD // parts, V7X_LANES), F32),
                   jax.ShapeDtypeStruct((batch, seq, B_WIDTH), F32),
                   jax.ShapeDtypeStruct((batch, seq, B_WIDTH), F32)],
        scratch_shapes=[pltpu.VMEM((batch, B_COLS), F32),
                        pltpu.VMEM((N_COEF + 1, batch, B_WIDTH, CHAIN_TILE), F32)],
        compiler_params=_params("arbitrary", "arbitrary"),
        name="rwkv_prep_chain",
    )(ub.reshape(batch, seq, B_COLS), p0, *prm)


def _rwkv_unpack_kernel(y_ref, o_ref, yt_ref, *, batch, parts):
    half_rows = B_HEAD // parts
    for i in range(half_rows):
        nt = y_ref[:, i, :].T
        for p in range(parts):
            rows = slice(p * batch * B_HEADS, (p + 1) * batch * B_HEADS)
            yt_ref[:, pl.ds((p * half_rows + i) * B_HEADS, B_HEADS), :] = (
                nt[rows].reshape(batch, B_HEADS, CHAIN_TILE))
    for b in range(batch):
        o_ref[b] = yt_ref[b].T


def _rwkv_unpack(y, *, batch, seq):
    parts = V7X_LANES // (batch * B_HEADS)
    out = pl.pallas_call(
        functools.partial(_rwkv_unpack_kernel, batch=batch, parts=parts),
        grid=(seq // CHAIN_TILE,),
        in_specs=[pl.BlockSpec((CHAIN_TILE, B_HEAD // parts, V7X_LANES), lambda t: (t, 0, 0))],
        out_specs=pl.BlockSpec((batch, CHAIN_TILE, B_WIDTH), lambda t: (0, t, 0)),
        out_shape=jax.ShapeDtypeStruct((batch, seq, B_WIDTH), F32),
        scratch_shapes=[pltpu.VMEM((batch, B_WIDTH, CHAIN_TILE), F32)],
        compiler_params=_params("arbitrary"),
        name="rwkv_unpack",
    )(y)
    return out.reshape(batch * seq, B_WIDTH)


def _hgrn_prep_kernel(q_ref, f_ref, lbl_ref, qs_ref, fg_ref, kc_ref):
    logits = lbl_ref[...]
    e = jnp.exp(logits - jnp.max(logits, axis=0, keepdims=True))
    lb = e[0:1, :] / jnp.sum(e, axis=0, keepdims=True)
    fz = f_ref[...]
    fg_ref[...] = lb + (1.0 - lb) * jax.nn.sigmoid(fz)
    kc_ref[...] = (1.0 - lb) * jax.nn.sigmoid(-fz)
    qs_ref[...] = jax.nn.silu(q_ref[...])


def _hgrn_prep(ua, lb_logits):
    n = ua.shape[0]
    tm = TOKEN_TILE
    col = lambda j: pl.BlockSpec((tm, A_WIDTH), lambda i: (i, j))
    return pl.pallas_call(
        _hgrn_prep_kernel,
        grid=(n // tm,),
        in_specs=[col(0), col(1), _full(lb_logits.shape)],
        out_specs=[col(0)] * 3,
        out_shape=[jax.ShapeDtypeStruct((n, A_WIDTH), F32)] * 3,
        compiler_params=_params("arbitrary"),
        name="hgrn_prep",
    )(ua, ua, lb_logits)


HGRN_CHUNK = 64


def _hgrn_tables(chunk):
    levels = chunk.bit_length() - 1
    t = np.arange(chunk)
    u, tt = t[None, :], t[:, None]
    rows = [u <= tt, u > tt]
    masks = []
    for level in range(levels):
        m = 1 << level
        anchor = (t // (2 * m)) * (2 * m) + m - 1
        right = (t % (2 * m)) >= m
        rows.append(((u > anchor[:, None]) & (u <= tt) & right[:, None])
                    | ((u > tt) & (u <= anchor[:, None]) & ~right[:, None]))
        masks.append((tt // (2 * m) == u // (2 * m)) & right[:, None] & ~right[None, :])
    return (jnp.asarray(np.concatenate(rows, 0), BF16),
            jnp.asarray(np.stack(masks), F32), levels)


def _split3(x):
    hi = x.astype(BF16)
    r1 = x - hi.astype(F32)
    mid = r1.astype(BF16)
    lo = (r1 - mid.astype(F32)).astype(BF16)
    return hi, mid, lo


def _dot_nt(a, b):
    return lax.dot_general(a.astype(BF16), b.astype(BF16), (((1,), (1,)), ((), ())),
                           preferred_element_type=F32)


def _dot_tn(a, b):
    return lax.dot_general(a.astype(BF16), b.astype(BF16), (((0,), (0,)), ((), ())),
                           preferred_element_type=F32)


def _hgrn_chunk_kernel(q_ref, f_ref, i_ref, lbl_ref, sums_ref, mask_ref, s0_ref, o_ref,
                       s_out_ref, st_ref, *, chunk, n_chunks, levels):
    @pl.when(pl.program_id(1) == 0)
    def _():
        for h in range(A_HEADS):
            st_ref[h] = s0_ref[h].T

    logits = lbl_ref[...]
    e = jnp.exp(logits - jnp.max(logits, axis=0, keepdims=True))
    lb = e[0:1, :] / jnp.sum(e, axis=0, keepdims=True)
    sums = sums_ref[...]
    for c in range(n_chunks):
        rows = pl.ds(c * chunk, chunk)
        fz = f_ref[rows, :]
        logf = jnp.log(lb + (1.0 - lb) * jax.nn.sigmoid(fz))
        kc = (1.0 - lb) * jax.nn.sigmoid(-fz)
        qs = jax.nn.silu(q_ref[rows, :])
        vi = i_ref[rows, :]
        decay = jnp.exp(sum(jnp.dot(sums, part, preferred_element_type=F32)
                            for part in _split3(logf)))
        seg = lambda r: slice(r * chunk, (r + 1) * chunk)
        for h in range(A_HEADS):
            hs = slice(h * A_HEAD, (h + 1) * A_HEAD)
            qh, kh, vh = qs[:, hs], kc[:, hs], vi[:, hs]
            att = None
            for level in range(levels):
                split = decay[seg(2 + level), hs]
                term = mask_ref[level] * _dot_nt(qh * split, kh * split)
                att = term if att is None else att + term
            st = st_ref[h]
            o = (_bdot(att, vh.astype(BF16))
                 + jnp.sum(qh * kh, axis=-1, keepdims=True) * vh
                 + _dot_nt(qh * decay[seg(0), hs], st))
            o_ref[rows, hs] = o
            total = decay[chunk - 1:chunk, hs]
            st_ref[h] = st * total + _dot_tn(vh, kh * decay[seg(1), hs])

    @pl.when(pl.program_id(1) == pl.num_programs(1) - 1)
    def _():
        for h in range(A_HEADS):
            s_out_ref[h] = st_ref[h].T


def _hgrn_chunked(ua, lb_logits, s0, *, batch, seq):
    chunk = HGRN_CHUNK
    tile = TOKEN_TILE
    sums, masks, levels = _hgrn_tables(chunk)
    tiles_per_seq = seq // tile
    col = lambda j: pl.BlockSpec((tile, A_WIDTH), lambda b, t: (b * tiles_per_seq + t, j))
    state_spec = pl.BlockSpec((None, A_HEADS, A_HEAD, A_HEAD), lambda b, t: (b, 0, 0, 0))
    return pl.pallas_call(
        functools.partial(_hgrn_chunk_kernel, chunk=chunk, n_chunks=tile // chunk, levels=levels),
        grid=(batch, tiles_per_seq),
        in_specs=[col(0), col(1), col(2), _full(lb_logits.shape), _full(sums.shape),
                  _full(masks.shape), state_spec],
        out_specs=[col(0), state_spec],
        out_shape=[jax.ShapeDtypeStruct((batch * seq, A_WIDTH), F32),
                   jax.ShapeDtypeStruct(s0.shape, F32)],
        scratch_shapes=[pltpu.VMEM((A_HEADS, A_HEAD, A_HEAD), F32)],
        compiler_params=_params("arbitrary", "arbitrary"),
        name="hgrn_chunked",
    )(ua, ua, ua, lb_logits, sums, masks, s0)


def _bcast_row(ref, t, j):
    return jnp.broadcast_to(ref[t, pl.ds(j, 1), :], (V7X_SUBLANES, V7X_LANES))


def _delta_rule_kernel(kk0_ref, kkn_ref, w_ref, b_ref, k_ref, r_ref, v_ref, s0_ref,
                       y_ref, s_ref, sk_ref, *, nib, nj, tb):
    @pl.when(pl.program_id(1) == 0)
    def _():
        s_ref[...] = s0_ref[...]
        for ib in range(nib):
            acc = None
            for j in range(nj):
                kk0 = jnp.broadcast_to(kk0_ref[pl.ds(j, 1), :], (V7X_SUBLANES, V7X_LANES))
                term = s0_ref[ib, j] * kk0
                acc = term if acc is None else acc + term
            sk_ref[ib] = acc

    def step(t, carry):
        sk = [sk_ref[ib] for ib in range(nib)]
        vv = [v_ref[t, ib] for ib in range(nib)]
        yacc = [None] * nib
        skn = [None] * nib
        for j in range(nj):
            wj = _bcast_row(w_ref, t, j)
            bj = _bcast_row(b_ref, t, j)
            kj = _bcast_row(k_ref, t, j)
            rj = _bcast_row(r_ref, t, j)
            nj_kk = _bcast_row(kkn_ref, t, j)
            for ib in range(nib):
                s = s_ref[ib, j] * wj - sk[ib] * bj + vv[ib] * kj
                s_ref[ib, j] = s
                yt = s * rj
                st = s * nj_kk
                yacc[ib] = yt if yacc[ib] is None else yacc[ib] + yt
                skn[ib] = st if skn[ib] is None else skn[ib] + st
        for ib in range(nib):
            y_ref[t, ib] = yacc[ib]
            sk_ref[ib] = skn[ib]
        return carry

    lax.fori_loop(0, tb, step, 0)


def _delta_rule_chain_kernel(kk_ref, w_ref, b_ref, k_ref, r_ref, kk_head_ref, v_ref, s0_ref,
                             y_ref, s_ref, sk_ref, *, nib, nj, tb):
    full = (V7X_SUBLANES, V7X_LANES)
    row = lambda ref, j, t: jnp.broadcast_to(ref[j, pl.ds(t, 1), :], full)

    @pl.when(pl.program_id(0) == 0)
    def _():
        s_ref[...] = s0_ref[...]
        for ib in range(nib):
            acc = None
            for j in range(nj):
                term = s0_ref[ib, j] * row(kk_ref, j, 0)
                acc = term if acc is None else acc + term
            sk_ref[ib] = acc

    def step(t, next_kk):
        sk = [sk_ref[ib] for ib in range(nib)]
        vv = [v_ref[t, ib] for ib in range(nib)]
        yacc = [None] * nib
        skn = [None] * nib
        for j in range(nj):
            wj, bj, kj, rj = (row(ref, j, t) for ref in (w_ref, b_ref, k_ref, r_ref))
            kkn = next_kk(j)
            for ib in range(nib):
                s = s_ref[ib, j] * wj - sk[ib] * bj + vv[ib] * kj
                s_ref[ib, j] = s
                yt = s * rj
                st = s * kkn
                yacc[ib] = yt if yacc[ib] is None else yacc[ib] + yt
                skn[ib] = st if skn[ib] is None else skn[ib] + st
        for ib in range(nib):
            y_ref[t, ib] = yacc[ib]
            sk_ref[ib] = skn[ib]

    def body(t, carry):
        step(t, lambda j: row(kk_ref, j, t + 1))
        return carry

    lax.fori_loop(0, tb - 1, body, 0)
    step(tb - 1, lambda j: row(kk_head_ref, j, 0))


def _delta_rule_chain(coef, v, s0):
    _, nj, t_len, _ = coef.shape
    nib = v.shape[1]
    tb = TIME_BLOCK
    n_blocks = t_len // tb
    head_rows = V7X_SUBLANES
    coef_spec = lambda a: pl.BlockSpec((None, nj, tb, V7X_LANES), lambda t: (a, 0, t, 0))
    head_spec = pl.BlockSpec(
        (None, nj, head_rows, V7X_LANES),
        lambda t: (0, 0, jnp.minimum(t + 1, n_blocks - 1) * (tb // head_rows), 0))
    row_spec = pl.BlockSpec((tb, nib, V7X_SUBLANES, V7X_LANES), lambda t: (t, 0, 0, 0))
    state_spec = _full(s0.shape)
    return pl.pallas_call(
        functools.partial(_delta_rule_chain_kernel, nib=nib, nj=nj, tb=tb),
        grid=(n_blocks,),
        in_specs=[coef_spec(a) for a in range(N_COEF)] + [head_spec, row_spec, state_spec],
        out_specs=[row_spec, state_spec],
        out_shape=[jax.ShapeDtypeStruct(v.shape, F32), jax.ShapeDtypeStruct(s0.shape, F32)],
        scratch_shapes=[pltpu.VMEM((nib, V7X_SUBLANES, V7X_LANES), F32)],
        compiler_params=_params("arbitrary"),
        name="delta_rule_chain",
    )(coef, coef, coef, coef, coef, coef, v, s0)


def _decay_rule_kernel(w_ref, k_ref, r_ref, v_ref, s0_ref, y_ref, s_ref, *, nib, nj, tb):
    @pl.when(pl.program_id(1) == 0)
    def _():
        s_ref[...] = s0_ref[...]

    def step(t, carry):
        vv = [v_ref[t, ib] for ib in range(nib)]
        yacc = [None] * nib
        for j in range(nj):
            wj = _bcast_row(w_ref, t, j)
            kj = _bcast_row(k_ref, t, j)
            rj = _bcast_row(r_ref, t, j)
            for ib in range(nib):
                s = s_ref[ib, j] * wj + vv[ib] * kj
                s_ref[ib, j] = s
                yt = s * rj
                yacc[ib] = yt if yacc[ib] is None else yacc[ib] + yt
        for ib in range(nib):
            y_ref[t, ib] = yacc[ib]
        return carry

    lax.fori_loop(0, tb, step, 0)


def _recurrence(coefs, v, s0, *, kk0=None, groups_per_coef):
    t_len, q, nib = v.shape[0], v.shape[1], v.shape[2]
    nj = coefs[0].shape[2]
    tb = min(TIME_BLOCK, t_len)
    coef_spec = pl.BlockSpec((tb, None, nj, V7X_LANES),
                             lambda g, t: (t, g // groups_per_coef, 0, 0))
    row_spec = pl.BlockSpec((tb, None, nib, V7X_SUBLANES, V7X_LANES),
                            lambda g, t: (t, g, 0, 0, 0))
    state_spec = pl.BlockSpec((None, nib, nj, V7X_SUBLANES, V7X_LANES),
                              lambda g, t: (g, 0, 0, 0, 0))
    delta = kk0 is not None
    if delta:
        body = functools.partial(_delta_rule_kernel, nib=nib, nj=nj, tb=tb)
        kk0_spec = pl.BlockSpec((None, nj, V7X_LANES), lambda g, t: (g // groups_per_coef, 0, 0))
        in_specs = [kk0_spec] + [coef_spec] * 5 + [row_spec, state_spec]
        args = (kk0,) + tuple(coefs) + (v, s0)
        scratch = [pltpu.VMEM((nib, V7X_SUBLANES, V7X_LANES), F32)]
    else:
        body = functools.partial(_decay_rule_kernel, nib=nib, nj=nj, tb=tb)
        in_specs = [coef_spec] * 3 + [row_spec, state_spec]
        args = tuple(coefs) + (v, s0)
        scratch = []
    return pl.pallas_call(
        body,
        grid=(q, t_len // tb),
        in_specs=in_specs,
        out_specs=[row_spec, state_spec],
        out_shape=[jax.ShapeDtypeStruct(v.shape, F32), jax.ShapeDtypeStruct(s0.shape, F32)],
        scratch_shapes=scratch,
        compiler_params=_params("arbitrary", "arbitrary"),
        name="delta_rule" if delta else "decay_rule",
    )(*args)


def _mix_kernel(x_ref, oa_ref, ga_ref, y_ref, bonus_ref, gb_ref, gate_ref, hw_ref, lnw_ref,
                lnb_ref, onesa_ref, onesb_ref, wa_ref, wb_ref, wo_ref, x1_ref):
    oa = oa_ref[...]
    ms = _segment_sum(oa * oa, onesa_ref[...]) * (1.0 / A_HEAD)
    oa = oa * lax.rsqrt(ms + HGRN_NORM_EPS) * hw_ref[...] * jax.nn.silu(ga_ref[...])
    y = y_ref[...]
    ones_b = onesb_ref[...]
    d = y - _segment_sum(y, ones_b) * (1.0 / B_HEAD)
    var = _segment_sum(d * d, ones_b) * (1.0 / B_HEAD)
    yn = d * lax.rsqrt(var + GN_EPS) * lnw_ref[...] + lnb_ref[...]
    ob = (yn + bonus_ref[...]) * gb_ref[...]
    ya = _bdot(oa, wa_ref[...])
    yb = _bdot(ob, wb_ref[...])
    gate = gate_ref[...]
    merged = gate[:, :D_MODEL] * ya + gate[:, D_MODEL:] * yb
    x1_ref[...] = x_ref[...] + _bdot(merged, wo_ref[...])


def _mix(x, oa, ua, y, bonus, gb, gate, hw, lnw, lnb, ones_a, ones_b, wa, wb, wo):
    n = x.shape[0]
    tm = TOKEN_TILE
    row = lambda c: pl.BlockSpec((tm, c), lambda i: (i, 0))
    vec = _full((1, A_WIDTH))
    return pl.pallas_call(
        _mix_kernel,
        grid=(n // tm,),
        in_specs=[row(D_MODEL), row(A_WIDTH), pl.BlockSpec((tm, A_WIDTH), lambda i: (i, 3)),
                  row(B_WIDTH), row(B_WIDTH), row(B_WIDTH), row(GATE_COLS), vec, vec, vec,
                  _full((A_WIDTH, A_WIDTH)), _full((B_WIDTH, B_WIDTH)),
                  _full((A_WIDTH, D_MODEL)), _full((B_WIDTH, D_MODEL)),
                  _full((D_MODEL, D_MODEL))],
        out_specs=row(D_MODEL),
        out_shape=jax.ShapeDtypeStruct((n, D_MODEL), F32),
        compiler_params=_params("arbitrary"),
        name="mix",
    )(x, oa, ua, y, bonus, gb, gate, hw, lnw, lnb, ones_a, ones_b, wa, wb, wo)


def _mlp_kernel(x_ref, g_ref, wu_ref, wd_ref, gf_ref, o_ref):
    x1 = x_ref[...]
    up = _bdot(_rmsnorm(x1, g_ref[...]), wu_ref[...])
    x2 = x1 + _bdot(jnp.square(jnp.maximum(up, 0.0)), wd_ref[...])
    o_ref[...] = _rmsnorm(x2, gf_ref[...])


def _mlp(x1, g, wu, wd, gf):
    n = x1.shape[0]
    tm = TOKEN_TILE
    row = pl.BlockSpec((tm, D_MODEL), lambda i: (i, 0))
    return pl.pallas_call(
        _mlp_kernel,
        grid=(n // tm,),
        in_specs=[row, _full((1, D_MODEL)), _full((D_MODEL, D_FF)), _full((D_FF, D_MODEL)),
                  _full((1, D_MODEL))],
        out_specs=row,
        out_shape=jax.ShapeDtypeStruct((n, D_MODEL), F32),
        compiler_params=_params("arbitrary"),
        name="mlp",
    )(x1, g, wu, wd, gf)


class _SampleLayout:
    def __init__(self, batch, seq, heads, nj, ni, head_minor):
        assert batch == V7X_LANES
        self.b, self.t, self.h, self.nj, self.ni = batch, seq, heads, nj, ni
        self.head_minor = head_minor
        self.groups_per_coef = ni // ROWS_PER_GROUP
        self.nib = ROWS_PER_GROUP // V7X_SUBLANES

    def _to_chain(self, x, width):
        b, t, h = self.b, self.t, self.h
        if self.head_minor:
            return x.reshape(t, b, width, h).transpose(0, 3, 2, 1)
        return x.reshape(t, b, h, width).transpose(0, 2, 3, 1)

    def coef(self, x):
        return self._to_chain(x, self.nj)

    def rows_in(self, v):
        v = self._to_chain(v, self.ni)
        return v.reshape(self.t, self.h * self.groups_per_coef, self.nib, V7X_SUBLANES, V7X_LANES)

    def rows_out(self, y):
        b, t, h, ni = self.b, self.t, self.h, self.ni
        y = y.reshape(t, h, ni, b)
        y = y.transpose(0, 3, 2, 1) if self.head_minor else y.transpose(0, 3, 1, 2)
        return y.reshape(t * b, h * ni)

    def state_in(self, s, rows_last):
        b, h, nj = self.b, self.h, self.nj
        if not rows_last:
            s = s.transpose(0, 1, 3, 2)
        s = s.reshape(b, h, nj, self.groups_per_coef, self.nib, V7X_SUBLANES)
        s = s.transpose(1, 3, 4, 2, 5, 0)
        return s.reshape(h * self.groups_per_coef, self.nib, nj, V7X_SUBLANES, V7X_LANES)

    def state_out(self, s, rows_last):
        b, h, ni, nj = self.b, self.h, self.ni, self.nj
        s = s.reshape(h, self.groups_per_coef, self.nib, nj, V7X_SUBLANES, b)
        s = s.transpose(5, 0, 3, 1, 2, 4).reshape(b, h, nj, ni)
        return s if rows_last else s.transpose(0, 1, 3, 2)


def _prompt_state_to_chain(s, parts):
    b, h, ni, nj = s.shape
    nib = ni // parts // V7X_SUBLANES
    s = s.reshape(b, h, parts, nib, V7X_SUBLANES, nj).transpose(3, 5, 4, 2, 0, 1)
    return s.reshape(nib, nj, V7X_SUBLANES, V7X_LANES)


def _prompt_state_from_chain(s, batch, heads, parts):
    nib, nj = s.shape[0], s.shape[1]
    s = s.reshape(nib, nj, V7X_SUBLANES, parts, batch, heads).transpose(4, 5, 3, 0, 2, 1)
    return s.reshape(batch, heads, parts * nib * V7X_SUBLANES, nj)


def _same_head_ones(width, head, head_minor):
    c = jnp.arange(width)
    idx = c % (width // head) if head_minor else c // head
    return (idx[:, None] == idx[None, :]).astype(BF16)


def _trunk(x, shift0, state_a, state_b, wts, *, time_major):
    batch, seq, _ = x.shape
    n = batch * seq
    if time_major:
        x2 = x.transpose(1, 0, 2).reshape(n, D_MODEL)
        time_stride = batch
    else:
        x2 = x.reshape(n, D_MODEL)
        time_stride = 1

    ua, ub, gate = _norm_proj(x2, wts["norm_mix_g"], wts["w_in_a"], wts["w_in_b"], wts["w_in_g"])
    p0 = _prev_proj(shift0, wts["w_in_b"])
    prep_params = (wts["mu_shift"], wts["w_lora"], wts["w_decay0"], wts["a0"], wts["k_k"],
                   wts["k_a"], wts["r_k"], wts["ones_b"])

    if time_major:
        kk, w, bco, k2, r, v, gb, bonus = _rwkv_prep(ub, p0, prep_params,
                                                     time_stride=time_stride)
        lay_b = _SampleLayout(batch, seq, B_HEADS, B_HEAD, B_HEAD, head_minor=True)
        kk_c = lay_b.coef(kk)
        kk_next = jnp.concatenate([kk_c[1:], jnp.zeros_like(kk_c[:1])], axis=0)
        y_c, sb_c = _recurrence(
            (kk_next, lay_b.coef(w), lay_b.coef(bco), lay_b.coef(k2), lay_b.coef(r)),
            lay_b.rows_in(v), lay_b.state_in(state_b, rows_last=False), kk0=kk_c[0],
            groups_per_coef=lay_b.groups_per_coef)
        y_b = lay_b.rows_out(y_c)
        new_wkv = lay_b.state_out(sb_c, rows_last=False)
    else:
        parts = V7X_LANES // (batch * B_HEADS)
        coef, v_c, gb, bonus = _rwkv_prep_chain(ub, p0, prep_params, batch=batch, seq=seq)
        gb = gb.reshape(n, B_WIDTH)
        bonus = bonus.reshape(n, B_WIDTH)
        nib = B_HEAD // parts // V7X_SUBLANES
        y_c, sb_c = _delta_rule_chain(
            coef, v_c.reshape(seq, nib, V7X_SUBLANES, V7X_LANES),
            _prompt_state_to_chain(state_b, parts))
        y_b = _rwkv_unpack(y_c.reshape(seq, nib * V7X_SUBLANES, V7X_LANES), batch=batch, seq=seq)
        new_wkv = _prompt_state_from_chain(sb_c, batch, B_HEADS, parts)

    if time_major:
        qs, fg, kc = _hgrn_prep(ua, wts["lb_logits"])
        lay_a = _SampleLayout(batch, seq, A_HEADS, A_HEAD, A_HEAD, head_minor=False)
        o_c, sa_c = _recurrence(
            (lay_a.coef(fg), lay_a.coef(kc), lay_a.coef(qs)),
            lay_a.rows_in(ua[:, 2 * A_WIDTH:3 * A_WIDTH]),
            lay_a.state_in(state_a, rows_last=True), groups_per_coef=lay_a.groups_per_coef)
        o_a = lay_a.rows_out(o_c)
        new_hgrn = lay_a.state_out(sa_c, rows_last=True)
    else:
        o_a, new_hgrn = _hgrn_chunked(ua, wts["lb_logits"], state_a, batch=batch, seq=seq)

    x1 = _mix(x2, o_a, ua, y_b, bonus, gb, gate, wts["hgrn_norm_w"], wts["ln_x_w"],
              wts["ln_x_b"], wts["ones_a"], wts["ones_b"], wts["w_a_out"], wts["w_b_out"],
              wts["w_out"])
    y = _mlp(x1, wts["norm_mlp_g"], wts["w_up"], wts["w_down"], wts["norm_final_g"])
    if time_major:
        y = y.reshape(seq, batch, D_MODEL).transpose(1, 0, 2)
    else:
        y = y.reshape(batch, seq, D_MODEL)
    new_shift = _norm_rows(x[:, -1, :], wts["norm_mix_g"])
    return y, new_hgrn[None], new_wkv[None], new_shift[None]


def kernel(x_prompt, x_sample, state_hgrn, state_wkv, state_shift, norm_mix_g, w_in, mu_shift,
           w_decay0, w_decay_up, a0, w_aaa_up, w_gate_up, k_k, k_a, r_k, ln_x_w, ln_x_b,
           lb_logits, hgrn_norm_w, w_a_out, w_b_out, w_out, norm_mlp_g, w_up, w_down,
           norm_final_g):
    assert w_in.shape[0] == 1, "single-layer stack"
    perm = np.arange(B_WIDTH).reshape(B_HEADS, B_HEAD).T.reshape(-1)
    pcols = lambda a: a[..., perm]
    w_in0 = w_in[0]
    w_in_b = w_in0[:, A_COLS:A_COLS + B_COLS]
    w_in_b = jnp.concatenate(
        [pcols(w_in_b[:, s * B_WIDTH:(s + 1) * B_WIDTH]) for s in range(3)]
        + [w_in_b[:, 3 * B_WIDTH:]], axis=1)
    mu = mu_shift[0]
    mu = jnp.concatenate([pcols(mu[s * B_WIDTH:(s + 1) * B_WIDTH]) for s in range(3)]
                         + [mu[3 * B_WIDTH:]])
    w_lora = jnp.zeros((LORA_COLS, 3 * B_WIDTH), F32)
    w_lora = w_lora.at[:DECAY_LORA, :B_WIDTH].set(pcols(w_decay_up[0]))
    w_lora = w_lora.at[DECAY_LORA:DECAY_LORA + AAA_LORA, B_WIDTH:2 * B_WIDTH].set(
        pcols(w_aaa_up[0]))
    w_lora = w_lora.at[DECAY_LORA + AAA_LORA:, 2 * B_WIDTH:].set(pcols(w_gate_up[0]))
    row = lambda a: a.reshape(1, -1).astype(F32)
    prow = lambda a: row(pcols(a.reshape(-1)))
    wts = {
        "norm_mix_g": row(norm_mix_g[0]),
        "w_in_a": w_in0[:, :A_COLS].astype(BF16),
        "w_in_b": w_in_b.astype(BF16),
        "w_in_g": w_in0[:, A_COLS + B_COLS:].astype(BF16),
        "mu_shift": row(mu),
        "w_lora": w_lora.astype(BF16),
        "w_decay0": prow(w_decay0[0]),
        "a0": prow(a0[0]),
        "k_k": prow(k_k[0]),
        "k_a": prow(k_a[0]),
        "r_k": prow(r_k[0]),
        "ln_x_w": prow(ln_x_w[0]),
        "ln_x_b": prow(ln_x_b[0]),
        "lb_logits": lb_logits.astype(F32),
        "hgrn_norm_w": row(hgrn_norm_w[0]),
        "w_a_out": w_a_out[0].astype(BF16),
        "w_b_out": w_b_out[0][perm, :].astype(BF16),
        "w_out": w_out[0].astype(BF16),
        "norm_mlp_g": row(norm_mlp_g[0]),
        "w_up": w_up[0].astype(BF16),
        "w_down": w_down[0].astype(BF16),
        "norm_final_g": row(norm_final_g),
        "ones_a": _same_head_ones(A_WIDTH, A_HEAD, head_minor=False),
        "ones_b": _same_head_ones(B_WIDTH, B_HEAD, head_minor=True),
    }
    bp = x_prompt.shape[0]
    y_p, hgrn_p, wkv_p, shift_p = _trunk(
        x_prompt, jnp.zeros((bp, D_MODEL), F32),
        jnp.zeros((bp, A_HEADS, A_HEAD, A_HEAD), F32),
        jnp.zeros((bp, B_HEADS, B_HEAD, B_HEAD), F32), wts, time_major=False)
    y_s, hgrn_s, wkv_s, shift_s = _trunk(
        x_sample, state_shift[0], state_hgrn[0], state_wkv[0], wts, time_major=True)
    return (y_p, y_s, hgrn_p, wkv_p, shift_p, hgrn_s, wkv_s, shift_s)
```

```python
import functools

import jax
import jax.numpy as jnp
import numpy as np
from jax import lax
from jax.experimental import pallas as pl
from jax.experimental.pallas import tpu as pltpu

F32 = jnp.float32
BF16 = jnp.bfloat16

D_MODEL = 1024
A_WIDTH = 512
A_HEADS = 4
A_HEAD = 128
B_WIDTH = 512
B_HEADS = 8
B_HEAD = 64
DECAY_LORA = 64
AAA_LORA = 64
GATE_LORA = 128
LORA_COLS = DECAY_LORA + AAA_LORA + GATE_LORA
D_FF = 4 * D_MODEL
A_COLS = 4 * A_WIDTH
B_COLS = 3 * B_WIDTH + LORA_COLS
GATE_COLS = 2 * D_MODEL
NORM_EPS = 1e-6
HGRN_NORM_EPS = 1e-5
GN_EPS = 64e-5
DECAY_SCALE = 0.6065306597126334

V7X_LANES = 128
V7X_SUBLANES = 8
V7X_VMEM_LIMIT_BYTES = 56 * 1024 * 1024

TOKEN_TILE = 256
TIME_BLOCK = 64
ROWS_PER_GROUP = 64


def _params(*semantics):
    return pltpu.CompilerParams(dimension_semantics=semantics,
                                vmem_limit_bytes=V7X_VMEM_LIMIT_BYTES)


def _full(shape):
    return pl.BlockSpec(shape, lambda *_: (0,) * len(shape))


def _rmsnorm(x, g):
    return x * lax.rsqrt(jnp.mean(x * x, axis=-1, keepdims=True) + NORM_EPS) * g


def _bdot(a, w):
    return jnp.dot(a.astype(BF16), w, preferred_element_type=F32)


def _head_sum_minor(x, heads):
    tiles = x.shape[1] // V7X_LANES
    t = x[:, 0:V7X_LANES]
    for c in range(1, tiles):
        t = t + x[:, c * V7X_LANES:(c + 1) * V7X_LANES]
    shift = heads
    while shift < V7X_LANES:
        t = t + pltpu.roll(t, shift, 1)
        shift *= 2
    return jnp.concatenate([t] * tiles, axis=1)


def _head_sum_major(x, head):
    out = []
    for h in range(x.shape[1] // head):
        seg = x[:, h * head:(h + 1) * head]
        out.append(jnp.broadcast_to(jnp.sum(seg, axis=-1, keepdims=True), seg.shape))
    return jnp.concatenate(out, axis=1)


def _norm_proj_kernel(x_ref, g_ref, wa_ref, wb_ref, wg_ref, ua_ref, ub_ref, gate_ref):
    hb = _rmsnorm(x_ref[...], g_ref[...]).astype(BF16)
    ua_ref[...] = jnp.dot(hb, wa_ref[...], preferred_element_type=F32)
    ub_ref[...] = jnp.dot(hb, wb_ref[...], preferred_element_type=F32)
    gate_ref[...] = jax.nn.sigmoid(jnp.dot(hb, wg_ref[...], preferred_element_type=F32))


def _norm_proj(x, g, wa, wb, wg):
    n = x.shape[0]
    tm = TOKEN_TILE
    row = lambda c: pl.BlockSpec((tm, c), lambda i: (i, 0))
    return pl.pallas_call(
        _norm_proj_kernel,
        grid=(n // tm,),
        in_specs=[row(D_MODEL), _full((1, D_MODEL)), _full((D_MODEL, A_COLS)),
                  _full((D_MODEL, B_COLS)), _full((D_MODEL, GATE_COLS))],
        out_specs=[row(A_COLS), row(B_COLS), row(GATE_COLS)],
        out_shape=[jax.ShapeDtypeStruct((n, A_COLS), F32),
                   jax.ShapeDtypeStruct((n, B_COLS), F32),
                   jax.ShapeDtypeStruct((n, GATE_COLS), F32)],
        compiler_params=_params("arbitrary"),
        name="norm_proj",
    )(x, g, wa, wb, wg)


def _norm_rows_kernel(x_ref, g_ref, o_ref):
    o_ref[...] = _rmsnorm(x_ref[...], g_ref[...])


def _norm_rows(x, g):
    return pl.pallas_call(
        _norm_rows_kernel,
        out_shape=jax.ShapeDtypeStruct(x.shape, F32),
        name="norm_rows",
    )(x, g)


def _prev_proj_kernel(h_ref, w_ref, o_ref):
    o_ref[...] = _bdot(h_ref[...], w_ref[...])


def _prev_proj(h_prev, wb):
    return pl.pallas_call(
        _prev_proj_kernel,
        out_shape=jax.ShapeDtypeStruct((h_prev.shape[0], B_COLS), F32),
        compiler_params=_params(),
        name="prev_proj",
    )(h_prev, wb)


def _rwkv_coefficients(ub, up, mu_ref, wlora_ref, wd0_ref, a0_ref, kk_w_ref, ka_ref, rk_ref):
    xm = ub + (up - ub) * mu_ref[...]
    r = xm[:, 0:B_WIDTH]
    k = xm[:, B_WIDTH:2 * B_WIDTH]
    v = xm[:, 2 * B_WIDTH:3 * B_WIDTH]
    lo = xm[:, 3 * B_WIDTH:]
    col = lax.broadcasted_iota(jnp.int32, lo.shape, 1)
    act = jnp.where(col < DECAY_LORA, jnp.tanh(lo),
                    jnp.where(col < DECAY_LORA + AAA_LORA, lo, jax.nn.sigmoid(lo)))
    up_proj = _bdot(act, wlora_ref[...])
    logw = -DECAY_SCALE * jax.nn.sigmoid(wd0_ref[...] + up_proj[:, 0:B_WIDTH])
    a = jax.nn.sigmoid(a0_ref[...] + up_proj[:, B_WIDTH:2 * B_WIDTH])
    kk = k * kk_w_ref[...]
    kk = kk / jnp.maximum(jnp.sqrt(_head_sum_minor(kk * kk, B_HEADS)), 1e-12)
    k2 = k * (1.0 + (a - 1.0) * ka_ref[...])
    g = up_proj[:, 2 * B_WIDTH:]
    bonus = _head_sum_minor(r * k2 * rk_ref[...], B_HEADS) * v
    return (kk, jnp.exp(logw), kk * a, k2, r, v), g, bonus


def _rwkv_prep_kernel(ub_ref, p0_ref, mu_ref, wlora_ref, wd0_ref, a0_ref, kk_w_ref, ka_ref,
                      rk_ref, kk_ref, w_ref, b_ref, k_ref, r_ref, v_ref, g_ref,
                      bonus_ref, *, time_stride):
    ub = ub_ref[...]
    up = jnp.concatenate([p0_ref[...], ub[:ub.shape[0] - time_stride, :]], axis=0)
    coefs, g, bonus = _rwkv_coefficients(ub, up, mu_ref, wlora_ref, wd0_ref, a0_ref, kk_w_ref,
                                         ka_ref, rk_ref)
    for ref, val in zip((kk_ref, w_ref, b_ref, k_ref, r_ref, v_ref), coefs):
        ref[...] = val
    g_ref[...] = g
    bonus_ref[...] = bonus


def _rwkv_prep(ub, p0, prm, *, time_stride):
    n = ub.shape[0]
    return pl.pallas_call(
        functools.partial(_rwkv_prep_kernel, time_stride=time_stride),
        out_shape=[jax.ShapeDtypeStruct((n, B_WIDTH), F32)] * 8,
        compiler_params=_params(),
        name="rwkv_prep",
    )(ub, p0, *prm)


CHAIN_TILE = 128
N_COEF = 5


def _rwkv_prep_chain_kernel(ub_ref, p0_ref, mu_ref, wlora_ref, wd0_ref, a0_ref, kk_w_ref, ka_ref,
                            rk_ref, coef_ref, v_ref, g_ref, bonus_ref, carry_ref,
                            xt_ref, *, batch, parts):
    tile = pl.program_id(0)
    which = pl.program_id(1)
    half_rows = B_HEAD // parts

    @pl.when(which == 0)
    def _():
        def per_seq(b, carry):
            ub = ub_ref[b]
            first = jnp.where(tile == 0, p0_ref[pl.ds(b, 1), :], carry_ref[pl.ds(b, 1), :])
            row_id = lax.broadcasted_iota(jnp.int32, ub.shape, 0)
            up = jnp.where(row_id == 0, first, pltpu.roll(ub, 1, 0))
            carry_ref[pl.ds(b, 1), :] = ub[CHAIN_TILE - 1:CHAIN_TILE, :]
            coefs, g, bonus = _rwkv_coefficients(ub, up, mu_ref, wlora_ref, wd0_ref, a0_ref,
                                                 kk_w_ref, ka_ref, rk_ref)
            g_ref[b] = g
            bonus_ref[b] = bonus
            for idx, val in enumerate(coefs):
                xt_ref[idx, b] = val.T
            return carry

        lax.fori_loop(0, batch, per_seq, 0)
        for i in range(half_rows):
            m = jnp.concatenate(
                [xt_ref[N_COEF, :, pl.ds((p * half_rows + i) * B_HEADS, B_HEADS), :]
                 .reshape(batch * B_HEADS, CHAIN_TILE) for p in range(parts)], axis=0)
            v_ref[:, i, :] = m.T

    for j in range(B_HEAD):
        m = xt_ref[which, :, pl.ds(j * B_HEADS, B_HEADS), :].reshape(batch * B_HEADS, CHAIN_TILE)
        coef_ref[j] = jnp.concatenate([m] * parts, axis=0).T


def _rwkv_prep_chain(ub, p0, prm, *, batch, seq):
    parts = V7X_LANES // (batch * B_HEADS)
    n_tiles = seq // CHAIN_TILE
    tok = pl.BlockSpec((batch, CHAIN_TILE, B_WIDTH), lambda t, a: (0, t, 0))
    return pl.pallas_call(
        functools.partial(_rwkv_prep_chain_kernel, batch=batch, parts=parts),
        grid=(n_tiles, N_COEF),
        in_specs=[pl.BlockSpec((batch, CHAIN_TILE, B_COLS), lambda t, a: (0, t, 0)),
                  _full(p0.shape)] + [_full(p.shape) for p in prm],
        out_specs=[pl.BlockSpec((None, B_HEAD, CHAIN_TILE, V7X_LANES), lambda t, a: (a, 0, t, 0)),
                   pl.BlockSpec((CHAIN_TILE, B_HEAD // parts, V7X_LANES), lambda t, a: (t, 0, 0)),
                   tok, tok],
        out_shape=[jax.ShapeDtypeStruct((N_COEF, B_HEAD, seq, V7X_LANES), F32),
                   jax.ShapeDtypeStruct((seq, B_HEAD // parts, V7X_LANES), F32),
                   jax.ShapeDtypeStruct((batch, seq, B_WIDTH), F32),
                   jax.ShapeDtypeStruct((batch, seq, B_WIDTH), F32)],
        scratch_shapes=[pltpu.VMEM((batch, B_COLS), F32),
                        pltpu.VMEM((N_COEF + 1, batch, B_WIDTH, CHAIN_TILE), F32)],
        compiler_params=_params("arbitrary", "arbitrary"),
        name="rwkv_prep_chain",
    )(ub.reshape(batch, seq, B_COLS), p0, *prm)


def _rwkv_unpack_kernel(y_ref, o_ref, yt_ref, *, batch, parts):
    half_rows = B_HEAD // parts
    for i in range(half_rows):
        nt = y_ref[:, i, :].T
        for p in range(parts):
            rows = slice(p * batch * B_HEADS, (p + 1) * batch * B_HEADS)
            yt_ref[:, pl.ds((p * half_rows + i) * B_HEADS, B_HEADS), :] = (
                nt[rows].reshape(batch, B_HEADS, CHAIN_TILE))
    for b in range(batch):
        o_ref[b] = yt_ref[b].T


def _rwkv_unpack(y, *, batch, seq):
    parts = V7X_LANES // (batch * B_HEADS)
    out = pl.pallas_call(
        functools.partial(_rwkv_unpack_kernel, batch=batch, parts=parts),
        grid=(seq // CHAIN_TILE,),
        in_specs=[pl.BlockSpec((CHAIN_TILE, B_HEAD // parts, V7X_LANES), lambda t: (t, 0, 0))],
        out_specs=pl.BlockSpec((batch, CHAIN_TILE, B_WIDTH), lambda t: (0, t, 0)),
        out_shape=jax.ShapeDtypeStruct((batch, seq, B_WIDTH), F32),
        scratch_shapes=[pltpu.VMEM((batch, B_WIDTH, CHAIN_TILE), F32)],
        compiler_params=_params("arbitrary"),
        name="rwkv_unpack",
    )(y)
    return out.reshape(batch * seq, B_WIDTH)


def _hgrn_prep_kernel(q_ref, f_ref, lbl_ref, qs_ref, fg_ref, kc_ref):
    logits = lbl_ref[...]
    e = jnp.exp(logits - jnp.max(logits, axis=0, keepdims=True))
    lb = e[0:1, :] / jnp.sum(e, axis=0, keepdims=True)
    fz = f_ref[...]
    fg_ref[...] = lb + (1.0 - lb) * jax.nn.sigmoid(fz)
    kc_ref[...] = (1.0 - lb) * jax.nn.sigmoid(-fz)
    qs_ref[...] = jax.nn.silu(q_ref[...])


def _hgrn_prep(ua, lb_logits):
    n = ua.shape[0]
    tm = TOKEN_TILE
    col = lambda j: pl.BlockSpec((tm, A_WIDTH), lambda i: (i, j))
    return pl.pallas_call(
        _hgrn_prep_kernel,
        grid=(n // tm,),
        in_specs=[col(0), col(1), _full(lb_logits.shape)],
        out_specs=[col(0)] * 3,
        out_shape=[jax.ShapeDtypeStruct((n, A_WIDTH), F32)] * 3,
        compiler_params=_params("arbitrary"),
        name="hgrn_prep",
    )(ua, ua, lb_logits)


HGRN_CHUNK = 64


def _hgrn_tables(chunk):
    levels = chunk.bit_length() - 1
    t = np.arange(chunk)
    u, tt = t[None, :], t[:, None]
    rows = [u <= tt, u > tt]
    masks = []
    for level in range(levels):
        m = 1 << level
        anchor = (t // (2 * m)) * (2 * m) + m - 1
        right = (t % (2 * m)) >= m
        rows.append(((u > anchor[:, None]) & (u <= tt) & right[:, None])
                    | ((u > tt) & (u <= anchor[:, None]) & ~right[:, None]))
        masks.append((tt // (2 * m) == u // (2 * m)) & right[:, None] & ~right[None, :])
    return (jnp.asarray(np.concatenate(rows, 0), BF16),
            jnp.asarray(np.stack(masks), F32), levels)


def _split3(x):
    hi = x.astype(BF16)
    r1 = x - hi.astype(F32)
    mid = r1.astype(BF16)
    lo = (r1 - mid.astype(F32)).astype(BF16)
    return hi, mid, lo


def _dot_nt(a, b):
    return lax.dot_general(a.astype(BF16), b.astype(BF16), (((1,), (1,)), ((), ())),
                           preferred_element_type=F32)


def _dot_tn(a, b):
    return lax.dot_general(a.astype(BF16), b.astype(BF16), (((0,), (0,)), ((), ())),
                           preferred_element_type=F32)


def _hgrn_lower_bound(lbl_ref):
    logits = lbl_ref[...]
    e = jnp.exp(logits - jnp.max(logits, axis=0, keepdims=True))
    return e[0:1, :] / jnp.sum(e, axis=0, keepdims=True)


def _hgrn_chunk_kernel(q_ref, f_ref, i_ref, lbl_ref, sums_ref, mask_ref, s0_ref, o_ref,
                       s_out_ref, st_ref, *, chunk, n_chunks, levels):
    @pl.when(pl.program_id(1) == 0)
    def _():
        for h in range(A_HEADS):
            st_ref[h] = s0_ref[h].T

    lb = _hgrn_lower_bound(lbl_ref)
    sums = sums_ref[...]
    seg = lambda r: slice(r * chunk, (r + 1) * chunk)
    for c in range(n_chunks):
        rows = pl.ds(c * chunk, chunk)
        fz = f_ref[rows, :]
        logf = jnp.log(lb + (1.0 - lb) * jax.nn.sigmoid(fz))
        kc = (1.0 - lb) * jax.nn.sigmoid(-fz)
        qs = jax.nn.silu(q_ref[rows, :])
        vi = i_ref[rows, :]
        decay = jnp.exp(sum(jnp.dot(sums, part, preferred_element_type=F32)
                            for part in _split3(logf)))
        for h in range(A_HEADS):
            hs = slice(h * A_HEAD, (h + 1) * A_HEAD)
            qh, kh, vh = qs[:, hs], kc[:, hs], vi[:, hs]
            att = None
            for level in range(levels):
                split = decay[seg(2 + level), hs]
                term = mask_ref[level] * _dot_nt(qh * split, kh * split)
                att = term if att is None else att + term
            st = st_ref[h]
            o_ref[rows, hs] = (_bdot(att, vh.astype(BF16))
                               + jnp.sum(qh * kh, axis=-1, keepdims=True) * vh
                               + _dot_nt(qh * decay[seg(0), hs], st))
            st_ref[h] = (st * decay[chunk - 1:chunk, hs]
                         + _dot_tn(vh, kh * decay[seg(1), hs]))

    @pl.when(pl.program_id(1) == pl.num_programs(1) - 1)
    def _():
        for h in range(A_HEADS):
            s_out_ref[h] = st_ref[h].T


def _hgrn_chunked(ua, lb_logits, s0, *, batch, seq):
    chunk = HGRN_CHUNK
    tile = TOKEN_TILE
    sums, masks, levels = _hgrn_tables(chunk)
    tiles_per_seq = seq // tile
    col = lambda j: pl.BlockSpec((tile, A_WIDTH), lambda b, t: (b * tiles_per_seq + t, j))
    state_spec = pl.BlockSpec((None, A_HEADS, A_HEAD, A_HEAD), lambda b, t: (b, 0, 0, 0))
    return pl.pallas_call(
        functools.partial(_hgrn_chunk_kernel, chunk=chunk, n_chunks=tile // chunk, levels=levels),
        grid=(batch, tiles_per_seq),
        in_specs=[col(0), col(1), col(2), _full(lb_logits.shape), _full(sums.shape),
                  _full(masks.shape), state_spec],
        out_specs=[col(0), state_spec],
        out_shape=[jax.ShapeDtypeStruct((batch * seq, A_WIDTH), F32),
                   jax.ShapeDtypeStruct(s0.shape, F32)],
        scratch_shapes=[pltpu.VMEM((A_HEADS, A_HEAD, A_HEAD), F32)],
        compiler_params=_params("arbitrary", "arbitrary"),
        name="hgrn_chunked",
    )(ua, ua, ua, lb_logits, sums, masks, s0)


def _bcast_row(ref, t, j):
    return jnp.broadcast_to(ref[t, pl.ds(j, 1), :], (V7X_SUBLANES, V7X_LANES))


def _delta_rule_kernel(kk0_ref, kkn_ref, w_ref, b_ref, k_ref, r_ref, v_ref, s0_ref,
                       y_ref, s_ref, sk_ref, *, nib, nj, tb):
    @pl.when(pl.program_id(1) == 0)
    def _():
        s_ref[...] = s0_ref[...]
        for ib in range(nib):
            acc = None
            for j in range(nj):
                kk0 = jnp.broadcast_to(kk0_ref[pl.ds(j, 1), :], (V7X_SUBLANES, V7X_LANES))
                term = s0_ref[ib, j] * kk0
                acc = term if acc is None else acc + term
            sk_ref[ib] = acc

    def step(t, carry):
        sk = [sk_ref[ib] for ib in range(nib)]
        vv = [v_ref[t, ib] for ib in range(nib)]
        yacc = [None] * nib
        skn = [None] * nib
        for j in range(nj):
            wj = _bcast_row(w_ref, t, j)
            bj = _bcast_row(b_ref, t, j)
            kj = _bcast_row(k_ref, t, j)
            rj = _bcast_row(r_ref, t, j)
            nj_kk = _bcast_row(kkn_ref, t, j)
            for ib in range(nib):
                s = s_ref[ib, j] * wj - sk[ib] * bj + vv[ib] * kj
                s_ref[ib, j] = s
                yt = s * rj
                st = s * nj_kk
                yacc[ib] = yt if yacc[ib] is None else yacc[ib] + yt
                skn[ib] = st if skn[ib] is None else skn[ib] + st
        for ib in range(nib):
            y_ref[t, ib] = yacc[ib]
            sk_ref[ib] = skn[ib]
        return carry

    lax.fori_loop(0, tb, step, 0)


def _delta_rule_chain_kernel(kk_ref, w_ref, b_ref, k_ref, r_ref, kk_head_ref, v_ref, s0_ref,
                             y_ref, s_ref, sk_ref, kkx_ref, *, nib, nj, tb):
    full = (V7X_SUBLANES, V7X_LANES)
    row = lambda ref, j, t: jnp.broadcast_to(ref[j, pl.ds(t, 1), :], full)

    @pl.when(pl.program_id(0) == 0)
    def _():
        s_ref[...] = s0_ref[...]
        for ib in range(nib):
            acc = None
            for j in range(nj):
                term = s0_ref[ib, j] * row(kk_ref, j, 0)
                acc = term if acc is None else acc + term
            sk_ref[ib] = acc

    kkx_ref[:, 0:tb, :] = kk_ref[...]
    kkx_ref[:, tb:tb + V7X_SUBLANES, :] = kk_head_ref[...]

    def step(t, carry):
        sk = [sk_ref[ib] for ib in range(nib)]
        vv = [v_ref[t, ib] for ib in range(nib)]
        yacc = [None] * nib
        skn = [None] * nib
        for j in range(nj):
            wj, bj, kj, rj = (row(ref, j, t) for ref in (w_ref, b_ref, k_ref, r_ref))
            kkn = row(kkx_ref, j, t + 1)
            for ib in range(nib):
                s = s_ref[ib, j] * wj - sk[ib] * bj + vv[ib] * kj
                s_ref[ib, j] = s
                yt = s * rj
                st = s * kkn
                yacc[ib] = yt if yacc[ib] is None else yacc[ib] + yt
                skn[ib] = st if skn[ib] is None else skn[ib] + st
        for ib in range(nib):
            y_ref[t, ib] = yacc[ib]
            sk_ref[ib] = skn[ib]
        return carry

    lax.fori_loop(0, tb, step, 0)


def _delta_rule_chain(coef, v, s0):
    _, nj, t_len, _ = coef.shape
    nib = v.shape[1]
    tb = TIME_BLOCK
    n_blocks = t_len // tb
    head_rows = V7X_SUBLANES
    coef_spec = lambda a: pl.BlockSpec((None, nj, tb, V7X_LANES), lambda t: (a, 0, t, 0))
    head_spec = pl.BlockSpec(
        (None, nj, head_rows, V7X_LANES),
        lambda t: (0, 0, jnp.minimum(t + 1, n_blocks - 1) * (tb // head_rows), 0))
    row_spec = pl.BlockSpec((tb, nib, V7X_SUBLANES, V7X_LANES), lambda t: (t, 0, 0, 0))
    return pl.pallas_call(
        functools.partial(_delta_rule_chain_kernel, nib=nib, nj=nj, tb=tb),
        grid=(n_blocks,),
        in_specs=[coef_spec(a) for a in range(N_COEF)]
        + [head_spec, row_spec, _full(s0.shape)],
        out_specs=[row_spec, _full(s0.shape)],
        out_shape=[jax.ShapeDtypeStruct(v.shape, F32), jax.ShapeDtypeStruct(s0.shape, F32)],
        scratch_shapes=[pltpu.VMEM((nib, V7X_SUBLANES, V7X_LANES), F32),
                        pltpu.VMEM((nj, tb + head_rows, V7X_LANES), F32)],
        compiler_params=_params("arbitrary"),
        name="delta_rule_chain",
    )(coef, coef, coef, coef, coef, coef, v, s0)


def _decay_rule_kernel(w_ref, k_ref, r_ref, v_ref, s0_ref, y_ref, s_ref, *, nib, nj, tb):
    @pl.when(pl.program_id(1) == 0)
    def _():
        s_ref[...] = s0_ref[...]

    def step(t, carry):
        vv = [v_ref[t, ib] for ib in range(nib)]
        yacc = [None] * nib
        for j in range(nj):
            wj = _bcast_row(w_ref, t, j)
            kj = _bcast_row(k_ref, t, j)
            rj = _bcast_row(r_ref, t, j)
            for ib in range(nib):
                s = s_ref[ib, j] * wj + vv[ib] * kj
                s_ref[ib, j] = s
                yt = s * rj
                yacc[ib] = yt if yacc[ib] is None else yacc[ib] + yt
        for ib in range(nib):
            y_ref[t, ib] = yacc[ib]
        return carry

    lax.fori_loop(0, tb, step, 0)


def _recurrence(coefs, v, s0, *, kk0=None, groups_per_coef):
    t_len, q, nib = v.shape[0], v.shape[1], v.shape[2]
    nj = coefs[0].shape[2]
    tb = min(TIME_BLOCK, t_len)
    coef_spec = pl.BlockSpec((tb, None, nj, V7X_LANES),
                             lambda g, t: (t, g // groups_per_coef, 0, 0))
    row_spec = pl.BlockSpec((tb, None, nib, V7X_SUBLANES, V7X_LANES),
                            lambda g, t: (t, g, 0, 0, 0))
    state_spec = pl.BlockSpec((None, nib, nj, V7X_SUBLANES, V7X_LANES),
                              lambda g, t: (g, 0, 0, 0, 0))
    delta = kk0 is not None
    if delta:
        body = functools.partial(_delta_rule_kernel, nib=nib, nj=nj, tb=tb)
        kk0_spec = pl.BlockSpec((None, nj, V7X_LANES), lambda g, t: (g // groups_per_coef, 0, 0))
        in_specs = [kk0_spec] + [coef_spec] * 5 + [row_spec, state_spec]
        args = (kk0,) + tuple(coefs) + (v, s0)
        scratch = [pltpu.VMEM((nib, V7X_SUBLANES, V7X_LANES), F32)]
    else:
        body = functools.partial(_decay_rule_kernel, nib=nib, nj=nj, tb=tb)
        in_specs = [coef_spec] * 3 + [row_spec, state_spec]
        args = tuple(coefs) + (v, s0)
        scratch = []
    return pl.pallas_call(
        body,
        grid=(q, t_len // tb),
        in_specs=in_specs,
        out_specs=[row_spec, state_spec],
        out_shape=[jax.ShapeDtypeStruct(v.shape, F32), jax.ShapeDtypeStruct(s0.shape, F32)],
        scratch_shapes=scratch,
        compiler_params=_params("arbitrary", "arbitrary"),
        name="delta_rule" if delta else "decay_rule",
    )(*args)


def _mix_kernel(x_ref, oa_ref, ga_ref, y_ref, bonus_ref, gb_ref, gate_ref, hw_ref, lnw_ref,
                lnb_ref, wa_ref, wb_ref, wo_ref, x1_ref):
    oa = oa_ref[...]
    ms = _head_sum_major(oa * oa, A_HEAD) * (1.0 / A_HEAD)
    oa = oa * lax.rsqrt(ms + HGRN_NORM_EPS) * hw_ref[...] * jax.nn.silu(ga_ref[...])
    y = y_ref[...]
    d = y - _head_sum_minor(y, B_HEADS) * (1.0 / B_HEAD)
    var = _head_sum_minor(d * d, B_HEADS) * (1.0 / B_HEAD)
    yn = d * lax.rsqrt(var + GN_EPS) * lnw_ref[...] + lnb_ref[...]
    ob = (yn + bonus_ref[...]) * gb_ref[...]
    ya = _bdot(oa, wa_ref[...])
    yb = _bdot(ob, wb_ref[...])
    gate = gate_ref[...]
    merged = gate[:, :D_MODEL] * ya + gate[:, D_MODEL:] * yb
    x1_ref[...] = x_ref[...] + _bdot(merged, wo_ref[...])


def _mix(x, oa, ua, y, bonus, gb, gate, hw, lnw, lnb, wa, wb, wo):
    n = x.shape[0]
    tm = TOKEN_TILE
    row = lambda c: pl.BlockSpec((tm, c), lambda i: (i, 0))
    vec = _full((1, A_WIDTH))
    return pl.pallas_call(
        _mix_kernel,
        grid=(n // tm,),
        in_specs=[row(D_MODEL), row(A_WIDTH), pl.BlockSpec((tm, A_WIDTH), lambda i: (i, 3)),
                  row(B_WIDTH), row(B_WIDTH), row(B_WIDTH), row(GATE_COLS), vec, vec, vec,
                  _full((A_WIDTH, D_MODEL)), _full((B_WIDTH, D_MODEL)),
                  _full((D_MODEL, D_MODEL))],
        out_specs=row(D_MODEL),
        out_shape=jax.ShapeDtypeStruct((n, D_MODEL), F32),
        compiler_params=_params("arbitrary"),
        name="mix",
    )(x, oa, ua, y, bonus, gb, gate, hw, lnw, lnb, wa, wb, wo)


def _mlp_kernel(x_ref, g_ref, wu_ref, wd_ref, gf_ref, o_ref):
    x1 = x_ref[...]
    up = _bdot(_rmsnorm(x1, g_ref[...]), wu_ref[...])
    x2 = x1 + _bdot(jnp.square(jnp.maximum(up, 0.0)), wd_ref[...])
    o_ref[...] = _rmsnorm(x2, gf_ref[...])


def _mlp(x1, g, wu, wd, gf):
    n = x1.shape[0]
    tm = TOKEN_TILE
    row = pl.BlockSpec((tm, D_MODEL), lambda i: (i, 0))
    return pl.pallas_call(
        _mlp_kernel,
        grid=(n // tm,),
        in_specs=[row, _full((1, D_MODEL)), _full((D_MODEL, D_FF)), _full((D_FF, D_MODEL)),
                  _full((1, D_MODEL))],
        out_specs=row,
        out_shape=jax.ShapeDtypeStruct((n, D_MODEL), F32),
        compiler_params=_params("arbitrary"),
        name="mlp",
    )(x1, g, wu, wd, gf)


class _SampleLayout:
    def __init__(self, batch, seq, heads, nj, ni, head_minor):
        assert batch == V7X_LANES
        self.b, self.t, self.h, self.nj, self.ni = batch, seq, heads, nj, ni
        self.head_minor = head_minor
        self.groups_per_coef = ni // ROWS_PER_GROUP
        self.nib = ROWS_PER_GROUP // V7X_SUBLANES

    def _to_chain(self, x, width):
        b, t, h = self.b, self.t, self.h
        if self.head_minor:
            return x.reshape(t, b, width, h).transpose(0, 3, 2, 1)
        return x.reshape(t, b, h, width).transpose(0, 2, 3, 1)

    def coef(self, x):
        return self._to_chain(x, self.nj)

    def rows_in(self, v):
        v = self._to_chain(v, self.ni)
        return v.reshape(self.t, self.h * self.groups_per_coef, self.nib, V7X_SUBLANES, V7X_LANES)

    def rows_out(self, y):
        b, t, h, ni = self.b, self.t, self.h, self.ni
        y = y.reshape(t, h, ni, b)
        y = y.transpose(0, 3, 2, 1) if self.head_minor else y.transpose(0, 3, 1, 2)
        return y.reshape(t * b, h * ni)

    def state_in(self, s, rows_last):
        b, h, nj = self.b, self.h, self.nj
        if not rows_last:
            s = s.transpose(0, 1, 3, 2)
        s = s.reshape(b, h, nj, self.groups_per_coef, self.nib, V7X_SUBLANES)
        s = s.transpose(1, 3, 4, 2, 5, 0)
        return s.reshape(h * self.groups_per_coef, self.nib, nj, V7X_SUBLANES, V7X_LANES)

    def state_out(self, s, rows_last):
        b, h, ni, nj = self.b, self.h, self.ni, self.nj
        s = s.reshape(h, self.groups_per_coef, self.nib, nj, V7X_SUBLANES, b)
        s = s.transpose(5, 0, 3, 1, 2, 4).reshape(b, h, nj, ni)
        return s if rows_last else s.transpose(0, 1, 3, 2)


def _prompt_state_to_chain(s, parts):
    b, h, ni, nj = s.shape
    nib = ni // parts // V7X_SUBLANES
    s = s.reshape(b, h, parts, nib, V7X_SUBLANES, nj).transpose(3, 5, 4, 2, 0, 1)
    return s.reshape(nib, nj, V7X_SUBLANES, V7X_LANES)


def _prompt_state_from_chain(s, batch, heads, parts):
    nib, nj = s.shape[0], s.shape[1]
    s = s.reshape(nib, nj, V7X_SUBLANES, parts, batch, heads).transpose(4, 5, 3, 0, 2, 1)
    return s.reshape(batch, heads, parts * nib * V7X_SUBLANES, nj)


def _trunk(x, shift0, state_a, state_b, wts, *, time_major):
    batch, seq, _ = x.shape
    n = batch * seq
    if time_major:
        x2 = x.transpose(1, 0, 2).reshape(n, D_MODEL)
        time_stride = batch
    else:
        x2 = x.reshape(n, D_MODEL)
        time_stride = 1

    ua, ub, gate = _norm_proj(x2, wts["norm_mix_g"], wts["w_in_a"], wts["w_in_b"], wts["w_in_g"])
    p0 = _prev_proj(shift0, wts["w_in_b"])
    prep_params = (wts["mu_shift"], wts["w_lora"], wts["w_decay0"], wts["a0"], wts["k_k"],
                   wts["k_a"], wts["r_k"])

    if time_major:
        kk, w, bco, k2, r, v, gb, bonus = _rwkv_prep(ub, p0, prep_params,
                                                     time_stride=time_stride)
        lay_b = _SampleLayout(batch, seq, B_HEADS, B_HEAD, B_HEAD, head_minor=True)
        kk_c = lay_b.coef(kk)
        kk_next = jnp.concatenate([kk_c[1:], jnp.zeros_like(kk_c[:1])], axis=0)
        y_c, sb_c = _recurrence(
            (kk_next, lay_b.coef(w), lay_b.coef(bco), lay_b.coef(k2), lay_b.coef(r)),
            lay_b.rows_in(v), lay_b.state_in(state_b, rows_last=False), kk0=kk_c[0],
            groups_per_coef=lay_b.groups_per_coef)
        y_b = lay_b.rows_out(y_c)
        new_wkv = lay_b.state_out(sb_c, rows_last=False)
    else:
        parts = V7X_LANES // (batch * B_HEADS)
        coef, v_c, gb, bonus = _rwkv_prep_chain(ub, p0, prep_params, batch=batch, seq=seq)
        gb = gb.reshape(n, B_WIDTH)
        bonus = bonus.reshape(n, B_WIDTH)
        nib = B_HEAD // parts // V7X_SUBLANES
        y_c, sb_c = _delta_rule_chain(
            coef, v_c.reshape(seq, nib, V7X_SUBLANES, V7X_LANES),
            _prompt_state_to_chain(state_b, parts))
        o_a, new_hgrn = _hgrn_chunked(ua, wts["lb_logits"], state_a, batch=batch, seq=seq)
        y_b = _rwkv_unpack(y_c.reshape(seq, nib * V7X_SUBLANES, V7X_LANES), batch=batch, seq=seq)
        new_wkv = _prompt_state_from_chain(sb_c, batch, B_HEADS, parts)

    if time_major:
        qs, fg, kc = _hgrn_prep(ua, wts["lb_logits"])
        lay_a = _SampleLayout(batch, seq, A_HEADS, A_HEAD, A_HEAD, head_minor=False)
        o_c, sa_c = _recurrence(
            (lay_a.coef(fg), lay_a.coef(kc), lay_a.coef(qs)),
            lay_a.rows_in(ua[:, 2 * A_WIDTH:3 * A_WIDTH]),
            lay_a.state_in(state_a, rows_last=True), groups_per_coef=lay_a.groups_per_coef)
        o_a = lay_a.rows_out(o_c)
        new_hgrn = lay_a.state_out(sa_c, rows_last=True)

    x1 = _mix(x2, o_a, ua, y_b, bonus, gb, gate, wts["hgrn_norm_w"], wts["ln_x_w"],
              wts["ln_x_b"], wts["w_a_out"], wts["w_b_out"], wts["w_out"])
    y = _mlp(x1, wts["norm_mlp_g"], wts["w_up"], wts["w_down"], wts["norm_final_g"])
    if time_major:
        y = y.reshape(seq, batch, D_MODEL).transpose(1, 0, 2)
    else:
        y = y.reshape(batch, seq, D_MODEL)
    new_shift = _norm_rows(x[:, -1, :], wts["norm_mix_g"])
    return y, new_hgrn[None], new_wkv[None], new_shift[None]


def kernel(x_prompt, x_sample, state_hgrn, state_wkv, state_shift, norm_mix_g, w_in, mu_shift,
           w_decay0, w_decay_up, a0, w_aaa_up, w_gate_up, k_k, k_a, r_k, ln_x_w, ln_x_b,
           lb_logits, hgrn_norm_w, w_a_out, w_b_out, w_out, norm_mlp_g, w_up, w_down,
           norm_final_g):
    assert w_in.shape[0] == 1, "single-layer stack"
    def pcols(a):
        lead = a.shape[:-1]
        return a.reshape(*lead, B_HEADS, B_HEAD).swapaxes(-1, -2).reshape(*lead, B_WIDTH)

    prows = lambda a: a.reshape(B_HEADS, B_HEAD, -1).swapaxes(0, 1).reshape(B_WIDTH, -1)
    w_in0 = w_in[0]
    w_in_b = w_in0[:, A_COLS:A_COLS + B_COLS]
    w_in_b = jnp.concatenate(
        [pcols(w_in_b[:, s * B_WIDTH:(s + 1) * B_WIDTH]) for s in range(3)]
        + [w_in_b[:, 3 * B_WIDTH:]], axis=1)
    mu = mu_shift[0]
    mu = jnp.concatenate([pcols(mu[s * B_WIDTH:(s + 1) * B_WIDTH]) for s in range(3)]
                         + [mu[3 * B_WIDTH:]])
    w_lora = jnp.zeros((LORA_COLS, 3 * B_WIDTH), F32)
    w_lora = w_lora.at[:DECAY_LORA, :B_WIDTH].set(pcols(w_decay_up[0]))
    w_lora = w_lora.at[DECAY_LORA:DECAY_LORA + AAA_LORA, B_WIDTH:2 * B_WIDTH].set(
        pcols(w_aaa_up[0]))
    w_lora = w_lora.at[DECAY_LORA + AAA_LORA:, 2 * B_WIDTH:].set(pcols(w_gate_up[0]))
    row = lambda a: a.reshape(1, -1).astype(F32)
    prow = lambda a: row(pcols(a.reshape(-1)))
    wts = {
        "norm_mix_g": row(norm_mix_g[0]),
        "w_in_a": w_in0[:, :A_COLS].astype(BF16),
        "w_in_b": w_in_b.astype(BF16),
        "w_in_g": w_in0[:, A_COLS + B_COLS:].astype(BF16),
        "mu_shift": row(mu),
        "w_lora": w_lora.astype(BF16),
        "w_decay0": prow(w_decay0[0]),
        "a0": prow(a0[0]),
        "k_k": prow(k_k[0]),
        "k_a": prow(k_a[0]),
        "r_k": prow(r_k[0]),
        "ln_x_w": prow(ln_x_w[0]),
        "ln_x_b": prow(ln_x_b[0]),
        "lb_logits": lb_logits.astype(F32),
        "hgrn_norm_w": row(hgrn_norm_w[0]),
        "w_a_out": w_a_out[0].astype(BF16),
        "w_b_out": prows(w_b_out[0]).astype(BF16),
        "w_out": w_out[0].astype(BF16),
        "norm_mlp_g": row(norm_mlp_g[0]),
        "w_up": w_up[0].astype(BF16),
        "w_down": w_down[0].astype(BF16),
        "norm_final_g": row(norm_final_g),
    }
    bp = x_prompt.shape[0]
    y_p, hgrn_p, wkv_p, shift_p = _trunk(
        x_prompt, jnp.zeros((bp, D_MODEL), F32),
        jnp.zeros((bp, A_HEADS, A_HEAD, A_HEAD), F32),
        jnp.zeros((bp, B_HEADS, B_HEAD, B_HEAD), F32), wts, time_major=False)
    y_s, hgrn_s, wkv_s, shift_s = _trunk(
        x_sample, state_shift[0], state_hgrn[0], state_wkv[0], wts, time_major=True)
    return (y_p, y_s, hgrn_p, wkv_p, shift_p, hgrn_s, wkv_s, shift_s)
```

```python
import functools

import jax
import jax.numpy as jnp
import numpy as np
from jax import lax
from jax.experimental import pallas as pl
from jax.experimental.pallas import tpu as pltpu

F32 = jnp.float32
BF16 = jnp.bfloat16

D_MODEL = 1024
A_WIDTH = 512
A_HEADS = 4
A_HEAD = 128
B_WIDTH = 512
B_HEADS = 8
B_HEAD = 64
DECAY_LORA = 64
AAA_LORA = 64
GATE_LORA = 128
LORA_COLS = DECAY_LORA + AAA_LORA + GATE_LORA
D_FF = 4 * D_MODEL
A_COLS = 4 * A_WIDTH
B_COLS = 3 * B_WIDTH + LORA_COLS
GATE_COLS = 2 * D_MODEL
NORM_EPS = 1e-6
HGRN_NORM_EPS = 1e-5
GN_EPS = 64e-5
DECAY_SCALE = 0.6065306597126334

V7X_LANES = 128
V7X_SUBLANES = 8
V7X_VMEM_LIMIT_BYTES = 56 * 1024 * 1024

TOKEN_TILE = 256
SUBTILES = 2
TIME_BLOCK = 64
ROWS_PER_GROUP = 64


def _params(*semantics):
    return pltpu.CompilerParams(dimension_semantics=semantics,
                                vmem_limit_bytes=V7X_VMEM_LIMIT_BYTES)


def _full(shape):
    return pl.BlockSpec(shape, lambda *_: (0,) * len(shape))


def _rmsnorm(x, g):
    return x * lax.rsqrt(jnp.mean(x * x, axis=-1, keepdims=True) + NORM_EPS) * g


def _bdot(a, w):
    return jnp.dot(a.astype(BF16), w, preferred_element_type=F32)


def _head_sum(a, same_head):
    hi = a.astype(BF16)
    lo = (a - hi.astype(F32)).astype(BF16)
    return (jnp.dot(hi, same_head, preferred_element_type=F32)
            + jnp.dot(lo, same_head, preferred_element_type=F32))


def _norm_proj_kernel(x_ref, g_ref, wa_ref, wb_ref, ua_ref, ub_ref):
    sub = x_ref.shape[0] // SUBTILES
    for part in range(SUBTILES):
        rows = pl.ds(part * sub, sub)
        hb = _rmsnorm(x_ref[rows, :], g_ref[...]).astype(BF16)
        ua_ref[rows, :] = jnp.dot(hb, wa_ref[...], preferred_element_type=F32)
        ub_ref[rows, :] = jnp.dot(hb, wb_ref[...], preferred_element_type=F32)


def _norm_proj(x, g, wa, wb):
    n = x.shape[0]
    tm = TOKEN_TILE * SUBTILES
    row = lambda c: pl.BlockSpec((tm, c), lambda i: (i, 0))
    return pl.pallas_call(
        _norm_proj_kernel,
        grid=(n // tm,),
        in_specs=[row(D_MODEL), _full((1, D_MODEL)), _full((D_MODEL, A_COLS)),
                  _full((D_MODEL, B_COLS))],
        out_specs=[row(A_COLS), row(B_COLS)],
        out_shape=[jax.ShapeDtypeStruct((n, A_COLS), F32),
                   jax.ShapeDtypeStruct((n, B_COLS), F32)],
        compiler_params=_params("arbitrary"),
        name="norm_proj",
    )(x, g, wa, wb)


def _norm_rows_kernel(x_ref, g_ref, o_ref):
    o_ref[...] = _rmsnorm(x_ref[...], g_ref[...])


def _norm_rows(x, g):
    return pl.pallas_call(
        _norm_rows_kernel,
        out_shape=jax.ShapeDtypeStruct(x.shape, F32),
        name="norm_rows",
    )(x, g)


def _prev_proj_kernel(h_ref, w_ref, o_ref):
    o_ref[...] = _bdot(h_ref[...], w_ref[...])


def _prev_proj(h_prev, wb):
    return pl.pallas_call(
        _prev_proj_kernel,
        out_shape=jax.ShapeDtypeStruct((h_prev.shape[0], B_COLS), F32),
        compiler_params=_params(),
        name="prev_proj",
    )(h_prev, wb)


def _rwkv_coefficients(ub, up, mu_ref, wlora_ref, wd0_ref, a0_ref, kk_w_ref, ka_ref, rk_ref,
                       same_head_ref):
    xm = ub + (up - ub) * mu_ref[...]
    r = xm[:, 0:B_WIDTH]
    k = xm[:, B_WIDTH:2 * B_WIDTH]
    v = xm[:, 2 * B_WIDTH:3 * B_WIDTH]
    lo = xm[:, 3 * B_WIDTH:]
    col = lax.broadcasted_iota(jnp.int32, lo.shape, 1)
    act = jnp.where(col < DECAY_LORA, jnp.tanh(lo),
                    jnp.where(col < DECAY_LORA + AAA_LORA, lo, jax.nn.sigmoid(lo)))
    up_proj = _bdot(act, wlora_ref[...])
    logw = -DECAY_SCALE * jax.nn.sigmoid(wd0_ref[...] + up_proj[:, 0:B_WIDTH])
    a = jax.nn.sigmoid(a0_ref[...] + up_proj[:, B_WIDTH:2 * B_WIDTH])
    same_head = same_head_ref[...]
    kk = k * kk_w_ref[...]
    kk = kk / jnp.maximum(jnp.sqrt(_head_sum(kk * kk, same_head)), 1e-12)
    k2 = k * (1.0 + (a - 1.0) * ka_ref[...])
    g = up_proj[:, 2 * B_WIDTH:]
    bonus = _head_sum(r * k2 * rk_ref[...], same_head) * v
    return (kk, jnp.exp(logw), kk * a, k2, r, v), g, bonus


def _rwkv_prep_kernel(ub_ref, p0_ref, mu_ref, wlora_ref, wd0_ref, a0_ref, kk_w_ref, ka_ref,
                      rk_ref, same_head_ref, kk_ref, w_ref, b_ref, k_ref, r_ref, v_ref, g_ref,
                      bonus_ref, *, time_stride):
    ub = ub_ref[...]
    up = jnp.concatenate([p0_ref[...], ub[:ub.shape[0] - time_stride, :]], axis=0)
    coefs, g, bonus = _rwkv_coefficients(ub, up, mu_ref, wlora_ref, wd0_ref, a0_ref, kk_w_ref,
                                         ka_ref, rk_ref, same_head_ref)
    for ref, val in zip((kk_ref, w_ref, b_ref, k_ref, r_ref, v_ref), coefs):
        ref[...] = val
    g_ref[...] = g
    bonus_ref[...] = bonus


def _rwkv_prep(ub, p0, prm, *, time_stride):
    n = ub.shape[0]
    return pl.pallas_call(
        functools.partial(_rwkv_prep_kernel, time_stride=time_stride),
        out_shape=[jax.ShapeDtypeStruct((n, B_WIDTH), F32)] * 8,
        compiler_params=_params(),
        name="rwkv_prep",
    )(ub, p0, *prm)


CHAIN_TILE = 128
N_COEF = 5


def _rwkv_prep_chain_kernel(ub_ref, p0_ref, mu_ref, wlora_ref, wd0_ref, a0_ref, kk_w_ref, ka_ref,
                            rk_ref, same_head_ref, coef_ref, v_ref, g_ref, bonus_ref, carry_ref,
                            xt_ref, *, batch, parts):
    tile = pl.program_id(0)
    which = pl.program_id(1)
    half_rows = B_HEAD // parts

    @pl.when(which == 0)
    def _():
        def per_seq(b, carry):
            ub = ub_ref[b]
            first = jnp.where(tile == 0, p0_ref[pl.ds(b, 1), :], carry_ref[pl.ds(b, 1), :])
            row_id = lax.broadcasted_iota(jnp.int32, ub.shape, 0)
            up = jnp.where(row_id == 0, first, pltpu.roll(ub, 1, 0))
            carry_ref[pl.ds(b, 1), :] = ub[CHAIN_TILE - 1:CHAIN_TILE, :]
            coefs, g, bonus = _rwkv_coefficients(ub, up, mu_ref, wlora_ref, wd0_ref, a0_ref,
                                                 kk_w_ref, ka_ref, rk_ref, same_head_ref)
            g_ref[b] = g
            bonus_ref[b] = bonus
            for idx, val in enumerate(coefs):
                xt_ref[idx, b] = val.T
            return carry

        lax.fori_loop(0, batch, per_seq, 0)
        for i in range(half_rows):
            m = jnp.concatenate(
                [xt_ref[N_COEF, :, pl.ds((p * half_rows + i) * B_HEADS, B_HEADS), :]
                 .reshape(batch * B_HEADS, CHAIN_TILE) for p in range(parts)], axis=0)
            v_ref[:, i, :] = m.T

    for j in range(B_HEAD):
        m = xt_ref[which, :, pl.ds(j * B_HEADS, B_HEADS), :].reshape(batch * B_HEADS, CHAIN_TILE)
        coef_ref[j] = jnp.concatenate([m] * parts, axis=0).T


def _rwkv_prep_chain(ub, p0, prm, *, batch, seq):
    parts = V7X_LANES // (batch * B_HEADS)
    n_tiles = seq // CHAIN_TILE
    tok = pl.BlockSpec((batch, CHAIN_TILE, B_WIDTH), lambda t, a: (0, t, 0))
    return pl.pallas_call(
        functools.partial(_rwkv_prep_chain_kernel, batch=batch, parts=parts),
        grid=(n_tiles, N_COEF),
        in_specs=[pl.BlockSpec((batch, CHAIN_TILE, B_COLS), lambda t, a: (0, t, 0)),
                  _full(p0.shape)] + [_full(p.shape) for p in prm],
        out_specs=[pl.BlockSpec((None, B_HEAD, CHAIN_TILE, V7X_LANES), lambda t, a: (a, 0, t, 0)),
                   pl.BlockSpec((CHAIN_TILE, B_HEAD // parts, V7X_LANES), lambda t, a: (t, 0, 0)),
                   tok, tok],
        out_shape=[jax.ShapeDtypeStruct((N_COEF, B_HEAD, seq, V7X_LANES), F32),
                   jax.ShapeDtypeStruct((seq, B_HEAD // parts, V7X_LANES), F32),
                   jax.ShapeDtypeStruct((batch, seq, B_WIDTH), F32),
                   jax.ShapeDtypeStruct((batch, seq, B_WIDTH), F32)],
        scratch_shapes=[pltpu.VMEM((batch, B_COLS), F32),
                        pltpu.VMEM((N_COEF + 1, batch, B_WIDTH, CHAIN_TILE), F32)],
        compiler_params=_params("arbitrary", "arbitrary"),
        name="rwkv_prep_chain",
    )(ub.reshape(batch, seq, B_COLS), p0, *prm)


def _rwkv_unpack_kernel(y_ref, o_ref, yt_ref, *, batch, parts):
    half_rows = B_HEAD // parts
    for i in range(half_rows):
        nt = y_ref[:, i, :].T
        for p in range(parts):
            rows = slice(p * batch * B_HEADS, (p + 1) * batch * B_HEADS)
            yt_ref[:, pl.ds((p * half_rows + i) * B_HEADS, B_HEADS), :] = (
                nt[rows].reshape(batch, B_HEADS, CHAIN_TILE))
    for b in range(batch):
        o_ref[b] = yt_ref[b].T


def _rwkv_unpack(y, *, batch, seq):
    parts = V7X_LANES // (batch * B_HEADS)
    out = pl.pallas_call(
        functools.partial(_rwkv_unpack_kernel, batch=batch, parts=parts),
        grid=(seq // CHAIN_TILE,),
        in_specs=[pl.BlockSpec((CHAIN_TILE, B_HEAD // parts, V7X_LANES), lambda t: (t, 0, 0))],
        out_specs=pl.BlockSpec((batch, CHAIN_TILE, B_WIDTH), lambda t: (0, t, 0)),
        out_shape=jax.ShapeDtypeStruct((batch, seq, B_WIDTH), F32),
        scratch_shapes=[pltpu.VMEM((batch, B_WIDTH, CHAIN_TILE), F32)],
        compiler_params=_params("arbitrary"),
        name="rwkv_unpack",
    )(y)
    return out.reshape(batch * seq, B_WIDTH)


def _hgrn_prep_kernel(q_ref, f_ref, lbl_ref, qs_ref, fg_ref, kc_ref):
    logits = lbl_ref[...]
    e = jnp.exp(logits - jnp.max(logits, axis=0, keepdims=True))
    lb = e[0:1, :] / jnp.sum(e, axis=0, keepdims=True)
    fz = f_ref[...]
    fg_ref[...] = lb + (1.0 - lb) * jax.nn.sigmoid(fz)
    kc_ref[...] = (1.0 - lb) * jax.nn.sigmoid(-fz)
    qs_ref[...] = jax.nn.silu(q_ref[...])


def _hgrn_prep(ua, lb_logits):
    n = ua.shape[0]
    tm = TOKEN_TILE
    col = lambda j: pl.BlockSpec((tm, A_WIDTH), lambda i: (i, j))
    return pl.pallas_call(
        _hgrn_prep_kernel,
        grid=(n // tm,),
        in_specs=[col(0), col(1), _full(lb_logits.shape)],
        out_specs=[col(0)] * 3,
        out_shape=[jax.ShapeDtypeStruct((n, A_WIDTH), F32)] * 3,
        compiler_params=_params("arbitrary"),
        name="hgrn_prep",
    )(ua, ua, lb_logits)


HGRN_CHUNK = 64


def _hgrn_tables(chunk):
    levels = chunk.bit_length() - 1
    t = np.arange(chunk)
    u, tt = t[None, :], t[:, None]
    rows = [u <= tt, u > tt]
    masks = []
    for level in range(levels):
        m = 1 << level
        anchor = (t // (2 * m)) * (2 * m) + m - 1
        right = (t % (2 * m)) >= m
        rows.append(((u > anchor[:, None]) & (u <= tt) & right[:, None])
                    | ((u > tt) & (u <= anchor[:, None]) & ~right[:, None]))
        masks.append((tt // (2 * m) == u // (2 * m)) & right[:, None] & ~right[None, :])
    return (jnp.asarray(np.concatenate(rows, 0), BF16),
            jnp.asarray(np.stack(masks), F32), levels)


def _split3(x):
    hi = x.astype(BF16)
    r1 = x - hi.astype(F32)
    mid = r1.astype(BF16)
    lo = (r1 - mid.astype(F32)).astype(BF16)
    return hi, mid, lo


def _dot_nt(a, b):
    return lax.dot_general(a.astype(BF16), b.astype(BF16), (((1,), (1,)), ((), ())),
                           preferred_element_type=F32)


def _dot_tn(a, b):
    return lax.dot_general(a.astype(BF16), b.astype(BF16), (((0,), (0,)), ((), ())),
                           preferred_element_type=F32)


def _hgrn_lower_bound(lbl_ref):
    logits = lbl_ref[...]
    e = jnp.exp(logits - jnp.max(logits, axis=0, keepdims=True))
    return e[0:1, :] / jnp.sum(e, axis=0, keepdims=True)


def _hgrn_chunk_kernel(q_ref, f_ref, i_ref, lbl_ref, sums_ref, mask_ref, s0_ref, o_ref,
                       s_out_ref, st_ref, *, chunk, n_chunks, levels):
    n_seqs = st_ref.shape[0]

    @pl.when(pl.program_id(1) == 0)
    def _():
        for s in range(n_seqs):
            for h in range(A_HEADS):
                st_ref[s, h] = s0_ref[s, h].T

    lb = _hgrn_lower_bound(lbl_ref)
    sums = sums_ref[...]
    seg = lambda r: slice(r * chunk, (r + 1) * chunk)

    def one_chunk(s, rows):
        fz = f_ref[s, rows, :]
        logf = jnp.log(lb + (1.0 - lb) * jax.nn.sigmoid(fz))
        kc = (1.0 - lb) * jax.nn.sigmoid(-fz)
        qs = jax.nn.silu(q_ref[s, rows, :])
        vi = i_ref[s, rows, :]
        parts = _split3(logf)
        yield
        for h in range(A_HEADS):
            hs = slice(h * A_HEAD, (h + 1) * A_HEAD)
            decay = jnp.exp(sum(jnp.dot(sums, part[:, hs], preferred_element_type=F32)
                                for part in parts))
            yield
            qh, kh, vh = qs[:, hs], kc[:, hs], vi[:, hs]
            att = None
            for level in range(levels):
                split = decay[seg(2 + level)]
                term = mask_ref[level] * _dot_nt(qh * split, kh * split)
                att = term if att is None else att + term
                if level % 2 == 1:
                    yield
            st = st_ref[s, h]
            o_ref[s, rows, hs] = (_bdot(att, vh.astype(BF16))
                                  + jnp.sum(qh * kh, axis=-1, keepdims=True) * vh
                                  + _dot_nt(qh * decay[seg(0)], st))
            yield
            st_ref[s, h] = st * decay[chunk - 1:chunk] + _dot_tn(vh, kh * decay[seg(1)])
            yield

    for c in range(n_chunks):
        streams = [one_chunk(s, pl.ds(c * chunk, chunk)) for s in range(n_seqs)]
        while streams:
            streams = [g for g in streams if next(g, StopIteration) is not StopIteration]

    @pl.when(pl.program_id(1) == pl.num_programs(1) - 1)
    def _():
        for s in range(n_seqs):
            for h in range(A_HEADS):
                s_out_ref[s, h] = st_ref[s, h].T


HGRN_SEQS_PER_STEP = 2


def _hgrn_chunked(ua, lb_logits, s0, *, batch, seq):
    chunk = HGRN_CHUNK
    tile = TOKEN_TILE
    n_seqs = HGRN_SEQS_PER_STEP
    sums, masks, levels = _hgrn_tables(chunk)
    col = lambda j: pl.BlockSpec((n_seqs, tile, A_WIDTH), lambda b, t: (b, t, j))
    state_spec = pl.BlockSpec((n_seqs, A_HEADS, A_HEAD, A_HEAD), lambda b, t: (b, 0, 0, 0))
    out, state = pl.pallas_call(
        functools.partial(_hgrn_chunk_kernel, chunk=chunk, n_chunks=tile // chunk, levels=levels),
        grid=(batch // n_seqs, seq // tile),
        in_specs=[col(0), col(1), col(2), _full(lb_logits.shape), _full(sums.shape),
                  _full(masks.shape), state_spec],
        out_specs=[col(0), state_spec],
        out_shape=[jax.ShapeDtypeStruct((batch, seq, A_WIDTH), F32),
                   jax.ShapeDtypeStruct(s0.shape, F32)],
        scratch_shapes=[pltpu.VMEM((n_seqs, A_HEADS, A_HEAD, A_HEAD), F32)],
        compiler_params=_params("arbitrary", "arbitrary"),
        name="hgrn_chunked",
    )(*[ua.reshape(batch, seq, A_COLS)] * 3, lb_logits, sums, masks, s0)
    return out.reshape(batch * seq, A_WIDTH), state


def _bcast_row(ref, t, j):
    return jnp.broadcast_to(ref[t, pl.ds(j, 1), :], (V7X_SUBLANES, V7X_LANES))


def _delta_rule_kernel(kk0_ref, kkn_ref, w_ref, b_ref, k_ref, r_ref, v_ref, s0_ref,
                       y_ref, s_ref, sk_ref, *, nib, nj, tb):
    @pl.when(pl.program_id(1) == 0)
    def _():
        s_ref[...] = s0_ref[...]
        for ib in range(nib):
            acc = None
            for j in range(nj):
                kk0 = jnp.broadcast_to(kk0_ref[pl.ds(j, 1), :], (V7X_SUBLANES, V7X_LANES))
                term = s0_ref[ib, j] * kk0
                acc = term if acc is None else acc + term
            sk_ref[ib] = acc

    def step(t, carry):
        sk = [sk_ref[ib] for ib in range(nib)]
        vv = [v_ref[t, ib] for ib in range(nib)]
        yacc = [None] * nib
        skn = [None] * nib
        for j in range(nj):
            wj = _bcast_row(w_ref, t, j)
            bj = _bcast_row(b_ref, t, j)
            kj = _bcast_row(k_ref, t, j)
            rj = _bcast_row(r_ref, t, j)
            nj_kk = _bcast_row(kkn_ref, t, j)
            for ib in range(nib):
                s = s_ref[ib, j] * wj - sk[ib] * bj + vv[ib] * kj
                s_ref[ib, j] = s
                yt = s * rj
                st = s * nj_kk
                yacc[ib] = yt if yacc[ib] is None else yacc[ib] + yt
                skn[ib] = st if skn[ib] is None else skn[ib] + st
        for ib in range(nib):
            y_ref[t, ib] = yacc[ib]
            sk_ref[ib] = skn[ib]
        return carry

    lax.fori_loop(0, tb, step, 0)


def _delta_rule_chain_kernel(kk_ref, w_ref, b_ref, k_ref, r_ref, kk_head_ref, v_ref, s0_ref,
                             y_ref, s_ref, sk_ref, kkx_ref, *, nib, nj, tb):
    full = (V7X_SUBLANES, V7X_LANES)
    row = lambda ref, j, t: jnp.broadcast_to(ref[j, pl.ds(t, 1), :], full)

    @pl.when(pl.program_id(0) == 0)
    def _():
        s_ref[...] = s0_ref[...]
        for ib in range(nib):
            acc = None
            for j in range(nj):
                term = s0_ref[ib, j] * row(kk_ref, j, 0)
                acc = term if acc is None else acc + term
            sk_ref[ib] = acc

    kkx_ref[:, 0:tb, :] = kk_ref[...]
    kkx_ref[:, tb:tb + V7X_SUBLANES, :] = kk_head_ref[...]

    def step(t, carry):
        sk = [sk_ref[ib] for ib in range(nib)]
        vv = [v_ref[t, ib] for ib in range(nib)]
        yacc = [None] * nib
        skn = [None] * nib
        for j in range(nj):
            wj, bj, kj, rj = (row(ref, j, t) for ref in (w_ref, b_ref, k_ref, r_ref))
            kkn = row(kkx_ref, j, t + 1)
            for ib in range(nib):
                s = s_ref[ib, j] * wj - sk[ib] * bj + vv[ib] * kj
                s_ref[ib, j] = s
                yt = s * rj
                st = s * kkn
                yacc[ib] = yt if yacc[ib] is None else yacc[ib] + yt
                skn[ib] = st if skn[ib] is None else skn[ib] + st
        for ib in range(nib):
            y_ref[t, ib] = yacc[ib]
            sk_ref[ib] = skn[ib]
        return carry

    lax.fori_loop(0, tb, step, 0)


def _delta_rule_chain(coef, v, s0):
    _, nj, t_len, _ = coef.shape
    nib = v.shape[1]
    tb = TIME_BLOCK
    n_blocks = t_len // tb
    head_rows = V7X_SUBLANES
    coef_spec = lambda a: pl.BlockSpec((None, nj, tb, V7X_LANES), lambda t: (a, 0, t, 0))
    head_spec = pl.BlockSpec(
        (None, nj, head_rows, V7X_LANES),
        lambda t: (0, 0, jnp.minimum(t + 1, n_blocks - 1) * (tb // head_rows), 0))
    row_spec = pl.BlockSpec((tb, nib, V7X_SUBLANES, V7X_LANES), lambda t: (t, 0, 0, 0))
    return pl.pallas_call(
        functools.partial(_delta_rule_chain_kernel, nib=nib, nj=nj, tb=tb),
        grid=(n_blocks,),
        in_specs=[coef_spec(a) for a in range(N_COEF)]
        + [head_spec, row_spec, _full(s0.shape)],
        out_specs=[row_spec, _full(s0.shape)],
        out_shape=[jax.ShapeDtypeStruct(v.shape, F32), jax.ShapeDtypeStruct(s0.shape, F32)],
        scratch_shapes=[pltpu.VMEM((nib, V7X_SUBLANES, V7X_LANES), F32),
                        pltpu.VMEM((nj, tb + head_rows, V7X_LANES), F32)],
        compiler_params=_params("arbitrary"),
        name="delta_rule_chain",
    )(coef, coef, coef, coef, coef, coef, v, s0)


def _decay_rule_kernel(w_ref, k_ref, r_ref, v_ref, s0_ref, y_ref, s_ref, *, nib, nj, tb):
    @pl.when(pl.program_id(1) == 0)
    def _():
        s_ref[...] = s0_ref[...]

    def step(t, carry):
        vv = [v_ref[t, ib] for ib in range(nib)]
        yacc = [None] * nib
        for j in range(nj):
            wj = _bcast_row(w_ref, t, j)
            kj = _bcast_row(k_ref, t, j)
            rj = _bcast_row(r_ref, t, j)
            for ib in range(nib):
                s = s_ref[ib, j] * wj + vv[ib] * kj
                s_ref[ib, j] = s
                yt = s * rj
                yacc[ib] = yt if yacc[ib] is None else yacc[ib] + yt
        for ib in range(nib):
            y_ref[t, ib] = yacc[ib]
        return carry

    lax.fori_loop(0, tb, step, 0)


def _recurrence(coefs, v, s0, *, kk0=None, groups_per_coef):
    t_len, q, nib = v.shape[0], v.shape[1], v.shape[2]
    nj = coefs[0].shape[2]
    tb = min(TIME_BLOCK, t_len)
    coef_spec = pl.BlockSpec((tb, None, nj, V7X_LANES),
                             lambda g, t: (t, g // groups_per_coef, 0, 0))
    row_spec = pl.BlockSpec((tb, None, nib, V7X_SUBLANES, V7X_LANES),
                            lambda g, t: (t, g, 0, 0, 0))
    state_spec = pl.BlockSpec((None, nib, nj, V7X_SUBLANES, V7X_LANES),
                              lambda g, t: (g, 0, 0, 0, 0))
    delta = kk0 is not None
    if delta:
        body = functools.partial(_delta_rule_kernel, nib=nib, nj=nj, tb=tb)
        kk0_spec = pl.BlockSpec((None, nj, V7X_LANES), lambda g, t: (g // groups_per_coef, 0, 0))
        in_specs = [kk0_spec] + [coef_spec] * 5 + [row_spec, state_spec]
        args = (kk0,) + tuple(coefs) + (v, s0)
        scratch = [pltpu.VMEM((nib, V7X_SUBLANES, V7X_LANES), F32)]
    else:
        body = functools.partial(_decay_rule_kernel, nib=nib, nj=nj, tb=tb)
        in_specs = [coef_spec] * 3 + [row_spec, state_spec]
        args = tuple(coefs) + (v, s0)
        scratch = []
    return pl.pallas_call(
        body,
        grid=(q, t_len // tb),
        in_specs=in_specs,
        out_specs=[row_spec, state_spec],
        out_shape=[jax.ShapeDtypeStruct(v.shape, F32), jax.ShapeDtypeStruct(s0.shape, F32)],
        scratch_shapes=scratch,
        compiler_params=_params("arbitrary", "arbitrary"),
        name="delta_rule" if delta else "decay_rule",
    )(*args)


def _mix_kernel(x_ref, oa_ref, ga_ref, y_ref, bonus_ref, gb_ref, gn_ref, wg_ref, hw_ref, lnw_ref,
                lnb_ref, same_a_ref, same_b_ref, wa_ref, wb_ref, wo_ref, x1_ref):
    sub = x_ref.shape[0] // SUBTILES
    for part in range(SUBTILES):
        rows = pl.ds(part * sub, sub)
        oa = oa_ref[rows, :]
        ms = _head_sum(oa * oa, same_a_ref[...]) * (1.0 / A_HEAD)
        oa = oa * lax.rsqrt(ms + HGRN_NORM_EPS) * hw_ref[...] * jax.nn.silu(ga_ref[rows, :])
        y = y_ref[rows, :]
        same_b = same_b_ref[...]
        d = y - _head_sum(y, same_b) * (1.0 / B_HEAD)
        var = _head_sum(d * d, same_b) * (1.0 / B_HEAD)
        yn = d * lax.rsqrt(var + GN_EPS) * lnw_ref[...] + lnb_ref[...]
        ob = (yn + bonus_ref[rows, :]) * gb_ref[rows, :]
        ya = _bdot(oa, wa_ref[...])
        yb = _bdot(ob, wb_ref[...])
        x = x_ref[rows, :]
        gate = jax.nn.sigmoid(_bdot(_rmsnorm(x, gn_ref[...]), wg_ref[...]))
        merged = gate[:, 0:D_MODEL] * ya + gate[:, D_MODEL:GATE_COLS] * yb
        x1_ref[rows, :] = x + _bdot(merged, wo_ref[...])


def _mix(x, oa, ua, y, bonus, gb, gn, wg, hw, lnw, lnb, same_a, same_b, wa, wb, wo):
    n = x.shape[0]
    tm = TOKEN_TILE * SUBTILES
    row = lambda c: pl.BlockSpec((tm, c), lambda i: (i, 0))
    vec = _full((1, A_WIDTH))
    return pl.pallas_call(
        _mix_kernel,
        grid=(n // tm,),
        in_specs=[row(D_MODEL), row(A_WIDTH), pl.BlockSpec((tm, A_WIDTH), lambda i: (i, 3)),
                  row(B_WIDTH), row(B_WIDTH), row(B_WIDTH), _full((1, D_MODEL)),
                  _full((D_MODEL, GATE_COLS)), vec, vec, vec,
                  _full((A_WIDTH, A_WIDTH)), _full((B_WIDTH, B_WIDTH)),
                  _full((A_WIDTH, D_MODEL)), _full((B_WIDTH, D_MODEL)),
                  _full((D_MODEL, D_MODEL))],
        out_specs=row(D_MODEL),
        out_shape=jax.ShapeDtypeStruct((n, D_MODEL), F32),
        compiler_params=_params("arbitrary"),
        name="mix",
    )(x, oa, ua, y, bonus, gb, gn, wg, hw, lnw, lnb, same_a, same_b, wa, wb, wo)


def _mlp_kernel(x_ref, g_ref, wu_ref, wd_ref, gf_ref, o_ref):
    sub = x_ref.shape[0] // SUBTILES
    for part in range(SUBTILES):
        rows = pl.ds(part * sub, sub)
        x1 = x_ref[rows, :]
        up = _bdot(_rmsnorm(x1, g_ref[...]), wu_ref[...])
        x2 = x1 + _bdot(jnp.square(jnp.maximum(up, 0.0)), wd_ref[...])
        o_ref[rows, :] = _rmsnorm(x2, gf_ref[...])


def _mlp(x1, g, wu, wd, gf):
    n = x1.shape[0]
    tm = TOKEN_TILE * SUBTILES
    row = pl.BlockSpec((tm, D_MODEL), lambda i: (i, 0))
    return pl.pallas_call(
        _mlp_kernel,
        grid=(n // tm,),
        in_specs=[row, _full((1, D_MODEL)), _full((D_MODEL, D_FF)), _full((D_FF, D_MODEL)),
                  _full((1, D_MODEL))],
        out_specs=row,
        out_shape=jax.ShapeDtypeStruct((n, D_MODEL), F32),
        compiler_params=_params("arbitrary"),
        name="mlp",
    )(x1, g, wu, wd, gf)


class _SampleLayout:
    def __init__(self, batch, seq, heads, nj, ni, head_minor):
        assert batch == V7X_LANES
        self.b, self.t, self.h, self.nj, self.ni = batch, seq, heads, nj, ni
        self.head_minor = head_minor
        self.groups_per_coef = ni // ROWS_PER_GROUP
        self.nib = ROWS_PER_GROUP // V7X_SUBLANES

    def _to_chain(self, x, width):
        b, t, h = self.b, self.t, self.h
        if self.head_minor:
            return x.reshape(t, b, width, h).transpose(0, 3, 2, 1)
        return x.reshape(t, b, h, width).transpose(0, 2, 3, 1)

    def coef(self, x):
        return self._to_chain(x, self.nj)

    def rows_in(self, v):
        v = self._to_chain(v, self.ni)
        return v.reshape(self.t, self.h * self.groups_per_coef, self.nib, V7X_SUBLANES, V7X_LANES)

    def rows_out(self, y):
        b, t, h, ni = self.b, self.t, self.h, self.ni
        y = y.reshape(t, h, ni, b)
        y = y.transpose(0, 3, 2, 1) if self.head_minor else y.transpose(0, 3, 1, 2)
        return y.reshape(t * b, h * ni)

    def state_in(self, s, rows_last):
        b, h, nj = self.b, self.h, self.nj
        if not rows_last:
            s = s.transpose(0, 1, 3, 2)
        s = s.reshape(b, h, nj, self.groups_per_coef, self.nib, V7X_SUBLANES)
        s = s.transpose(1, 3, 4, 2, 5, 0)
        return s.reshape(h * self.groups_per_coef, self.nib, nj, V7X_SUBLANES, V7X_LANES)

    def state_out(self, s, rows_last):
        b, h, ni, nj = self.b, self.h, self.ni, self.nj
        s = s.reshape(h, self.groups_per_coef, self.nib, nj, V7X_SUBLANES, b)
        s = s.transpose(5, 0, 3, 1, 2, 4).reshape(b, h, nj, ni)
        return s if rows_last else s.transpose(0, 1, 3, 2)


def _prompt_state_to_chain(s, parts):
    b, h, ni, nj = s.shape
    nib = ni // parts // V7X_SUBLANES
    s = s.reshape(b, h, parts, nib, V7X_SUBLANES, nj).transpose(3, 5, 4, 2, 0, 1)
    return s.reshape(nib, nj, V7X_SUBLANES, V7X_LANES)


def _prompt_state_from_chain(s, batch, heads, parts):
    nib, nj = s.shape[0], s.shape[1]
    s = s.reshape(nib, nj, V7X_SUBLANES, parts, batch, heads).transpose(4, 5, 3, 0, 2, 1)
    return s.reshape(batch, heads, parts * nib * V7X_SUBLANES, nj)


def _same_head_matrix(width, heads, head_minor):
    c = np.arange(width)
    idx = c % heads if head_minor else c // (width // heads)
    return jnp.asarray(idx[:, None] == idx[None, :], BF16)


def _trunk(x, shift0, state_a, state_b, wts, *, time_major):
    batch, seq, _ = x.shape
    n = batch * seq
    if time_major:
        x2 = x.transpose(1, 0, 2).reshape(n, D_MODEL)
        time_stride = batch
    else:
        x2 = x.reshape(n, D_MODEL)
        time_stride = 1

    ua, ub = _norm_proj(x2, wts["norm_mix_g"], wts["w_in_a"], wts["w_in_b"])
    p0 = _prev_proj(shift0, wts["w_in_b"])
    prep_params = (wts["mu_shift"], wts["w_lora"], wts["w_decay0"], wts["a0"], wts["k_k"],
                   wts["k_a"], wts["r_k"], wts["same_head_b"])

    if time_major:
        kk, w, bco, k2, r, v, gb, bonus = _rwkv_prep(ub, p0, prep_params,
                                                     time_stride=time_stride)
        lay_b = _SampleLayout(batch, seq, B_HEADS, B_HEAD, B_HEAD, head_minor=True)
        kk_c = lay_b.coef(kk)
        kk_next = jnp.concatenate([kk_c[1:], jnp.zeros_like(kk_c[:1])], axis=0)
        y_c, sb_c = _recurrence(
            (kk_next, lay_b.coef(w), lay_b.coef(bco), lay_b.coef(k2), lay_b.coef(r)),
            lay_b.rows_in(v), lay_b.state_in(state_b, rows_last=False), kk0=kk_c[0],
            groups_per_coef=lay_b.groups_per_coef)
        y_b = lay_b.rows_out(y_c)
        new_wkv = lay_b.state_out(sb_c, rows_last=False)
    else:
        parts = V7X_LANES // (batch * B_HEADS)
        coef, v_c, gb, bonus = _rwkv_prep_chain(ub, p0, prep_params, batch=batch, seq=seq)
        gb = gb.reshape(n, B_WIDTH)
        bonus = bonus.reshape(n, B_WIDTH)
        nib = B_HEAD // parts // V7X_SUBLANES
        y_c, sb_c = _delta_rule_chain(
            coef, v_c.reshape(seq, nib, V7X_SUBLANES, V7X_LANES),
            _prompt_state_to_chain(state_b, parts))
        o_a, new_hgrn = _hgrn_chunked(ua, wts["lb_logits"], state_a, batch=batch, seq=seq)
        y_b = _rwkv_unpack(y_c.reshape(seq, nib * V7X_SUBLANES, V7X_LANES), batch=batch, seq=seq)
        new_wkv = _prompt_state_from_chain(sb_c, batch, B_HEADS, parts)

    if time_major:
        qs, fg, kc = _hgrn_prep(ua, wts["lb_logits"])
        lay_a = _SampleLayout(batch, seq, A_HEADS, A_HEAD, A_HEAD, head_minor=False)
        o_c, sa_c = _recurrence(
            (lay_a.coef(fg), lay_a.coef(kc), lay_a.coef(qs)),
            lay_a.rows_in(ua[:, 2 * A_WIDTH:3 * A_WIDTH]),
            lay_a.state_in(state_a, rows_last=True), groups_per_coef=lay_a.groups_per_coef)
        o_a = lay_a.rows_out(o_c)
        new_hgrn = lay_a.state_out(sa_c, rows_last=True)

    x1 = _mix(x2, o_a, ua, y_b, bonus, gb, wts["norm_mix_g"], wts["w_in_g"],
              wts["hgrn_norm_w"], wts["ln_x_w"],
              wts["ln_x_b"], wts["same_head_a"], wts["same_head_b"], wts["w_a_out"],
              wts["w_b_out"], wts["w_out"])
    y = _mlp(x1, wts["norm_mlp_g"], wts["w_up"], wts["w_down"], wts["norm_final_g"])
    if time_major:
        y = y.reshape(seq, batch, D_MODEL).transpose(1, 0, 2)
    else:
        y = y.reshape(batch, seq, D_MODEL)
    new_shift = _norm_rows(x[:, -1, :], wts["norm_mix_g"])
    return y, new_hgrn[None], new_wkv[None], new_shift[None]


def kernel(x_prompt, x_sample, state_hgrn, state_wkv, state_shift, norm_mix_g, w_in, mu_shift,
           w_decay0, w_decay_up, a0, w_aaa_up, w_gate_up, k_k, k_a, r_k, ln_x_w, ln_x_b,
           lb_logits, hgrn_norm_w, w_a_out, w_b_out, w_out, norm_mlp_g, w_up, w_down,
           norm_final_g):
    assert w_in.shape[0] == 1, "single-layer stack"
    def pcols(a):
        lead = a.shape[:-1]
        return a.reshape(*lead, B_HEADS, B_HEAD).swapaxes(-1, -2).reshape(*lead, B_WIDTH)

    prows = lambda a: a.reshape(B_HEADS, B_HEAD, -1).swapaxes(0, 1).reshape(B_WIDTH, -1)
    w_in0 = w_in[0]
    w_in_b = w_in0[:, A_COLS:A_COLS + B_COLS]
    w_in_b = jnp.concatenate(
        [pcols(w_in_b[:, s * B_WIDTH:(s + 1) * B_WIDTH]) for s in range(3)]
        + [w_in_b[:, 3 * B_WIDTH:]], axis=1)
    mu = mu_shift[0]
    mu = jnp.concatenate([pcols(mu[s * B_WIDTH:(s + 1) * B_WIDTH]) for s in range(3)]
                         + [mu[3 * B_WIDTH:]])
    w_lora = jnp.zeros((LORA_COLS, 3 * B_WIDTH), F32)
    w_lora = w_lora.at[:DECAY_LORA, :B_WIDTH].set(pcols(w_decay_up[0]))
    w_lora = w_lora.at[DECAY_LORA:DECAY_LORA + AAA_LORA, B_WIDTH:2 * B_WIDTH].set(
        pcols(w_aaa_up[0]))
    w_lora = w_lora.at[DECAY_LORA + AAA_LORA:, 2 * B_WIDTH:].set(pcols(w_gate_up[0]))
    row = lambda a: a.reshape(1, -1).astype(F32)
    prow = lambda a: row(pcols(a.reshape(-1)))
    wts = {
        "norm_mix_g": row(norm_mix_g[0]),
        "w_in_a": w_in0[:, :A_COLS].astype(BF16),
        "w_in_b": w_in_b.astype(BF16),
        "w_in_g": w_in0[:, A_COLS + B_COLS:].astype(BF16),
        "mu_shift": row(mu),
        "w_lora": w_lora.astype(BF16),
        "w_decay0": prow(w_decay0[0]),
        "a0": prow(a0[0]),
        "k_k": prow(k_k[0]),
        "k_a": prow(k_a[0]),
        "r_k": prow(r_k[0]),
        "ln_x_w": prow(ln_x_w[0]),
        "ln_x_b": prow(ln_x_b[0]),
        "lb_logits": lb_logits.astype(F32),
        "hgrn_norm_w": row(hgrn_norm_w[0]),
        "w_a_out": w_a_out[0].astype(BF16),
        "w_b_out": prows(w_b_out[0]).astype(BF16),
        "w_out": w_out[0].astype(BF16),
        "norm_mlp_g": row(norm_mlp_g[0]),
        "w_up": w_up[0].astype(BF16),
        "w_down": w_down[0].astype(BF16),
        "norm_final_g": row(norm_final_g),
        "same_head_a": _same_head_matrix(A_WIDTH, A_HEADS, head_minor=False),
        "same_head_b": _same_head_matrix(B_WIDTH, B_HEADS, head_minor=True),
    }
    bp = x_prompt.shape[0]
    y_p, hgrn_p, wkv_p, shift_p = _trunk(
        x_prompt, jnp.zeros((bp, D_MODEL), F32),
        jnp.zeros((bp, A_HEADS, A_HEAD, A_HEAD), F32),
        jnp.zeros((bp, B_HEADS, B_HEAD, B_HEAD), F32), wts, time_major=False)
    y_s, hgrn_s, wkv_s, shift_s = _trunk(
        x_sample, state_shift[0], state_hgrn[0], state_wkv[0], wts, time_major=True)
    return (y_p, y_s, hgrn_p, wkv_p, shift_p, hgrn_s, wkv_s, shift_s)
```

```python
import functools

import jax
import jax.numpy as jnp
import numpy as np
from jax import lax
from jax.experimental import pallas as pl
from jax.experimental.pallas import tpu as pltpu

F32 = jnp.float32
BF16 = jnp.bfloat16

D_MODEL = 1024
A_WIDTH = 512
A_HEADS = 4
A_HEAD = 128
B_WIDTH = 512
B_HEADS = 8
B_HEAD = 64
DECAY_LORA = 64
AAA_LORA = 64
GATE_LORA = 128
LORA_COLS = DECAY_LORA + AAA_LORA + GATE_LORA
D_FF = 4 * D_MODEL
A_COLS = 4 * A_WIDTH
B_COLS = 3 * B_WIDTH + LORA_COLS
GATE_COLS = 2 * D_MODEL
NORM_EPS = 1e-6
HGRN_NORM_EPS = 1e-5
GN_EPS = 64e-5
DECAY_SCALE = 0.6065306597126334

V7X_LANES = 128
V7X_SUBLANES = 8
V7X_VMEM_LIMIT_BYTES = 56 * 1024 * 1024

TOKEN_TILE = 256
SUBTILES = 2
TIME_BLOCK = 64
ROWS_PER_GROUP = 64


def _params(*semantics):
    return pltpu.CompilerParams(dimension_semantics=semantics,
                                vmem_limit_bytes=V7X_VMEM_LIMIT_BYTES)


def _full(shape):
    return pl.BlockSpec(shape, lambda *_: (0,) * len(shape))


def _sub_tile_rows(rows):
    sub = rows // SUBTILES
    return [pl.ds(part * sub, sub) for part in range(SUBTILES)]


def _round_robin(streams):
    streams = list(streams)
    while streams:
        streams = [g for g in streams if next(g, StopIteration) is not StopIteration]


def _rmsnorm(x, g):
    return x * lax.rsqrt(jnp.mean(x * x, axis=-1, keepdims=True) + NORM_EPS) * g


def _bdot(a, w):
    return jnp.dot(a.astype(BF16), w, preferred_element_type=F32)


def _head_sum_rolled(x, heads):
    tiles = x.shape[1] // V7X_LANES
    t = x[:, 0:V7X_LANES]
    for c in range(1, tiles):
        t = t + x[:, c * V7X_LANES:(c + 1) * V7X_LANES]
    shift = heads
    while shift < V7X_LANES:
        t = t + pltpu.roll(t, shift, 1)
        shift *= 2
    return jnp.concatenate([t] * tiles, axis=1)


def _head_sum_lanes(x, head):
    out = []
    for h in range(x.shape[1] // head):
        seg = x[:, h * head:(h + 1) * head]
        out.append(jnp.broadcast_to(jnp.sum(seg, axis=-1, keepdims=True), seg.shape))
    return jnp.concatenate(out, axis=1)


def _head_sum(a, same_head):
    hi = a.astype(BF16)
    lo = (a - hi.astype(F32)).astype(BF16)
    return (jnp.dot(hi, same_head, preferred_element_type=F32)
            + jnp.dot(lo, same_head, preferred_element_type=F32))


def _norm_proj_kernel(x_ref, g_ref, wa_ref, wb_ref, ua_ref, ub_ref):
    def sub_tile(rows):
        hb = _rmsnorm(x_ref[rows, :], g_ref[...]).astype(BF16)
        yield
        ua_ref[rows, :] = jnp.dot(hb, wa_ref[...], preferred_element_type=F32)
        yield
        ub_ref[rows, :] = jnp.dot(hb, wb_ref[...], preferred_element_type=F32)

    _round_robin(sub_tile(rows) for rows in _sub_tile_rows(x_ref.shape[0]))


def _norm_proj(x, g, wa, wb):
    n = x.shape[0]
    tm = TOKEN_TILE * SUBTILES
    row = lambda c: pl.BlockSpec((tm, c), lambda i: (i, 0))
    return pl.pallas_call(
        _norm_proj_kernel,
        grid=(n // tm,),
        in_specs=[row(D_MODEL), _full((1, D_MODEL)), _full((D_MODEL, A_COLS)),
                  _full((D_MODEL, B_COLS))],
        out_specs=[row(A_COLS), row(B_COLS)],
        out_shape=[jax.ShapeDtypeStruct((n, A_COLS), F32),
                   jax.ShapeDtypeStruct((n, B_COLS), F32)],
        compiler_params=_params("arbitrary"),
        name="norm_proj",
    )(x, g, wa, wb)


def _norm_rows_kernel(x_ref, g_ref, o_ref):
    o_ref[...] = _rmsnorm(x_ref[...], g_ref[...])


def _norm_rows(x, g):
    return pl.pallas_call(
        _norm_rows_kernel,
        out_shape=jax.ShapeDtypeStruct(x.shape, F32),
        name="norm_rows",
    )(x, g)


def _prev_proj_kernel(h_ref, w_ref, o_ref):
    o_ref[...] = _bdot(h_ref[...], w_ref[...])


def _prev_proj(h_prev, wb):
    return pl.pallas_call(
        _prev_proj_kernel,
        out_shape=jax.ShapeDtypeStruct((h_prev.shape[0], B_COLS), F32),
        compiler_params=_params(),
        name="prev_proj",
    )(h_prev, wb)


def _rwkv_coefficients(ub, up, mu_ref, wlora_ref, wd0_ref, a0_ref, kk_w_ref, ka_ref, rk_ref,
                       same_head_ref):
    xm = ub + (up - ub) * mu_ref[...]
    r = xm[:, 0:B_WIDTH]
    k = xm[:, B_WIDTH:2 * B_WIDTH]
    v = xm[:, 2 * B_WIDTH:3 * B_WIDTH]
    lo = xm[:, 3 * B_WIDTH:]
    col = lax.broadcasted_iota(jnp.int32, lo.shape, 1)
    act = jnp.where(col < DECAY_LORA, jnp.tanh(lo),
                    jnp.where(col < DECAY_LORA + AAA_LORA, lo, jax.nn.sigmoid(lo)))
    up_proj = _bdot(act, wlora_ref[...])
    logw = -DECAY_SCALE * jax.nn.sigmoid(wd0_ref[...] + up_proj[:, 0:B_WIDTH])
    a = jax.nn.sigmoid(a0_ref[...] + up_proj[:, B_WIDTH:2 * B_WIDTH])
    same_head = same_head_ref[...]
    kk = k * kk_w_ref[...]
    kk = kk / jnp.maximum(jnp.sqrt(_head_sum(kk * kk, same_head)), 1e-12)
    k2 = k * (1.0 + (a - 1.0) * ka_ref[...])
    g = up_proj[:, 2 * B_WIDTH:]
    bonus = _head_sum(r * k2 * rk_ref[...], same_head) * v
    return (kk, jnp.exp(logw), kk * a, k2, r, v), g, bonus


def _rwkv_prep_kernel(ub_ref, p0_ref, mu_ref, wlora_ref, wd0_ref, a0_ref, kk_w_ref, ka_ref,
                      rk_ref, same_head_ref, kk_ref, w_ref, b_ref, k_ref, r_ref, v_ref, g_ref,
                      bonus_ref, *, time_stride):
    ub = ub_ref[...]
    up = jnp.concatenate([p0_ref[...], ub[:ub.shape[0] - time_stride, :]], axis=0)
    coefs, g, bonus = _rwkv_coefficients(ub, up, mu_ref, wlora_ref, wd0_ref, a0_ref, kk_w_ref,
                                         ka_ref, rk_ref, same_head_ref)
    for ref, val in zip((kk_ref, w_ref, b_ref, k_ref, r_ref, v_ref), coefs):
        ref[...] = val
    g_ref[...] = g
    bonus_ref[...] = bonus


def _rwkv_prep(ub, p0, prm, *, time_stride):
    n = ub.shape[0]
    return pl.pallas_call(
        functools.partial(_rwkv_prep_kernel, time_stride=time_stride),
        out_shape=[jax.ShapeDtypeStruct((n, B_WIDTH), F32)] * 8,
        compiler_params=_params(),
        name="rwkv_prep",
    )(ub, p0, *prm)


CHAIN_TILE = 128
N_COEF = 5


def _rwkv_prep_chain_kernel(ub_ref, p0_ref, mu_ref, wlora_ref, wd0_ref, a0_ref, kk_w_ref, ka_ref,
                            rk_ref, same_head_ref, coef_ref, v_ref, g_ref, bonus_ref, carry_ref,
                            xt_ref, *, batch, parts):
    tile = pl.program_id(0)
    which = pl.program_id(1)
    half_rows = B_HEAD // parts

    @pl.when(which == 0)
    def _():
        def per_seq(b, carry):
            ub = ub_ref[b]
            first = jnp.where(tile == 0, p0_ref[pl.ds(b, 1), :], carry_ref[pl.ds(b, 1), :])
            row_id = lax.broadcasted_iota(jnp.int32, ub.shape, 0)
            up = jnp.where(row_id == 0, first, pltpu.roll(ub, 1, 0))
            carry_ref[pl.ds(b, 1), :] = ub[CHAIN_TILE - 1:CHAIN_TILE, :]
            coefs, g, bonus = _rwkv_coefficients(ub, up, mu_ref, wlora_ref, wd0_ref, a0_ref,
                                                 kk_w_ref, ka_ref, rk_ref, same_head_ref)
            g_ref[b] = g
            bonus_ref[b] = bonus
            for idx, val in enumerate(coefs):
                xt_ref[idx, b] = val.T
            return carry

        lax.fori_loop(0, batch, per_seq, 0)
        for i in range(half_rows):
            m = jnp.concatenate(
                [xt_ref[N_COEF, :, pl.ds((p * half_rows + i) * B_HEADS, B_HEADS), :]
                 .reshape(batch * B_HEADS, CHAIN_TILE) for p in range(parts)], axis=0)
            v_ref[:, i, :] = m.T

    for j in range(B_HEAD):
        m = xt_ref[which, :, pl.ds(j * B_HEADS, B_HEADS), :].reshape(batch * B_HEADS, CHAIN_TILE)
        coef_ref[j] = jnp.concatenate([m] * parts, axis=0).T


def _rwkv_prep_chain(ub, p0, prm, *, batch, seq):
    parts = V7X_LANES // (batch * B_HEADS)
    n_tiles = seq // CHAIN_TILE
    tok = pl.BlockSpec((batch, CHAIN_TILE, B_WIDTH), lambda t, a: (0, t, 0))
    return pl.pallas_call(
        functools.partial(_rwkv_prep_chain_kernel, batch=batch, parts=parts),
        grid=(n_tiles, N_COEF),
        in_specs=[pl.BlockSpec((batch, CHAIN_TILE, B_COLS), lambda t, a: (0, t, 0)),
                  _full(p0.shape)] + [_full(p.shape) for p in prm],
        out_specs=[pl.BlockSpec((None, B_HEAD, CHAIN_TILE, V7X_LANES), lambda t, a: (a, 0, t, 0)),
                   pl.BlockSpec((CHAIN_TILE, B_HEAD // parts, V7X_LANES), lambda t, a: (t, 0, 0)),
                   tok, tok],
        out_shape=[jax.ShapeDtypeStruct((N_COEF, B_HEAD, seq, V7X_LANES), F32),
                   jax.ShapeDtypeStruct((seq, B_HEAD // parts, V7X_LANES), F32),
                   jax.ShapeDtypeStruct((batch, seq, B_WIDTH), F32),
                   jax.ShapeDtypeStruct((batch, seq, B_WIDTH), F32)],
        scratch_shapes=[pltpu.VMEM((batch, B_COLS), F32),
                        pltpu.VMEM((N_COEF + 1, batch, B_WIDTH, CHAIN_TILE), F32)],
        compiler_params=_params("arbitrary", "arbitrary"),
        name="rwkv_prep_chain",
    )(ub.reshape(batch, seq, B_COLS), p0, *prm)


def _rwkv_unpack_kernel(y_ref, o_ref, yt_ref, *, batch, parts):
    half_rows = B_HEAD // parts
    for i in range(half_rows):
        nt = y_ref[:, i, :].T
        for p in range(parts):
            rows = slice(p * batch * B_HEADS, (p + 1) * batch * B_HEADS)
            yt_ref[:, pl.ds((p * half_rows + i) * B_HEADS, B_HEADS), :] = (
                nt[rows].reshape(batch, B_HEADS, CHAIN_TILE))
    for b in range(batch):
        o_ref[b] = yt_ref[b].T


def _rwkv_unpack(y, *, batch, seq):
    parts = V7X_LANES // (batch * B_HEADS)
    out = pl.pallas_call(
        functools.partial(_rwkv_unpack_kernel, batch=batch, parts=parts),
        grid=(seq // CHAIN_TILE,),
        in_specs=[pl.BlockSpec((CHAIN_TILE, B_HEAD // parts, V7X_LANES), lambda t: (t, 0, 0))],
        out_specs=pl.BlockSpec((batch, CHAIN_TILE, B_WIDTH), lambda t: (0, t, 0)),
        out_shape=jax.ShapeDtypeStruct((batch, seq, B_WIDTH), F32),
        scratch_shapes=[pltpu.VMEM((batch, B_WIDTH, CHAIN_TILE), F32)],
        compiler_params=_params("arbitrary"),
        name="rwkv_unpack",
    )(y)
    return out.reshape(batch * seq, B_WIDTH)


def _hgrn_prep_kernel(q_ref, f_ref, lbl_ref, qs_ref, fg_ref, kc_ref):
    logits = lbl_ref[...]
    e = jnp.exp(logits - jnp.max(logits, axis=0, keepdims=True))
    lb = e[0:1, :] / jnp.sum(e, axis=0, keepdims=True)
    fz = f_ref[...]
    fg_ref[...] = lb + (1.0 - lb) * jax.nn.sigmoid(fz)
    kc_ref[...] = (1.0 - lb) * jax.nn.sigmoid(-fz)
    qs_ref[...] = jax.nn.silu(q_ref[...])


def _hgrn_prep(ua, lb_logits):
    n = ua.shape[0]
    tm = TOKEN_TILE
    col = lambda j: pl.BlockSpec((tm, A_WIDTH), lambda i: (i, j))
    return pl.pallas_call(
        _hgrn_prep_kernel,
        grid=(n // tm,),
        in_specs=[col(0), col(1), _full(lb_logits.shape)],
        out_specs=[col(0)] * 3,
        out_shape=[jax.ShapeDtypeStruct((n, A_WIDTH), F32)] * 3,
        compiler_params=_params("arbitrary"),
        name="hgrn_prep",
    )(ua, ua, lb_logits)


HGRN_CHUNK = 64


def _hgrn_tables(chunk):
    levels = chunk.bit_length() - 1
    t = np.arange(chunk)
    u, tt = t[None, :], t[:, None]
    rows = [u <= tt, u > tt]
    masks = []
    for level in range(levels):
        m = 1 << level
        anchor = (t // (2 * m)) * (2 * m) + m - 1
        right = (t % (2 * m)) >= m
        rows.append(((u > anchor[:, None]) & (u <= tt) & right[:, None])
                    | ((u > tt) & (u <= anchor[:, None]) & ~right[:, None]))
        masks.append((tt // (2 * m) == u // (2 * m)) & right[:, None] & ~right[None, :])
    return (jnp.asarray(np.concatenate(rows, 0), BF16),
            jnp.asarray(np.stack(masks), F32), levels)


def _split3(x):
    hi = x.astype(BF16)
    r1 = x - hi.astype(F32)
    mid = r1.astype(BF16)
    lo = (r1 - mid.astype(F32)).astype(BF16)
    return hi, mid, lo


def _dot_nt(a, b):
    return lax.dot_general(a.astype(BF16), b.astype(BF16), (((1,), (1,)), ((), ())),
                           preferred_element_type=F32)


def _dot_tn(a, b):
    return lax.dot_general(a.astype(BF16), b.astype(BF16), (((0,), (0,)), ((), ())),
                           preferred_element_type=F32)


def _hgrn_lower_bound(lbl_ref):
    logits = lbl_ref[...]
    e = jnp.exp(logits - jnp.max(logits, axis=0, keepdims=True))
    return e[0:1, :] / jnp.sum(e, axis=0, keepdims=True)


def _hgrn_chunk_kernel(q_ref, f_ref, i_ref, lbl_ref, sums_ref, mask_ref, s0_ref, o_ref,
                       s_out_ref, st_ref, *, chunk, n_chunks, levels):
    n_seqs = st_ref.shape[0]

    @pl.when(pl.program_id(1) == 0)
    def _():
        for s in range(n_seqs):
            for h in range(A_HEADS):
                st_ref[s, h] = s0_ref[s, h].T

    lb = _hgrn_lower_bound(lbl_ref)
    sums = sums_ref[...]
    seg = lambda r: slice(r * chunk, (r + 1) * chunk)

    def one_head(s, h, rows):
        hs = slice(h * A_HEAD, (h + 1) * A_HEAD)
        fz = f_ref[s, rows, hs]
        lbh = lb[:, hs]
        logf = jnp.log(lbh + (1.0 - lbh) * jax.nn.sigmoid(fz))
        kh = (1.0 - lbh) * jax.nn.sigmoid(-fz)
        qh = jax.nn.silu(q_ref[s, rows, hs])
        vh = i_ref[s, rows, hs]
        yield
        decay = jnp.exp(sum(jnp.dot(sums, part, preferred_element_type=F32)
                            for part in _split3(logf)))
        yield
        att = None
        for level in range(levels):
            split = decay[seg(2 + level)]
            term = mask_ref[level] * _dot_nt(qh * split, kh * split)
            att = term if att is None else att + term
            if level % 2 == 1:
                yield
        st = st_ref[s, h]
        o_ref[s, rows, hs] = (_bdot(att, vh.astype(BF16))
                              + jnp.sum(qh * kh, axis=-1, keepdims=True) * vh
                              + _dot_nt(qh * decay[seg(0)], st))
        yield
        st_ref[s, h] = st * decay[chunk - 1:chunk] + _dot_tn(vh, kh * decay[seg(1)])
        yield

    for c in range(n_chunks):
        _round_robin(one_head(s, h, pl.ds(c * chunk, chunk))
                     for h in range(A_HEADS) for s in range(n_seqs))

    @pl.when(pl.program_id(1) == pl.num_programs(1) - 1)
    def _():
        for s in range(n_seqs):
            for h in range(A_HEADS):
                s_out_ref[s, h] = st_ref[s, h].T


HGRN_SEQS_PER_STEP = 2


def _hgrn_chunked(ua, lb_logits, s0, *, batch, seq):
    chunk = HGRN_CHUNK
    tile = TOKEN_TILE
    n_seqs = HGRN_SEQS_PER_STEP
    sums, masks, levels = _hgrn_tables(chunk)
    col = lambda j: pl.BlockSpec((n_seqs, tile, A_WIDTH), lambda b, t: (b, t, j))
    state_spec = pl.BlockSpec((n_seqs, A_HEADS, A_HEAD, A_HEAD), lambda b, t: (b, 0, 0, 0))
    out, state = pl.pallas_call(
        functools.partial(_hgrn_chunk_kernel, chunk=chunk, n_chunks=tile // chunk, levels=levels),
        grid=(batch // n_seqs, seq // tile),
        in_specs=[col(0), col(1), col(2), _full(lb_logits.shape), _full(sums.shape),
                  _full(masks.shape), state_spec],
        out_specs=[col(0), state_spec],
        out_shape=[jax.ShapeDtypeStruct((batch, seq, A_WIDTH), F32),
                   jax.ShapeDtypeStruct(s0.shape, F32)],
        scratch_shapes=[pltpu.VMEM((n_seqs, A_HEADS, A_HEAD, A_HEAD), F32)],
        compiler_params=_params("arbitrary", "arbitrary"),
        name="hgrn_chunked",
    )(*[ua.reshape(batch, seq, A_COLS)] * 3, lb_logits, sums, masks, s0)
    return out.reshape(batch * seq, A_WIDTH), state


def _bcast_row(ref, t, j):
    return jnp.broadcast_to(ref[t, pl.ds(j, 1), :], (V7X_SUBLANES, V7X_LANES))


def _delta_rule_kernel(kk0_ref, kkn_ref, w_ref, b_ref, k_ref, r_ref, v_ref, s0_ref,
                       y_ref, s_ref, sk_ref, *, nib, nj, tb):
    @pl.when(pl.program_id(1) == 0)
    def _():
        s_ref[...] = s0_ref[...]
        for ib in range(nib):
            acc = None
            for j in range(nj):
                kk0 = jnp.broadcast_to(kk0_ref[pl.ds(j, 1), :], (V7X_SUBLANES, V7X_LANES))
                term = s0_ref[ib, j] * kk0
                acc = term if acc is None else acc + term
            sk_ref[ib] = acc

    def step(t, carry):
        sk = [sk_ref[ib] for ib in range(nib)]
        vv = [v_ref[t, ib] for ib in range(nib)]
        yacc = [None] * nib
        skn = [None] * nib
        for j in range(nj):
            wj = _bcast_row(w_ref, t, j)
            bj = _bcast_row(b_ref, t, j)
            kj = _bcast_row(k_ref, t, j)
            rj = _bcast_row(r_ref, t, j)
            nj_kk = _bcast_row(kkn_ref, t, j)
            for ib in range(nib):
                s = s_ref[ib, j] * wj - sk[ib] * bj + vv[ib] * kj
                s_ref[ib, j] = s
                yt = s * rj
                st = s * nj_kk
                yacc[ib] = yt if yacc[ib] is None else yacc[ib] + yt
                skn[ib] = st if skn[ib] is None else skn[ib] + st
        for ib in range(nib):
            y_ref[t, ib] = yacc[ib]
            sk_ref[ib] = skn[ib]
        return carry

    lax.fori_loop(0, tb, step, 0)


def _delta_rule_chain_kernel(kk_ref, w_ref, b_ref, k_ref, r_ref, kk_head_ref, v_ref, s0_ref,
                             y_ref, s_ref, sk_ref, kkx_ref, *, nib, nj, tb):
    full = (V7X_SUBLANES, V7X_LANES)
    row = lambda ref, j, t: jnp.broadcast_to(ref[j, pl.ds(t, 1), :], full)

    @pl.when(pl.program_id(0) == 0)
    def _():
        s_ref[...] = s0_ref[...]
        for ib in range(nib):
            acc = None
            for j in range(nj):
                term = s0_ref[ib, j] * row(kk_ref, j, 0)
                acc = term if acc is None else acc + term
            sk_ref[ib] = acc

    kkx_ref[:, 0:tb, :] = kk_ref[...]
    kkx_ref[:, tb:tb + V7X_SUBLANES, :] = kk_head_ref[...]

    def step(t, carry):
        sk = [sk_ref[ib] for ib in range(nib)]
        vv = [v_ref[t, ib] for ib in range(nib)]
        yacc = [None] * nib
        skn = [None] * nib
        for j in range(nj):
            wj, bj, kj, rj = (row(ref, j, t) for ref in (w_ref, b_ref, k_ref, r_ref))
            kkn = row(kkx_ref, j, t + 1)
            for ib in range(nib):
                s = s_ref[ib, j] * wj - sk[ib] * bj + vv[ib] * kj
                s_ref[ib, j] = s
                yt = s * rj
                st = s * kkn
                yacc[ib] = yt if yacc[ib] is None else yacc[ib] + yt
                skn[ib] = st if skn[ib] is None else skn[ib] + st
        for ib in range(nib):
            y_ref[t, ib] = yacc[ib]
            sk_ref[ib] = skn[ib]
        return carry

    lax.fori_loop(0, tb, step, 0)


def _delta_rule_chain(coef, v, s0):
    _, nj, t_len, _ = coef.shape
    nib = v.shape[1]
    tb = TIME_BLOCK
    n_blocks = t_len // tb
    head_rows = V7X_SUBLANES
    coef_spec = lambda a: pl.BlockSpec((None, nj, tb, V7X_LANES), lambda t: (a, 0, t, 0))
    head_spec = pl.BlockSpec(
        (None, nj, head_rows, V7X_LANES),
        lambda t: (0, 0, jnp.minimum(t + 1, n_blocks - 1) * (tb // head_rows), 0))
    row_spec = pl.BlockSpec((tb, nib, V7X_SUBLANES, V7X_LANES), lambda t: (t, 0, 0, 0))
    return pl.pallas_call(
        functools.partial(_delta_rule_chain_kernel, nib=nib, nj=nj, tb=tb),
        grid=(n_blocks,),
        in_specs=[coef_spec(a) for a in range(N_COEF)]
        + [head_spec, row_spec, _full(s0.shape)],
        out_specs=[row_spec, _full(s0.shape)],
        out_shape=[jax.ShapeDtypeStruct(v.shape, F32), jax.ShapeDtypeStruct(s0.shape, F32)],
        scratch_shapes=[pltpu.VMEM((nib, V7X_SUBLANES, V7X_LANES), F32),
                        pltpu.VMEM((nj, tb + head_rows, V7X_LANES), F32)],
        compiler_params=_params("arbitrary"),
        name="delta_rule_chain",
    )(coef, coef, coef, coef, coef, coef, v, s0)


def _decay_rule_kernel(w_ref, k_ref, r_ref, v_ref, s0_ref, y_ref, s_ref, *, nib, nj, tb):
    @pl.when(pl.program_id(1) == 0)
    def _():
        s_ref[...] = s0_ref[...]

    def step(t, carry):
        vv = [v_ref[t, ib] for ib in range(nib)]
        yacc = [None] * nib
        for j in range(nj):
            wj = _bcast_row(w_ref, t, j)
            kj = _bcast_row(k_ref, t, j)
            rj = _bcast_row(r_ref, t, j)
            for ib in range(nib):
                s = s_ref[ib, j] * wj + vv[ib] * kj
                s_ref[ib, j] = s
                yt = s * rj
                yacc[ib] = yt if yacc[ib] is None else yacc[ib] + yt
        for ib in range(nib):
            y_ref[t, ib] = yacc[ib]
        return carry

    lax.fori_loop(0, tb, step, 0)


def _recurrence(coefs, v, s0, *, kk0=None, groups_per_coef):
    t_len, q, nib = v.shape[0], v.shape[1], v.shape[2]
    nj = coefs[0].shape[2]
    tb = min(TIME_BLOCK, t_len)
    coef_spec = pl.BlockSpec((tb, None, nj, V7X_LANES),
                             lambda g, t: (t, g // groups_per_coef, 0, 0))
    row_spec = pl.BlockSpec((tb, None, nib, V7X_SUBLANES, V7X_LANES),
                            lambda g, t: (t, g, 0, 0, 0))
    state_spec = pl.BlockSpec((None, nib, nj, V7X_SUBLANES, V7X_LANES),
                              lambda g, t: (g, 0, 0, 0, 0))
    delta = kk0 is not None
    if delta:
        body = functools.partial(_delta_rule_kernel, nib=nib, nj=nj, tb=tb)
        kk0_spec = pl.BlockSpec((None, nj, V7X_LANES), lambda g, t: (g // groups_per_coef, 0, 0))
        in_specs = [kk0_spec] + [coef_spec] * 5 + [row_spec, state_spec]
        args = (kk0,) + tuple(coefs) + (v, s0)
        scratch = [pltpu.VMEM((nib, V7X_SUBLANES, V7X_LANES), F32)]
    else:
        body = functools.partial(_decay_rule_kernel, nib=nib, nj=nj, tb=tb)
        in_specs = [coef_spec] * 3 + [row_spec, state_spec]
        args = tuple(coefs) + (v, s0)
        scratch = []
    return pl.pallas_call(
        body,
        grid=(q, t_len // tb),
        in_specs=in_specs,
        out_specs=[row_spec, state_spec],
        out_shape=[jax.ShapeDtypeStruct(v.shape, F32), jax.ShapeDtypeStruct(s0.shape, F32)],
        scratch_shapes=scratch,
        compiler_params=_params("arbitrary", "arbitrary"),
        name="delta_rule" if delta else "decay_rule",
    )(*args)


def _mix_kernel(x_ref, oa_ref, ga_ref, y_ref, bonus_ref, gb_ref, gn_ref, wg_ref, hw_ref, lnw_ref,
                lnb_ref, wa_ref, wb_ref, wo_ref, x1_ref):
    def sub_tile(rows):
        x = x_ref[rows, :]
        gate = jax.nn.sigmoid(_bdot(_rmsnorm(x, gn_ref[...]), wg_ref[...]))
        yield
        oa = oa_ref[rows, :]
        ms = _head_sum_lanes(oa * oa, A_HEAD) * (1.0 / A_HEAD)
        oa = oa * lax.rsqrt(ms + HGRN_NORM_EPS) * hw_ref[...] * jax.nn.silu(ga_ref[rows, :])
        ya = _bdot(oa, wa_ref[...])
        yield
        y = y_ref[rows, :]
        d = y - _head_sum_rolled(y, B_HEADS) * (1.0 / B_HEAD)
        var = _head_sum_rolled(d * d, B_HEADS) * (1.0 / B_HEAD)
        yn = d * lax.rsqrt(var + GN_EPS) * lnw_ref[...] + lnb_ref[...]
        ob = (yn + bonus_ref[rows, :]) * gb_ref[rows, :]
        yb = _bdot(ob, wb_ref[...])
        yield
        merged = gate[:, 0:D_MODEL] * ya + gate[:, D_MODEL:GATE_COLS] * yb
        x1_ref[rows, :] = x + _bdot(merged, wo_ref[...])

    _round_robin(sub_tile(rows) for rows in _sub_tile_rows(x_ref.shape[0]))


def _mix(x, oa, ua, y, bonus, gb, gn, wg, hw, lnw, lnb, wa, wb, wo):
    n = x.shape[0]
    tm = TOKEN_TILE * SUBTILES
    row = lambda c: pl.BlockSpec((tm, c), lambda i: (i, 0))
    vec = _full((1, A_WIDTH))
    return pl.pallas_call(
        _mix_kernel,
        grid=(n // tm,),
        in_specs=[row(D_MODEL), row(A_WIDTH), pl.BlockSpec((tm, A_WIDTH), lambda i: (i, 3)),
                  row(B_WIDTH), row(B_WIDTH), row(B_WIDTH), _full((1, D_MODEL)),
                  _full((D_MODEL, GATE_COLS)), vec, vec, vec,
                  _full((A_WIDTH, D_MODEL)), _full((B_WIDTH, D_MODEL)),
                  _full((D_MODEL, D_MODEL))],
        out_specs=row(D_MODEL),
        out_shape=jax.ShapeDtypeStruct((n, D_MODEL), F32),
        compiler_params=_params("arbitrary"),
        name="mix",
    )(x, oa, ua, y, bonus, gb, gn, wg, hw, lnw, lnb, wa, wb, wo)


def _mlp_kernel(x_ref, g_ref, wu_ref, wd_ref, gf_ref, o_ref):
    def sub_tile(rows):
        x1 = x_ref[rows, :]
        hb = _rmsnorm(x1, g_ref[...]).astype(BF16)
        yield
        act = jnp.square(jnp.maximum(jnp.dot(hb, wu_ref[...], preferred_element_type=F32), 0.0))
        yield
        x2 = x1 + _bdot(act, wd_ref[...])
        yield
        o_ref[rows, :] = _rmsnorm(x2, gf_ref[...])

    _round_robin(sub_tile(rows) for rows in _sub_tile_rows(x_ref.shape[0]))


def _mlp(x1, g, wu, wd, gf):
    n = x1.shape[0]
    tm = TOKEN_TILE * SUBTILES
    row = pl.BlockSpec((tm, D_MODEL), lambda i: (i, 0))
    return pl.pallas_call(
        _mlp_kernel,
        grid=(n // tm,),
        in_specs=[row, _full((1, D_MODEL)), _full((D_MODEL, D_FF)), _full((D_FF, D_MODEL)),
                  _full((1, D_MODEL))],
        out_specs=row,
        out_shape=jax.ShapeDtypeStruct((n, D_MODEL), F32),
        compiler_params=_params("arbitrary"),
        name="mlp",
    )(x1, g, wu, wd, gf)


class _SampleLayout:
    def __init__(self, batch, seq, heads, nj, ni, head_minor):
        assert batch == V7X_LANES
        self.b, self.t, self.h, self.nj, self.ni = batch, seq, heads, nj, ni
        self.head_minor = head_minor
        self.groups_per_coef = ni // ROWS_PER_GROUP
        self.nib = ROWS_PER_GROUP // V7X_SUBLANES

    def _to_chain(self, x, width):
        b, t, h = self.b, self.t, self.h
        if self.head_minor:
            return x.reshape(t, b, width, h).transpose(0, 3, 2, 1)
        return x.reshape(t, b, h, width).transpose(0, 2, 3, 1)

    def coef(self, x):
        return self._to_chain(x, self.nj)

    def rows_in(self, v):
        v = self._to_chain(v, self.ni)
        return v.reshape(self.t, self.h * self.groups_per_coef, self.nib, V7X_SUBLANES, V7X_LANES)

    def rows_out(self, y):
        b, t, h, ni = self.b, self.t, self.h, self.ni
        y = y.reshape(t, h, ni, b)
        y = y.transpose(0, 3, 2, 1) if self.head_minor else y.transpose(0, 3, 1, 2)
        return y.reshape(t * b, h * ni)

    def state_in(self, s, rows_last):
        b, h, nj = self.b, self.h, self.nj
        if not rows_last:
            s = s.transpose(0, 1, 3, 2)
        s = s.reshape(b, h, nj, self.groups_per_coef, self.nib, V7X_SUBLANES)
        s = s.transpose(1, 3, 4, 2, 5, 0)
        return s.reshape(h * self.groups_per_coef, self.nib, nj, V7X_SUBLANES, V7X_LANES)

    def state_out(self, s, rows_last):
        b, h, ni, nj = self.b, self.h, self.ni, self.nj
        s = s.reshape(h, self.groups_per_coef, self.nib, nj, V7X_SUBLANES, b)
        s = s.transpose(5, 0, 3, 1, 2, 4).reshape(b, h, nj, ni)
        return s if rows_last else s.transpose(0, 1, 3, 2)


def _prompt_state_to_chain(s, parts):
    b, h, ni, nj = s.shape
    nib = ni // parts // V7X_SUBLANES
    s = s.reshape(b, h, parts, nib, V7X_SUBLANES, nj).transpose(3, 5, 4, 2, 0, 1)
    return s.reshape(nib, nj, V7X_SUBLANES, V7X_LANES)


def _prompt_state_from_chain(s, batch, heads, parts):
    nib, nj = s.shape[0], s.shape[1]
    s = s.reshape(nib, nj, V7X_SUBLANES, parts, batch, heads).transpose(4, 5, 3, 0, 2, 1)
    return s.reshape(batch, heads, parts * nib * V7X_SUBLANES, nj)


def _same_head_matrix(width, heads):
    idx = np.arange(width) % heads
    return jnp.asarray(idx[:, None] == idx[None, :], BF16)


def _trunk(x, shift0, state_a, state_b, wts, *, time_major):
    batch, seq, _ = x.shape
    n = batch * seq
    if time_major:
        x2 = x.transpose(1, 0, 2).reshape(n, D_MODEL)
        time_stride = batch
    else:
        x2 = x.reshape(n, D_MODEL)
        time_stride = 1

    ua, ub = _norm_proj(x2, wts["norm_mix_g"], wts["w_in_a"], wts["w_in_b"])
    p0 = _prev_proj(shift0, wts["w_in_b"])
    prep_params = (wts["mu_shift"], wts["w_lora"], wts["w_decay0"], wts["a0"], wts["k_k"],
                   wts["k_a"], wts["r_k"], wts["same_head_b"])

    if time_major:
        kk, w, bco, k2, r, v, gb, bonus = _rwkv_prep(ub, p0, prep_params,
                                                     time_stride=time_stride)
        lay_b = _SampleLayout(batch, seq, B_HEADS, B_HEAD, B_HEAD, head_minor=True)
        kk_c = lay_b.coef(kk)
        kk_next = jnp.concatenate([kk_c[1:], jnp.zeros_like(kk_c[:1])], axis=0)
        y_c, sb_c = _recurrence(
            (kk_next, lay_b.coef(w), lay_b.coef(bco), lay_b.coef(k2), lay_b.coef(r)),
            lay_b.rows_in(v), lay_b.state_in(state_b, rows_last=False), kk0=kk_c[0],
            groups_per_coef=lay_b.groups_per_coef)
        y_b = lay_b.rows_out(y_c)
        new_wkv = lay_b.state_out(sb_c, rows_last=False)
    else:
        parts = V7X_LANES // (batch * B_HEADS)
        coef, v_c, gb, bonus = _rwkv_prep_chain(ub, p0, prep_params, batch=batch, seq=seq)
        gb = gb.reshape(n, B_WIDTH)
        bonus = bonus.reshape(n, B_WIDTH)
        nib = B_HEAD // parts // V7X_SUBLANES
        y_c, sb_c = _delta_rule_chain(
            coef, v_c.reshape(seq, nib, V7X_SUBLANES, V7X_LANES),
            _prompt_state_to_chain(state_b, parts))
        o_a, new_hgrn = _hgrn_chunked(ua, wts["lb_logits"], state_a, batch=batch, seq=seq)
        y_b = _rwkv_unpack(y_c.reshape(seq, nib * V7X_SUBLANES, V7X_LANES), batch=batch, seq=seq)
        new_wkv = _prompt_state_from_chain(sb_c, batch, B_HEADS, parts)

    if time_major:
        qs, fg, kc = _hgrn_prep(ua, wts["lb_logits"])
        lay_a = _SampleLayout(batch, seq, A_HEADS, A_HEAD, A_HEAD, head_minor=False)
        o_c, sa_c = _recurrence(
            (lay_a.coef(fg), lay_a.coef(kc), lay_a.coef(qs)),
            lay_a.rows_in(ua[:, 2 * A_WIDTH:3 * A_WIDTH]),
            lay_a.state_in(state_a, rows_last=True), groups_per_coef=lay_a.groups_per_coef)
        o_a = lay_a.rows_out(o_c)
        new_hgrn = lay_a.state_out(sa_c, rows_last=True)

    x1 = _mix(x2, o_a, ua, y_b, bonus, gb, wts["norm_mix_g"], wts["w_in_g"],
              wts["hgrn_norm_w"], wts["ln_x_w"],
              wts["ln_x_b"], wts["w_a_out"], wts["w_b_out"], wts["w_out"])
    y = _mlp(x1, wts["norm_mlp_g"], wts["w_up"], wts["w_down"], wts["norm_final_g"])
    if time_major:
        y = y.reshape(seq, batch, D_MODEL).transpose(1, 0, 2)
    else:
        y = y.reshape(batch, seq, D_MODEL)
    new_shift = _norm_rows(x[:, -1, :], wts["norm_mix_g"])
    return y, new_hgrn[None], new_wkv[None], new_shift[None]


def kernel(x_prompt, x_sample, state_hgrn, state_wkv, state_shift, norm_mix_g, w_in, mu_shift,
           w_decay0, w_decay_up, a0, w_aaa_up, w_gate_up, k_k, k_a, r_k, ln_x_w, ln_x_b,
           lb_logits, hgrn_norm_w, w_a_out, w_b_out, w_out, norm_mlp_g, w_up, w_down,
           norm_final_g):
    assert w_in.shape[0] == 1, "single-layer stack"
    def pcols(a):
        lead = a.shape[:-1]
        return a.reshape(*lead, B_HEADS, B_HEAD).swapaxes(-1, -2).reshape(*lead, B_WIDTH)

    prows = lambda a: a.reshape(B_HEADS, B_HEAD, -1).swapaxes(0, 1).reshape(B_WIDTH, -1)
    w_in0 = w_in[0]
    w_in_b = w_in0[:, A_COLS:A_COLS + B_COLS]
    w_in_b = jnp.concatenate(
        [pcols(w_in_b[:, s * B_WIDTH:(s + 1) * B_WIDTH]) for s in range(3)]
        + [w_in_b[:, 3 * B_WIDTH:]], axis=1)
    mu = mu_shift[0]
    mu = jnp.concatenate([pcols(mu[s * B_WIDTH:(s + 1) * B_WIDTH]) for s in range(3)]
                         + [mu[3 * B_WIDTH:]])
    w_lora = jnp.zeros((LORA_COLS, 3 * B_WIDTH), F32)
    w_lora = w_lora.at[:DECAY_LORA, :B_WIDTH].set(pcols(w_decay_up[0]))
    w_lora = w_lora.at[DECAY_LORA:DECAY_LORA + AAA_LORA, B_WIDTH:2 * B_WIDTH].set(
        pcols(w_aaa_up[0]))
    w_lora = w_lora.at[DECAY_LORA + AAA_LORA:, 2 * B_WIDTH:].set(pcols(w_gate_up[0]))
    row = lambda a: a.reshape(1, -1).astype(F32)
    prow = lambda a: row(pcols(a.reshape(-1)))
    wts = {
        "norm_mix_g": row(norm_mix_g[0]),
        "w_in_a": w_in0[:, :A_COLS].astype(BF16),
        "w_in_b": w_in_b.astype(BF16),
        "w_in_g": w_in0[:, A_COLS + B_COLS:].astype(BF16),
        "mu_shift": row(mu),
        "w_lora": w_lora.astype(BF16),
        "w_decay0": prow(w_decay0[0]),
        "a0": prow(a0[0]),
        "k_k": prow(k_k[0]),
        "k_a": prow(k_a[0]),
        "r_k": prow(r_k[0]),
        "ln_x_w": prow(ln_x_w[0]),
        "ln_x_b": prow(ln_x_b[0]),
        "lb_logits": lb_logits.astype(F32),
        "hgrn_norm_w": row(hgrn_norm_w[0]),
        "w_a_out": w_a_out[0].astype(BF16),
        "w_b_out": prows(w_b_out[0]).astype(BF16),
        "w_out": w_out[0].astype(BF16),
        "norm_mlp_g": row(norm_mlp_g[0]),
        "w_up": w_up[0].astype(BF16),
        "w_down": w_down[0].astype(BF16),
        "norm_final_g": row(norm_final_g),
        "same_head_b": _same_head_matrix(B_WIDTH, B_HEADS),
    }
    bp = x_prompt.shape[0]
    y_p, hgrn_p, wkv_p, shift_p = _trunk(
        x_prompt, jnp.zeros((bp, D_MODEL), F32),
        jnp.zeros((bp, A_HEADS, A_HEAD, A_HEAD), F32),
        jnp.zeros((bp, B_HEADS, B_HEAD, B_HEAD), F32), wts, time_major=False)
    y_s, hgrn_s, wkv_s, shift_s = _trunk(
        x_sample, state_shift[0], state_hgrn[0], state_wkv[0], wts, time_major=True)
    return (y_p, y_s, hgrn_p, wkv_p, shift_p, hgrn_s, wkv_s, shift_s)
```

```python
import functools

import jax
import jax.numpy as jnp
import numpy as np
from jax import lax
from jax.experimental import pallas as pl
from jax.experimental.pallas import tpu as pltpu

F32 = jnp.float32
BF16 = jnp.bfloat16

D_MODEL = 1024
A_WIDTH = 512
A_HEADS = 4
A_HEAD = 128
B_WIDTH = 512
B_HEADS = 8
B_HEAD = 64
DECAY_LORA = 64
AAA_LORA = 64
GATE_LORA = 128
LORA_COLS = DECAY_LORA + AAA_LORA + GATE_LORA
D_FF = 4 * D_MODEL
A_COLS = 4 * A_WIDTH
B_COLS = 3 * B_WIDTH + LORA_COLS
GATE_COLS = 2 * D_MODEL
NORM_EPS = 1e-6
HGRN_NORM_EPS = 1e-5
GN_EPS = 64e-5
DECAY_SCALE = 0.6065306597126334

V7X_LANES = 128
V7X_SUBLANES = 8
V7X_VMEM_LIMIT_BYTES = 56 * 1024 * 1024

TOKEN_TILE = 256
SUBTILES = 2
TIME_BLOCK = 64
ROWS_PER_GROUP = 64


def _params(*semantics):
    return pltpu.CompilerParams(dimension_semantics=semantics,
                                vmem_limit_bytes=V7X_VMEM_LIMIT_BYTES)


def _full(shape):
    return pl.BlockSpec(shape, lambda *_: (0,) * len(shape))


def _sub_tile_rows(rows):
    sub = rows // SUBTILES
    return [pl.ds(part * sub, sub) for part in range(SUBTILES)]


def _round_robin(streams):
    streams = list(streams)
    while streams:
        streams = [g for g in streams if next(g, StopIteration) is not StopIteration]


def _rmsnorm(x, g):
    return x * lax.rsqrt(jnp.mean(x * x, axis=-1, keepdims=True) + NORM_EPS) * g


def _bdot(a, w):
    return jnp.dot(a.astype(BF16), w, preferred_element_type=F32)


def _head_sum_rolled(x, heads):
    tiles = x.shape[1] // V7X_LANES
    t = x[:, 0:V7X_LANES]
    for c in range(1, tiles):
        t = t + x[:, c * V7X_LANES:(c + 1) * V7X_LANES]
    shift = heads
    while shift < V7X_LANES:
        t = t + pltpu.roll(t, shift, 1)
        shift *= 2
    return jnp.concatenate([t] * tiles, axis=1)


def _head_sum_lanes(x, head):
    out = []
    for h in range(x.shape[1] // head):
        seg = x[:, h * head:(h + 1) * head]
        out.append(jnp.broadcast_to(jnp.sum(seg, axis=-1, keepdims=True), seg.shape))
    return jnp.concatenate(out, axis=1)


def _head_sum(a, same_head):
    hi = a.astype(BF16)
    lo = (a - hi.astype(F32)).astype(BF16)
    return (jnp.dot(hi, same_head, preferred_element_type=F32)
            + jnp.dot(lo, same_head, preferred_element_type=F32))


def _norm_proj_kernel(x_ref, g_ref, wa_ref, wb_ref, ua_ref, ub_ref):
    def sub_tile(rows):
        hb = _rmsnorm(x_ref[rows, :], g_ref[...]).astype(BF16)
        yield
        ua_ref[rows, :] = jnp.dot(hb, wa_ref[...], preferred_element_type=F32)
        yield
        ub_ref[rows, :] = jnp.dot(hb, wb_ref[...], preferred_element_type=F32)

    _round_robin(sub_tile(rows) for rows in _sub_tile_rows(x_ref.shape[0]))


def _norm_proj(x, g, wa, wb):
    n = x.shape[0]
    tm = TOKEN_TILE * SUBTILES
    row = lambda c: pl.BlockSpec((tm, c), lambda i: (i, 0))
    return pl.pallas_call(
        _norm_proj_kernel,
        grid=(n // tm,),
        in_specs=[row(D_MODEL), _full((1, D_MODEL)), _full((D_MODEL, A_COLS)),
                  _full((D_MODEL, B_COLS))],
        out_specs=[row(A_COLS), row(B_COLS)],
        out_shape=[jax.ShapeDtypeStruct((n, A_COLS), F32),
                   jax.ShapeDtypeStruct((n, B_COLS), F32)],
        compiler_params=_params("arbitrary"),
        name="norm_proj",
    )(x, g, wa, wb)


def _norm_rows_kernel(x_ref, g_ref, o_ref):
    o_ref[...] = _rmsnorm(x_ref[...], g_ref[...])


def _norm_rows(x, g):
    return pl.pallas_call(
        _norm_rows_kernel,
        out_shape=jax.ShapeDtypeStruct(x.shape, F32),
        name="norm_rows",
    )(x, g)


def _prev_proj_kernel(h_ref, w_ref, o_ref):
    o_ref[...] = _bdot(h_ref[...], w_ref[...])


def _prev_proj(h_prev, wb):
    return pl.pallas_call(
        _prev_proj_kernel,
        out_shape=jax.ShapeDtypeStruct((h_prev.shape[0], B_COLS), F32),
        compiler_params=_params(),
        name="prev_proj",
    )(h_prev, wb)


def _rwkv_coefficients(ub, up, mu_ref, wlora_ref, wd0_ref, a0_ref, kk_w_ref, ka_ref, rk_ref,
                       same_head_ref):
    xm = ub + (up - ub) * mu_ref[...]
    r = xm[:, 0:B_WIDTH]
    k = xm[:, B_WIDTH:2 * B_WIDTH]
    v = xm[:, 2 * B_WIDTH:3 * B_WIDTH]
    lo = xm[:, 3 * B_WIDTH:]
    col = lax.broadcasted_iota(jnp.int32, lo.shape, 1)
    act = jnp.where(col < DECAY_LORA, jnp.tanh(lo),
                    jnp.where(col < DECAY_LORA + AAA_LORA, lo, jax.nn.sigmoid(lo)))
    yield
    up_proj = _bdot(act, wlora_ref[...])
    yield
    logw = -DECAY_SCALE * jax.nn.sigmoid(wd0_ref[...] + up_proj[:, 0:B_WIDTH])
    a = jax.nn.sigmoid(a0_ref[...] + up_proj[:, B_WIDTH:2 * B_WIDTH])
    same_head = same_head_ref[...]
    kk = k * kk_w_ref[...]
    norm2 = _head_sum(kk * kk, same_head)
    yield
    kk = kk / jnp.maximum(jnp.sqrt(norm2), 1e-12)
    k2 = k * (1.0 + (a - 1.0) * ka_ref[...])
    g = up_proj[:, 2 * B_WIDTH:]
    rk_sum = _head_sum(r * k2 * rk_ref[...], same_head)
    yield
    return (kk, logw, kk * a, k2, r, v), g, rk_sum * v


def _run_to_end(gen):
    while True:
        try:
            next(gen)
        except StopIteration as done:
            return done.value


def _rwkv_prep_kernel(ub_ref, p0_ref, mu_ref, wlora_ref, wd0_ref, a0_ref, kk_w_ref, ka_ref,
                      rk_ref, same_head_ref, kk_ref, w_ref, b_ref, k_ref, r_ref, v_ref, g_ref,
                      bonus_ref, *, time_stride):
    ub = ub_ref[...]
    up = jnp.concatenate([p0_ref[...], ub[:ub.shape[0] - time_stride, :]], axis=0)
    coefs, g, bonus = _run_to_end(_rwkv_coefficients(
        ub, up, mu_ref, wlora_ref, wd0_ref, a0_ref, kk_w_ref, ka_ref, rk_ref, same_head_ref))
    kk, logw, bco, k2, r, v = coefs
    for ref, val in zip((kk_ref, w_ref, b_ref, k_ref, r_ref, v_ref),
                        (kk, jnp.exp(logw), bco, k2, r, v)):
        ref[...] = val
    g_ref[...] = g
    bonus_ref[...] = bonus


def _rwkv_prep(ub, p0, prm, *, time_stride):
    n = ub.shape[0]
    return pl.pallas_call(
        functools.partial(_rwkv_prep_kernel, time_stride=time_stride),
        out_shape=[jax.ShapeDtypeStruct((n, B_WIDTH), F32)] * 8,
        compiler_params=_params(),
        name="rwkv_prep",
    )(ub, p0, *prm)


CHAIN_TILE = 128
N_COEF = 5
PREP_SEQS_PER_ITER = 2


def _block_prefix_ones(rows, block):
    t = np.arange(rows)
    return jnp.asarray((t[None, :] <= t[:, None]) & (t[None, :] // block == t[:, None] // block),
                       BF16)


def _rwkv_prep_chain_kernel(ub_ref, p0_ref, mu_ref, wlora_ref, wd0_ref, a0_ref, kk_w_ref, ka_ref,
                            rk_ref, same_head_ref, prefix_ref, coef_ref, v_ref, g_ref, bonus_ref,
                            carry_ref, xt_ref, *, batch, parts):
    tile = pl.program_id(0)
    which = pl.program_id(1)
    half_rows = B_HEAD // parts

    @pl.when(which == 0)
    def _():
        def per_seq(b):
            ub = ub_ref[b]
            first = jnp.where(tile == 0, p0_ref[pl.ds(b, 1), :], carry_ref[pl.ds(b, 1), :])
            row_id = lax.broadcasted_iota(jnp.int32, ub.shape, 0)
            up = jnp.where(row_id == 0, first, pltpu.roll(ub, 1, 0))
            carry_ref[pl.ds(b, 1), :] = ub[CHAIN_TILE - 1:CHAIN_TILE, :]
            coefs, g, bonus = yield from _rwkv_coefficients(
                ub, up, mu_ref, wlora_ref, wd0_ref, a0_ref, kk_w_ref, ka_ref, rk_ref,
                same_head_ref)
            g_ref[b] = g
            bonus_ref[b] = bonus
            kk, logw, bco, k2, r, v = coefs
            cum = sum(jnp.dot(prefix_ref[...], part, preferred_element_type=F32)
                      for part in _split3(logw))
            yield
            gamma = jnp.exp(cum)
            inv_gamma = jnp.exp(-cum)
            scaled = (kk * jnp.exp(cum - logw), gamma, bco * inv_gamma, k2 * inv_gamma,
                      r * gamma, v)
            for idx, val in enumerate(scaled):
                xt_ref[idx, b] = val.T
                yield

        def per_group(group, carry):
            _round_robin(per_seq(group * PREP_SEQS_PER_ITER + s)
                         for s in range(PREP_SEQS_PER_ITER))
            return carry

        lax.fori_loop(0, batch // PREP_SEQS_PER_ITER, per_group, 0)
        for i in range(half_rows):
            m = jnp.concatenate(
                [xt_ref[N_COEF, :, pl.ds((p * half_rows + i) * B_HEADS, B_HEADS), :]
                 .reshape(batch * B_HEADS, CHAIN_TILE) for p in range(parts)], axis=0)
            v_ref[:, i, :] = m.T

    for j in range(B_HEAD):
        m = xt_ref[which, :, pl.ds(j * B_HEADS, B_HEADS), :].reshape(batch * B_HEADS, CHAIN_TILE)
        coef_ref[j] = jnp.concatenate([m] * parts, axis=0).T


def _rwkv_prep_chain(ub, p0, prm, *, batch, seq):
    parts = V7X_LANES // (batch * B_HEADS)
    n_tiles = seq // CHAIN_TILE
    tok = pl.BlockSpec((batch, CHAIN_TILE, B_WIDTH), lambda t, a: (0, t, 0))
    prm = tuple(prm) + (_block_prefix_ones(CHAIN_TILE, TIME_BLOCK),)
    return pl.pallas_call(
        functools.partial(_rwkv_prep_chain_kernel, batch=batch, parts=parts),
        grid=(n_tiles, N_COEF),
        in_specs=[pl.BlockSpec((batch, CHAIN_TILE, B_COLS), lambda t, a: (0, t, 0)),
                  _full(p0.shape)] + [_full(p.shape) for p in prm],
        out_specs=[pl.BlockSpec((None, B_HEAD, CHAIN_TILE, V7X_LANES), lambda t, a: (a, 0, t, 0)),
                   pl.BlockSpec((CHAIN_TILE, B_HEAD // parts, V7X_LANES), lambda t, a: (t, 0, 0)),
                   tok, tok],
        out_shape=[jax.ShapeDtypeStruct((N_COEF, B_HEAD, seq, V7X_LANES), F32),
                   jax.ShapeDtypeStruct((seq, B_HEAD // parts, V7X_LANES), F32),
                   jax.ShapeDtypeStruct((batch, seq, B_WIDTH), F32),
                   jax.ShapeDtypeStruct((batch, seq, B_WIDTH), F32)],
        scratch_shapes=[pltpu.VMEM((batch, B_COLS), F32),
                        pltpu.VMEM((N_COEF + 1, batch, B_WIDTH, CHAIN_TILE), F32)],
        compiler_params=_params("arbitrary", "arbitrary"),
        name="rwkv_prep_chain",
    )(ub.reshape(batch, seq, B_COLS), p0, *prm)


def _rwkv_unpack_kernel(y_ref, o_ref, yt_ref, *, batch, parts):
    half_rows = B_HEAD // parts
    for i in range(half_rows):
        nt = y_ref[:, i, :].T
        for p in range(parts):
            rows = slice(p * batch * B_HEADS, (p + 1) * batch * B_HEADS)
            yt_ref[:, pl.ds((p * half_rows + i) * B_HEADS, B_HEADS), :] = (
                nt[rows].reshape(batch, B_HEADS, CHAIN_TILE))
    for b in range(batch):
        o_ref[b] = yt_ref[b].T


def _rwkv_unpack(y, *, batch, seq):
    parts = V7X_LANES // (batch * B_HEADS)
    out = pl.pallas_call(
        functools.partial(_rwkv_unpack_kernel, batch=batch, parts=parts),
        grid=(seq // CHAIN_TILE,),
        in_specs=[pl.BlockSpec((CHAIN_TILE, B_HEAD // parts, V7X_LANES), lambda t: (t, 0, 0))],
        out_specs=pl.BlockSpec((batch, CHAIN_TILE, B_WIDTH), lambda t: (0, t, 0)),
        out_shape=jax.ShapeDtypeStruct((batch, seq, B_WIDTH), F32),
        scratch_shapes=[pltpu.VMEM((batch, B_WIDTH, CHAIN_TILE), F32)],
        compiler_params=_params("arbitrary"),
        name="rwkv_unpack",
    )(y)
    return out.reshape(batch * seq, B_WIDTH)


def _hgrn_prep_kernel(q_ref, f_ref, lbl_ref, qs_ref, fg_ref, kc_ref):
    logits = lbl_ref[...]
    e = jnp.exp(logits - jnp.max(logits, axis=0, keepdims=True))
    lb = e[0:1, :] / jnp.sum(e, axis=0, keepdims=True)
    fz = f_ref[...]
    fg_ref[...] = lb + (1.0 - lb) * jax.nn.sigmoid(fz)
    kc_ref[...] = (1.0 - lb) * jax.nn.sigmoid(-fz)
    qs_ref[...] = jax.nn.silu(q_ref[...])


def _hgrn_prep(ua, lb_logits):
    n = ua.shape[0]
    tm = TOKEN_TILE
    col = lambda j: pl.BlockSpec((tm, A_WIDTH), lambda i: (i, j))
    return pl.pallas_call(
        _hgrn_prep_kernel,
        grid=(n // tm,),
        in_specs=[col(0), col(1), _full(lb_logits.shape)],
        out_specs=[col(0)] * 3,
        out_shape=[jax.ShapeDtypeStruct((n, A_WIDTH), F32)] * 3,
        compiler_params=_params("arbitrary"),
        name="hgrn_prep",
    )(ua, ua, lb_logits)


HGRN_CHUNK = 64


def _hgrn_tables(chunk):
    levels = chunk.bit_length() - 1
    t = np.arange(chunk)
    u, tt = t[None, :], t[:, None]
    rows = [u <= tt, u > tt]
    masks = []
    for level in range(levels):
        m = 1 << level
        anchor = (t // (2 * m)) * (2 * m) + m - 1
        right = (t % (2 * m)) >= m
        rows.append(((u > anchor[:, None]) & (u <= tt) & right[:, None])
                    | ((u > tt) & (u <= anchor[:, None]) & ~right[:, None]))
        masks.append((tt // (2 * m) == u // (2 * m)) & right[:, None] & ~right[None, :])
    return (jnp.asarray(np.concatenate(rows, 0), BF16),
            jnp.asarray(np.stack(masks), F32), levels)


def _split3(x):
    hi = x.astype(BF16)
    r1 = x - hi.astype(F32)
    mid = r1.astype(BF16)
    lo = (r1 - mid.astype(F32)).astype(BF16)
    return hi, mid, lo


def _dot_nt(a, b):
    return lax.dot_general(a.astype(BF16), b.astype(BF16), (((1,), (1,)), ((), ())),
                           preferred_element_type=F32)


def _dot_tn(a, b):
    return lax.dot_general(a.astype(BF16), b.astype(BF16), (((0,), (0,)), ((), ())),
                           preferred_element_type=F32)


def _hgrn_lower_bound(lbl_ref):
    logits = lbl_ref[...]
    e = jnp.exp(logits - jnp.max(logits, axis=0, keepdims=True))
    return e[0:1, :] / jnp.sum(e, axis=0, keepdims=True)


def _hgrn_chunk_kernel(q_ref, f_ref, i_ref, lbl_ref, sums_ref, mask_ref, s0_ref, o_ref,
                       s_out_ref, st_ref, *, chunk, n_chunks, levels):
    n_seqs = st_ref.shape[0]

    @pl.when(pl.program_id(1) == 0)
    def _():
        for s in range(n_seqs):
            for h in range(A_HEADS):
                st_ref[s, h] = s0_ref[s, h].T

    lb = _hgrn_lower_bound(lbl_ref)
    sums = sums_ref[...]
    seg = lambda r: slice(r * chunk, (r + 1) * chunk)

    def one_head(s, h, rows):
        hs = slice(h * A_HEAD, (h + 1) * A_HEAD)
        fz = f_ref[s, rows, hs]
        lbh = lb[:, hs]
        logf = jnp.log(lbh + (1.0 - lbh) * jax.nn.sigmoid(fz))
        kh = (1.0 - lbh) * jax.nn.sigmoid(-fz)
        qh = jax.nn.silu(q_ref[s, rows, hs])
        vh = i_ref[s, rows, hs]
        yield
        decay = jnp.exp(sum(jnp.dot(sums, part, preferred_element_type=F32)
                            for part in _split3(logf)))
        yield
        att = None
        for level in range(levels):
            split = decay[seg(2 + level)]
            term = mask_ref[level] * _dot_nt(qh * split, kh * split)
            att = term if att is None else att + term
            if level % 2 == 1:
                yield
        st = st_ref[s, h]
        o_ref[s, rows, hs] = (_bdot(att, vh.astype(BF16))
                              + jnp.sum(qh * kh, axis=-1, keepdims=True) * vh
                              + _dot_nt(qh * decay[seg(0)], st))
        yield
        st_ref[s, h] = st * decay[chunk - 1:chunk] + _dot_tn(vh, kh * decay[seg(1)])
        yield

    for c in range(n_chunks):
        _round_robin(one_head(s, h, pl.ds(c * chunk, chunk))
                     for h in range(A_HEADS) for s in range(n_seqs))

    @pl.when(pl.program_id(1) == pl.num_programs(1) - 1)
    def _():
        for s in range(n_seqs):
            for h in range(A_HEADS):
                s_out_ref[s, h] = st_ref[s, h].T


HGRN_SEQS_PER_STEP = 2


def _hgrn_chunked(ua, lb_logits, s0, *, batch, seq):
    chunk = HGRN_CHUNK
    tile = TOKEN_TILE
    n_seqs = HGRN_SEQS_PER_STEP
    sums, masks, levels = _hgrn_tables(chunk)
    col = lambda j: pl.BlockSpec((n_seqs, tile, A_WIDTH), lambda b, t: (b, t, j))
    state_spec = pl.BlockSpec((n_seqs, A_HEADS, A_HEAD, A_HEAD), lambda b, t: (b, 0, 0, 0))
    out, state = pl.pallas_call(
        functools.partial(_hgrn_chunk_kernel, chunk=chunk, n_chunks=tile // chunk, levels=levels),
        grid=(batch // n_seqs, seq // tile),
        in_specs=[col(0), col(1), col(2), _full(lb_logits.shape), _full(sums.shape),
                  _full(masks.shape), state_spec],
        out_specs=[col(0), state_spec],
        out_shape=[jax.ShapeDtypeStruct((batch, seq, A_WIDTH), F32),
                   jax.ShapeDtypeStruct(s0.shape, F32)],
        scratch_shapes=[pltpu.VMEM((n_seqs, A_HEADS, A_HEAD, A_HEAD), F32)],
        compiler_params=_params("arbitrary", "arbitrary"),
        name="hgrn_chunked",
    )(*[ua.reshape(batch, seq, A_COLS)] * 3, lb_logits, sums, masks, s0)
    return out.reshape(batch * seq, A_WIDTH), state


def _bcast_row(ref, t, j):
    return jnp.broadcast_to(ref[t, pl.ds(j, 1), :], (V7X_SUBLANES, V7X_LANES))


def _delta_rule_kernel(kk0_ref, kkn_ref, w_ref, b_ref, k_ref, r_ref, v_ref, s0_ref,
                       y_ref, s_ref, sk_ref, *, nib, nj, tb):
    @pl.when(pl.program_id(1) == 0)
    def _():
        s_ref[...] = s0_ref[...]
        for ib in range(nib):
            acc = None
            for j in range(nj):
                kk0 = jnp.broadcast_to(kk0_ref[pl.ds(j, 1), :], (V7X_SUBLANES, V7X_LANES))
                term = s0_ref[ib, j] * kk0
                acc = term if acc is None else acc + term
            sk_ref[ib] = acc

    def step(t, carry):
        sk = [sk_ref[ib] for ib in range(nib)]
        vv = [v_ref[t, ib] for ib in range(nib)]
        yacc = [None] * nib
        skn = [None] * nib
        for j in range(nj):
            wj = _bcast_row(w_ref, t, j)
            bj = _bcast_row(b_ref, t, j)
            kj = _bcast_row(k_ref, t, j)
            rj = _bcast_row(r_ref, t, j)
            nj_kk = _bcast_row(kkn_ref, t, j)
            for ib in range(nib):
                s = s_ref[ib, j] * wj - sk[ib] * bj + vv[ib] * kj
                s_ref[ib, j] = s
                yt = s * rj
                st = s * nj_kk
                yacc[ib] = yt if yacc[ib] is None else yacc[ib] + yt
                skn[ib] = st if skn[ib] is None else skn[ib] + st
        for ib in range(nib):
            y_ref[t, ib] = yacc[ib]
            sk_ref[ib] = skn[ib]
        return carry

    lax.fori_loop(0, tb, step, 0)


def _delta_rule_chain_kernel(kk_ref, gamma_ref, b_ref, k_ref, r_ref, kk_head_ref, v_ref, s0_ref,
                             y_ref, s_ref, sk_ref, kkx_ref, *, nib, nj, tb):
    full = (V7X_SUBLANES, V7X_LANES)
    row = lambda ref, j, t: jnp.broadcast_to(ref[j, pl.ds(t, 1), :], full)

    @pl.when(pl.program_id(0) == 0)
    def _():
        s_ref[...] = s0_ref[...]
        for ib in range(nib):
            acc = None
            for j in range(nj):
                term = s0_ref[ib, j] * row(kk_ref, j, 0)
                acc = term if acc is None else acc + term
            sk_ref[ib] = acc

    kkx_ref[:, 0:tb, :] = kk_ref[...]
    kkx_ref[:, tb:tb + V7X_SUBLANES, :] = (
        kk_head_ref[...] * gamma_ref[:, tb - 1:tb, :])

    def step(t, carry):
        sk = [sk_ref[ib] for ib in range(nib)]
        vv = [v_ref[t, ib] for ib in range(nib)]
        yacc = [None] * nib
        skn = [None] * nib
        for j in range(nj):
            bj, kj, rj = (row(ref, j, t) for ref in (b_ref, k_ref, r_ref))
            kkn = row(kkx_ref, j, t + 1)
            for ib in range(nib):
                s = s_ref[ib, j] + (vv[ib] * kj - sk[ib] * bj)
                s_ref[ib, j] = s
                yt = s * rj
                st = s * kkn
                yacc[ib] = yt if yacc[ib] is None else yacc[ib] + yt
                skn[ib] = st if skn[ib] is None else skn[ib] + st
        for ib in range(nib):
            y_ref[t, ib] = yacc[ib]
            sk_ref[ib] = skn[ib]
        return carry

    lax.fori_loop(0, tb, step, 0)
    for j in range(nj):
        total = row(gamma_ref, j, tb - 1)
        for ib in range(nib):
            s_ref[ib, j] = s_ref[ib, j] * total


def _delta_rule_chain(coef, v, s0):
    _, nj, t_len, _ = coef.shape
    nib = v.shape[1]
    tb = TIME_BLOCK
    n_blocks = t_len // tb
    head_rows = V7X_SUBLANES
    coef_spec = lambda a: pl.BlockSpec((None, nj, tb, V7X_LANES), lambda t: (a, 0, t, 0))
    head_spec = pl.BlockSpec(
        (None, nj, head_rows, V7X_LANES),
        lambda t: (0, 0, jnp.minimum(t + 1, n_blocks - 1) * (tb // head_rows), 0))
    row_spec = pl.BlockSpec((tb, nib, V7X_SUBLANES, V7X_LANES), lambda t: (t, 0, 0, 0))
    return pl.pallas_call(
        functools.partial(_delta_rule_chain_kernel, nib=nib, nj=nj, tb=tb),
        grid=(n_blocks,),
        in_specs=[coef_spec(a) for a in range(N_COEF)]
        + [head_spec, row_spec, _full(s0.shape)],
        out_specs=[row_spec, _full(s0.shape)],
        out_shape=[jax.ShapeDtypeStruct(v.shape, F32), jax.ShapeDtypeStruct(s0.shape, F32)],
        scratch_shapes=[pltpu.VMEM((nib, V7X_SUBLANES, V7X_LANES), F32),
                        pltpu.VMEM((nj, tb + head_rows, V7X_LANES), F32)],
        compiler_params=_params("arbitrary"),
        name="delta_rule_chain",
    )(coef, coef, coef, coef, coef, coef, v, s0)


def _decay_rule_kernel(w_ref, k_ref, r_ref, v_ref, s0_ref, y_ref, s_ref, *, nib, nj, tb):
    @pl.when(pl.program_id(1) == 0)
    def _():
        s_ref[...] = s0_ref[...]

    def step(t, carry):
        vv = [v_ref[t, ib] for ib in range(nib)]
        yacc = [None] * nib
        for j in range(nj):
            wj = _bcast_row(w_ref, t, j)
            kj = _bcast_row(k_ref, t, j)
            rj = _bcast_row(r_ref, t, j)
            for ib in range(nib):
                s = s_ref[ib, j] * wj + vv[ib] * kj
                s_ref[ib, j] = s
                yt = s * rj
                yacc[ib] = yt if yacc[ib] is None else yacc[ib] + yt
        for ib in range(nib):
            y_ref[t, ib] = yacc[ib]
        return carry

    lax.fori_loop(0, tb, step, 0)


def _recurrence(coefs, v, s0, *, kk0=None, groups_per_coef):
    t_len, q, nib = v.shape[0], v.shape[1], v.shape[2]
    nj = coefs[0].shape[2]
    tb = min(TIME_BLOCK, t_len)
    coef_spec = pl.BlockSpec((tb, None, nj, V7X_LANES),
                             lambda g, t: (t, g // groups_per_coef, 0, 0))
    row_spec = pl.BlockSpec((tb, None, nib, V7X_SUBLANES, V7X_LANES),
                            lambda g, t: (t, g, 0, 0, 0))
    state_spec = pl.BlockSpec((None, nib, nj, V7X_SUBLANES, V7X_LANES),
                              lambda g, t: (g, 0, 0, 0, 0))
    delta = kk0 is not None
    if delta:
        body = functools.partial(_delta_rule_kernel, nib=nib, nj=nj, tb=tb)
        kk0_spec = pl.BlockSpec((None, nj, V7X_LANES), lambda g, t: (g // groups_per_coef, 0, 0))
        in_specs = [kk0_spec] + [coef_spec] * 5 + [row_spec, state_spec]
        args = (kk0,) + tuple(coefs) + (v, s0)
        scratch = [pltpu.VMEM((nib, V7X_SUBLANES, V7X_LANES), F32)]
    else:
        body = functools.partial(_decay_rule_kernel, nib=nib, nj=nj, tb=tb)
        in_specs = [coef_spec] * 3 + [row_spec, state_spec]
        args = tuple(coefs) + (v, s0)
        scratch = []
    return pl.pallas_call(
        body,
        grid=(q, t_len // tb),
        in_specs=in_specs,
        out_specs=[row_spec, state_spec],
        out_shape=[jax.ShapeDtypeStruct(v.shape, F32), jax.ShapeDtypeStruct(s0.shape, F32)],
        scratch_shapes=scratch,
        compiler_params=_params("arbitrary", "arbitrary"),
        name="delta_rule" if delta else "decay_rule",
    )(*args)


def _mix_kernel(x_ref, oa_ref, ga_ref, y_ref, bonus_ref, gb_ref, gn_ref, wg_ref, hw_ref, lnw_ref,
                lnb_ref, wa_ref, wb_ref, wo_ref, x1_ref):
    def sub_tile(rows):
        x = x_ref[rows, :]
        gate = jax.nn.sigmoid(_bdot(_rmsnorm(x, gn_ref[...]), wg_ref[...]))
        yield
        oa = oa_ref[rows, :]
        ms = _head_sum_lanes(oa * oa, A_HEAD) * (1.0 / A_HEAD)
        oa = oa * lax.rsqrt(ms + HGRN_NORM_EPS) * hw_ref[...] * jax.nn.silu(ga_ref[rows, :])
        ya = _bdot(oa, wa_ref[...])
        yield
        y = y_ref[rows, :]
        d = y - _head_sum_rolled(y, B_HEADS) * (1.0 / B_HEAD)
        var = _head_sum_rolled(d * d, B_HEADS) * (1.0 / B_HEAD)
        yn = d * lax.rsqrt(var + GN_EPS) * lnw_ref[...] + lnb_ref[...]
        ob = (yn + bonus_ref[rows, :]) * gb_ref[rows, :]
        yb = _bdot(ob, wb_ref[...])
        yield
        merged = gate[:, 0:D_MODEL] * ya + gate[:, D_MODEL:GATE_COLS] * yb
        x1_ref[rows, :] = x + _bdot(merged, wo_ref[...])

    _round_robin(sub_tile(rows) for rows in _sub_tile_rows(x_ref.shape[0]))


def _mix(x, oa, ua, y, bonus, gb, gn, wg, hw, lnw, lnb, wa, wb, wo):
    n = x.shape[0]
    tm = TOKEN_TILE * SUBTILES
    row = lambda c: pl.BlockSpec((tm, c), lambda i: (i, 0))
    vec = _full((1, A_WIDTH))
    return pl.pallas_call(
        _mix_kernel,
        grid=(n // tm,),
        in_specs=[row(D_MODEL), row(A_WIDTH), pl.BlockSpec((tm, A_WIDTH), lambda i: (i, 3)),
                  row(B_WIDTH), row(B_WIDTH), row(B_WIDTH), _full((1, D_MODEL)),
                  _full((D_MODEL, GATE_COLS)), vec, vec, vec,
                  _full((A_WIDTH, D_MODEL)), _full((B_WIDTH, D_MODEL)),
                  _full((D_MODEL, D_MODEL))],
        out_specs=row(D_MODEL),
        out_shape=jax.ShapeDtypeStruct((n, D_MODEL), F32),
        compiler_params=_params("arbitrary"),
        name="mix",
    )(x, oa, ua, y, bonus, gb, gn, wg, hw, lnw, lnb, wa, wb, wo)


def _mlp_kernel(x_ref, g_ref, wu_ref, wd_ref, gf_ref, o_ref):
    def sub_tile(rows):
        x1 = x_ref[rows, :]
        hb = _rmsnorm(x1, g_ref[...]).astype(BF16)
        yield
        act = jnp.square(jnp.maximum(jnp.dot(hb, wu_ref[...], preferred_element_type=F32), 0.0))
        yield
        x2 = x1 + _bdot(act, wd_ref[...])
        yield
        o_ref[rows, :] = _rmsnorm(x2, gf_ref[...])

    _round_robin(sub_tile(rows) for rows in _sub_tile_rows(x_ref.shape[0]))


def _mlp(x1, g, wu, wd, gf):
    n = x1.shape[0]
    tm = TOKEN_TILE * SUBTILES
    row = pl.BlockSpec((tm, D_MODEL), lambda i: (i, 0))
    return pl.pallas_call(
        _mlp_kernel,
        grid=(n // tm,),
        in_specs=[row, _full((1, D_MODEL)), _full((D_MODEL, D_FF)), _full((D_FF, D_MODEL)),
                  _full((1, D_MODEL))],
        out_specs=row,
        out_shape=jax.ShapeDtypeStruct((n, D_MODEL), F32),
        compiler_params=_params("arbitrary"),
        name="mlp",
    )(x1, g, wu, wd, gf)


class _SampleLayout:
    def __init__(self, batch, seq, heads, nj, ni, head_minor):
        assert batch == V7X_LANES
        self.b, self.t, self.h, self.nj, self.ni = batch, seq, heads, nj, ni
        self.head_minor = head_minor
        self.groups_per_coef = ni // ROWS_PER_GROUP
        self.nib = ROWS_PER_GROUP // V7X_SUBLANES

    def _to_chain(self, x, width):
        b, t, h = self.b, self.t, self.h
        if self.head_minor:
            return x.reshape(t, b, width, h).transpose(0, 3, 2, 1)
        return x.reshape(t, b, h, width).transpose(0, 2, 3, 1)

    def coef(self, x):
        return self._to_chain(x, self.nj)

    def rows_in(self, v):
        v = self._to_chain(v, self.ni)
        return v.reshape(self.t, self.h * self.groups_per_coef, self.nib, V7X_SUBLANES, V7X_LANES)

    def rows_out(self, y):
        b, t, h, ni = self.b, self.t, self.h, self.ni
        y = y.reshape(t, h, ni, b)
        y = y.transpose(0, 3, 2, 1) if self.head_minor else y.transpose(0, 3, 1, 2)
        return y.reshape(t * b, h * ni)

    def state_in(self, s, rows_last):
        b, h, nj = self.b, self.h, self.nj
        if not rows_last:
            s = s.transpose(0, 1, 3, 2)
        s = s.reshape(b, h, nj, self.groups_per_coef, self.nib, V7X_SUBLANES)
        s = s.transpose(1, 3, 4, 2, 5, 0)
        return s.reshape(h * self.groups_per_coef, self.nib, nj, V7X_SUBLANES, V7X_LANES)

    def state_out(self, s, rows_last):
        b, h, ni, nj = self.b, self.h, self.ni, self.nj
        s = s.reshape(h, self.groups_per_coef, self.nib, nj, V7X_SUBLANES, b)
        s = s.transpose(5, 0, 3, 1, 2, 4).reshape(b, h, nj, ni)
        return s if rows_last else s.transpose(0, 1, 3, 2)


def _prompt_state_to_chain(s, parts):
    b, h, ni, nj = s.shape
    nib = ni // parts // V7X_SUBLANES
    s = s.reshape(b, h, parts, nib, V7X_SUBLANES, nj).transpose(3, 5, 4, 2, 0, 1)
    return s.reshape(nib, nj, V7X_SUBLANES, V7X_LANES)


def _prompt_state_from_chain(s, batch, heads, parts):
    nib, nj = s.shape[0], s.shape[1]
    s = s.reshape(nib, nj, V7X_SUBLANES, parts, batch, heads).transpose(4, 5, 3, 0, 2, 1)
    return s.reshape(batch, heads, parts * nib * V7X_SUBLANES, nj)


def _same_head_matrix(width, heads):
    idx = np.arange(width) % heads
    return jnp.asarray(idx[:, None] == idx[None, :], BF16)


def _trunk(x, shift0, state_a, state_b, wts, *, time_major):
    batch, seq, _ = x.shape
    n = batch * seq
    if time_major:
        x2 = x.transpose(1, 0, 2).reshape(n, D_MODEL)
        time_stride = batch
    else:
        x2 = x.reshape(n, D_MODEL)
        time_stride = 1

    ua, ub = _norm_proj(x2, wts["norm_mix_g"], wts["w_in_a"], wts["w_in_b"])
    p0 = _prev_proj(shift0, wts["w_in_b"])
    prep_params = (wts["mu_shift"], wts["w_lora"], wts["w_decay0"], wts["a0"], wts["k_k"],
                   wts["k_a"], wts["r_k"], wts["same_head_b"])

    if time_major:
        kk, w, bco, k2, r, v, gb, bonus = _rwkv_prep(ub, p0, prep_params,
                                                     time_stride=time_stride)
        lay_b = _SampleLayout(batch, seq, B_HEADS, B_HEAD, B_HEAD, head_minor=True)
        kk_c = lay_b.coef(kk)
        kk_next = jnp.concatenate([kk_c[1:], jnp.zeros_like(kk_c[:1])], axis=0)
        y_c, sb_c = _recurrence(
            (kk_next, lay_b.coef(w), lay_b.coef(bco), lay_b.coef(k2), lay_b.coef(r)),
            lay_b.rows_in(v), lay_b.state_in(state_b, rows_last=False), kk0=kk_c[0],
            groups_per_coef=lay_b.groups_per_coef)
        y_b = lay_b.rows_out(y_c)
        new_wkv = lay_b.state_out(sb_c, rows_last=False)
    else:
        parts = V7X_LANES // (batch * B_HEADS)
        coef, v_c, gb, bonus = _rwkv_prep_chain(ub, p0, prep_params, batch=batch, seq=seq)
        gb = gb.reshape(n, B_WIDTH)
        bonus = bonus.reshape(n, B_WIDTH)
        nib = B_HEAD // parts // V7X_SUBLANES
        y_c, sb_c = _delta_rule_chain(
            coef, v_c.reshape(seq, nib, V7X_SUBLANES, V7X_LANES),
            _prompt_state_to_chain(state_b, parts))
        o_a, new_hgrn = _hgrn_chunked(ua, wts["lb_logits"], state_a, batch=batch, seq=seq)
        y_b = _rwkv_unpack(y_c.reshape(seq, nib * V7X_SUBLANES, V7X_LANES), batch=batch, seq=seq)
        new_wkv = _prompt_state_from_chain(sb_c, batch, B_HEADS, parts)

    if time_major:
        qs, fg, kc = _hgrn_prep(ua, wts["lb_logits"])
        lay_a = _SampleLayout(batch, seq, A_HEADS, A_HEAD, A_HEAD, head_minor=False)
        o_c, sa_c = _recurrence(
            (lay_a.coef(fg), lay_a.coef(kc), lay_a.coef(qs)),
            lay_a.rows_in(ua[:, 2 * A_WIDTH:3 * A_WIDTH]),
            lay_a.state_in(state_a, rows_last=True), groups_per_coef=lay_a.groups_per_coef)
        o_a = lay_a.rows_out(o_c)
        new_hgrn = lay_a.state_out(sa_c, rows_last=True)

    x1 = _mix(x2, o_a, ua, y_b, bonus, gb, wts["norm_mix_g"], wts["w_in_g"],
              wts["hgrn_norm_w"], wts["ln_x_w"],
              wts["ln_x_b"], wts["w_a_out"], wts["w_b_out"], wts["w_out"])
    y = _mlp(x1, wts["norm_mlp_g"], wts["w_up"], wts["w_down"], wts["norm_final_g"])
    if time_major:
        y = y.reshape(seq, batch, D_MODEL).transpose(1, 0, 2)
    else:
        y = y.reshape(batch, seq, D_MODEL)
    new_shift = _norm_rows(x[:, -1, :], wts["norm_mix_g"])
    return y, new_hgrn[None], new_wkv[None], new_shift[None]


def kernel(x_prompt, x_sample, state_hgrn, state_wkv, state_shift, norm_mix_g, w_in, mu_shift,
           w_decay0, w_decay_up, a0, w_aaa_up, w_gate_up, k_k, k_a, r_k, ln_x_w, ln_x_b,
           lb_logits, hgrn_norm_w, w_a_out, w_b_out, w_out, norm_mlp_g, w_up, w_down,
           norm_final_g):
    assert w_in.shape[0] == 1, "single-layer stack"
    def pcols(a):
        lead = a.shape[:-1]
        return a.reshape(*lead, B_HEADS, B_HEAD).swapaxes(-1, -2).reshape(*lead, B_WIDTH)

    prows = lambda a: a.reshape(B_HEADS, B_HEAD, -1).swapaxes(0, 1).reshape(B_WIDTH, -1)
    w_in0 = w_in[0]
    w_in_b = w_in0[:, A_COLS:A_COLS + B_COLS]
    w_in_b = jnp.concatenate(
        [pcols(w_in_b[:, s * B_WIDTH:(s + 1) * B_WIDTH]) for s in range(3)]
        + [w_in_b[:, 3 * B_WIDTH:]], axis=1)
    mu = mu_shift[0]
    mu = jnp.concatenate([pcols(mu[s * B_WIDTH:(s + 1) * B_WIDTH]) for s in range(3)]
                         + [mu[3 * B_WIDTH:]])
    w_lora = jnp.zeros((LORA_COLS, 3 * B_WIDTH), F32)
    w_lora = w_lora.at[:DECAY_LORA, :B_WIDTH].set(pcols(w_decay_up[0]))
    w_lora = w_lora.at[DECAY_LORA:DECAY_LORA + AAA_LORA, B_WIDTH:2 * B_WIDTH].set(
        pcols(w_aaa_up[0]))
    w_lora = w_lora.at[DECAY_LORA + AAA_LORA:, 2 * B_WIDTH:].set(pcols(w_gate_up[0]))
    row = lambda a: a.reshape(1, -1).astype(F32)
    prow = lambda a: row(pcols(a.reshape(-1)))
    wts = {
        "norm_mix_g": row(norm_mix_g[0]),
        "w_in_a": w_in0[:, :A_COLS].astype(BF16),
        "w_in_b": w_in_b.astype(BF16),
        "w_in_g": w_in0[:, A_COLS + B_COLS:].astype(BF16),
        "mu_shift": row(mu),
        "w_lora": w_lora.astype(BF16),
        "w_decay0": prow(w_decay0[0]),
        "a0": prow(a0[0]),
        "k_k": prow(k_k[0]),
        "k_a": prow(k_a[0]),
        "r_k": prow(r_k[0]),
        "ln_x_w": prow(ln_x_w[0]),
        "ln_x_b": prow(ln_x_b[0]),
        "lb_logits": lb_logits.astype(F32),
        "hgrn_norm_w": row(hgrn_norm_w[0]),
        "w_a_out": w_a_out[0].astype(BF16),
        "w_b_out": prows(w_b_out[0]).astype(BF16),
        "w_out": w_out[0].astype(BF16),
        "norm_mlp_g": row(norm_mlp_g[0]),
        "w_up": w_up[0].astype(BF16),
        "w_down": w_down[0].astype(BF16),
        "norm_final_g": row(norm_final_g),
        "same_head_b": _same_head_matrix(B_WIDTH, B_HEADS),
    }
    bp = x_prompt.shape[0]
    y_p, hgrn_p, wkv_p, shift_p = _trunk(
        x_prompt, jnp.zeros((bp, D_MODEL), F32),
        jnp.zeros((bp, A_HEADS, A_HEAD, A_HEAD), F32),
        jnp.zeros((bp, B_HEADS, B_HEAD, B_HEAD), F32), wts, time_major=False)
    y_s, hgrn_s, wkv_s, shift_s = _trunk(
        x_sample, state_shift[0], state_hgrn[0], state_wkv[0], wts, time_major=True)
    return (y_p, y_s, hgrn_p, wkv_p, shift_p, hgrn_s, wkv_s, shift_s)
```

```python
import functools

import jax
import jax.numpy as jnp
import numpy as np
from jax import lax
from jax.experimental import pallas as pl
from jax.experimental.pallas import tpu as pltpu

F32 = jnp.float32
BF16 = jnp.bfloat16

D_MODEL = 1024
A_WIDTH = 512
A_HEADS = 4
A_HEAD = 128
B_WIDTH = 512
B_HEADS = 8
B_HEAD = 64
DECAY_LORA = 64
AAA_LORA = 64
GATE_LORA = 128
LORA_COLS = DECAY_LORA + AAA_LORA + GATE_LORA
D_FF = 4 * D_MODEL
A_COLS = 4 * A_WIDTH
B_COLS = 3 * B_WIDTH + LORA_COLS
GATE_COLS = 2 * D_MODEL
NORM_EPS = 1e-6
HGRN_NORM_EPS = 1e-5
GN_EPS = 64e-5
DECAY_SCALE = 0.6065306597126334

V7X_LANES = 128
V7X_SUBLANES = 8
V7X_VMEM_LIMIT_BYTES = 56 * 1024 * 1024

TOKEN_TILE = 256
SUBTILES = 2
TIME_BLOCK = 64
ROWS_PER_GROUP = 64


def _params(*semantics):
    return pltpu.CompilerParams(dimension_semantics=semantics,
                                vmem_limit_bytes=V7X_VMEM_LIMIT_BYTES)


def _full(shape):
    return pl.BlockSpec(shape, lambda *_: (0,) * len(shape))


def _sub_tile_rows(rows):
    sub = rows // SUBTILES
    return [pl.ds(part * sub, sub) for part in range(SUBTILES)]


def _round_robin(streams):
    streams = list(streams)
    while streams:
        streams = [g for g in streams if next(g, StopIteration) is not StopIteration]


def _rmsnorm(x, g):
    return x * lax.rsqrt(jnp.mean(x * x, axis=-1, keepdims=True) + NORM_EPS) * g


def _bdot(a, w):
    return jnp.dot(a.astype(BF16), w, preferred_element_type=F32)


def _head_sum_rolled(x, heads):
    tiles = x.shape[1] // V7X_LANES
    t = x[:, 0:V7X_LANES]
    for c in range(1, tiles):
        t = t + x[:, c * V7X_LANES:(c + 1) * V7X_LANES]
    shift = heads
    while shift < V7X_LANES:
        t = t + pltpu.roll(t, shift, 1)
        shift *= 2
    return jnp.concatenate([t] * tiles, axis=1)


def _head_sum_lanes(x, head):
    out = []
    for h in range(x.shape[1] // head):
        seg = x[:, h * head:(h + 1) * head]
        out.append(jnp.broadcast_to(jnp.sum(seg, axis=-1, keepdims=True), seg.shape))
    return jnp.concatenate(out, axis=1)


def _head_sum(a, same_head):
    hi = a.astype(BF16)
    lo = (a - hi.astype(F32)).astype(BF16)
    return (jnp.dot(hi, same_head, preferred_element_type=F32)
            + jnp.dot(lo, same_head, preferred_element_type=F32))


def _norm_proj_kernel(x_ref, g_ref, wa_ref, wb_ref, ua_ref, ub_ref):
    def sub_tile(rows):
        hb = _rmsnorm(x_ref[rows, :], g_ref[...]).astype(BF16)
        yield
        ua_ref[rows, :] = jnp.dot(hb, wa_ref[...], preferred_element_type=F32)
        yield
        ub_ref[rows, :] = jnp.dot(hb, wb_ref[...], preferred_element_type=F32)

    _round_robin(sub_tile(rows) for rows in _sub_tile_rows(x_ref.shape[0]))


def _norm_proj(x, g, wa, wb):
    n = x.shape[0]
    tm = TOKEN_TILE * SUBTILES
    row = lambda c: pl.BlockSpec((tm, c), lambda i: (i, 0))
    return pl.pallas_call(
        _norm_proj_kernel,
        grid=(n // tm,),
        in_specs=[row(D_MODEL), _full((1, D_MODEL)), _full((D_MODEL, A_COLS)),
                  _full((D_MODEL, B_COLS))],
        out_specs=[row(A_COLS), row(B_COLS)],
        out_shape=[jax.ShapeDtypeStruct((n, A_COLS), F32),
                   jax.ShapeDtypeStruct((n, B_COLS), F32)],
        compiler_params=_params("arbitrary"),
        name="norm_proj",
    )(x, g, wa, wb)


def _norm_rows_kernel(x_ref, g_ref, o_ref):
    o_ref[...] = _rmsnorm(x_ref[...], g_ref[...])


def _norm_rows(x, g):
    return pl.pallas_call(
        _norm_rows_kernel,
        out_shape=jax.ShapeDtypeStruct(x.shape, F32),
        name="norm_rows",
    )(x, g)


def _prev_proj_kernel(h_ref, w_ref, o_ref):
    o_ref[...] = _bdot(h_ref[...], w_ref[...])


def _prev_proj(h_prev, wb):
    return pl.pallas_call(
        _prev_proj_kernel,
        out_shape=jax.ShapeDtypeStruct((h_prev.shape[0], B_COLS), F32),
        compiler_params=_params(),
        name="prev_proj",
    )(h_prev, wb)


def _rwkv_coefficients(ub, up, mu_ref, wlora_ref, wd0_ref, a0_ref, kk_w_ref, ka_ref, rk_ref,
                       same_head_ref):
    xm = ub + (up - ub) * mu_ref[...]
    r = xm[:, 0:B_WIDTH]
    k = xm[:, B_WIDTH:2 * B_WIDTH]
    v = xm[:, 2 * B_WIDTH:3 * B_WIDTH]
    lo = xm[:, 3 * B_WIDTH:]
    col = lax.broadcasted_iota(jnp.int32, lo.shape, 1)
    act = jnp.where(col < DECAY_LORA, jnp.tanh(lo),
                    jnp.where(col < DECAY_LORA + AAA_LORA, lo, jax.nn.sigmoid(lo)))
    yield
    up_proj = _bdot(act, wlora_ref[...])
    yield
    logw = -DECAY_SCALE * jax.nn.sigmoid(wd0_ref[...] + up_proj[:, 0:B_WIDTH])
    a = jax.nn.sigmoid(a0_ref[...] + up_proj[:, B_WIDTH:2 * B_WIDTH])
    same_head = same_head_ref[...]
    kk = k * kk_w_ref[...]
    norm2 = _head_sum(kk * kk, same_head)
    yield
    kk = kk / jnp.maximum(jnp.sqrt(norm2), 1e-12)
    k2 = k * (1.0 + (a - 1.0) * ka_ref[...])
    g = up_proj[:, 2 * B_WIDTH:]
    rk_sum = _head_sum(r * k2 * rk_ref[...], same_head)
    yield
    return (kk, logw, kk * a, k2, r, v), g, rk_sum * v


def _run_to_end(gen):
    while True:
        try:
            next(gen)
        except StopIteration as done:
            return done.value


def _rwkv_prep_kernel(ub_ref, p0_ref, mu_ref, wlora_ref, wd0_ref, a0_ref, kk_w_ref, ka_ref,
                      rk_ref, same_head_ref, kk_ref, w_ref, b_ref, k_ref, r_ref, v_ref, g_ref,
                      bonus_ref, *, time_stride):
    ub = ub_ref[...]
    up = jnp.concatenate([p0_ref[...], ub[:ub.shape[0] - time_stride, :]], axis=0)
    coefs, g, bonus = _run_to_end(_rwkv_coefficients(
        ub, up, mu_ref, wlora_ref, wd0_ref, a0_ref, kk_w_ref, ka_ref, rk_ref, same_head_ref))
    kk, logw, bco, k2, r, v = coefs
    for ref, val in zip((kk_ref, w_ref, b_ref, k_ref, r_ref, v_ref),
                        (kk, jnp.exp(logw), bco, k2, r, v)):
        ref[...] = val
    g_ref[...] = g
    bonus_ref[...] = bonus


def _rwkv_prep(ub, p0, prm, *, time_stride):
    n = ub.shape[0]
    return pl.pallas_call(
        functools.partial(_rwkv_prep_kernel, time_stride=time_stride),
        out_shape=[jax.ShapeDtypeStruct((n, B_WIDTH), F32)] * 8,
        compiler_params=_params(),
        name="rwkv_prep",
    )(ub, p0, *prm)


CHAIN_TILE = 128
N_COEF = 5
PREP_SEQS_PER_ITER = 2
PREP_KEY_SLICES = 4


def _block_prefix_ones(rows, block):
    t = np.arange(rows)
    return jnp.asarray((t[None, :] <= t[:, None]) & (t[None, :] // block == t[:, None] // block),
                       BF16)


def _rwkv_prep_chain_kernel(ub_ref, p0_ref, mu_ref, wlora_ref, wd0_ref, a0_ref, kk_w_ref, ka_ref,
                            rk_ref, same_head_ref, prefix_ref, coef_ref, v_ref, g_ref, bonus_ref,
                            carry_ref, xt_ref, *, batch, parts):
    tile = pl.program_id(0)
    key_slice = pl.program_id(1)
    half_rows = B_HEAD // parts

    @pl.when(key_slice == 0)
    def _():
        def per_seq(b):
            ub = ub_ref[b]
            first = jnp.where(tile == 0, p0_ref[pl.ds(b, 1), :], carry_ref[pl.ds(b, 1), :])
            row_id = lax.broadcasted_iota(jnp.int32, ub.shape, 0)
            up = jnp.where(row_id == 0, first, pltpu.roll(ub, 1, 0))
            carry_ref[pl.ds(b, 1), :] = ub[CHAIN_TILE - 1:CHAIN_TILE, :]
            coefs, g, bonus = yield from _rwkv_coefficients(
                ub, up, mu_ref, wlora_ref, wd0_ref, a0_ref, kk_w_ref, ka_ref, rk_ref,
                same_head_ref)
            g_ref[b] = g
            bonus_ref[b] = bonus
            kk, logw, bco, k2, r, v = coefs
            cum = sum(jnp.dot(prefix_ref[...], part, preferred_element_type=F32)
                      for part in _split3(logw))
            yield
            gamma = jnp.exp(cum)
            inv_gamma = jnp.exp(-cum)
            scaled = (kk * jnp.exp(cum - logw), gamma, bco * inv_gamma, k2 * inv_gamma,
                      r * gamma, v)
            for idx, val in enumerate(scaled):
                xt_ref[idx, b] = val.T
                yield

        def per_group(group, carry):
            _round_robin(per_seq(group * PREP_SEQS_PER_ITER + s)
                         for s in range(PREP_SEQS_PER_ITER))
            return carry

        lax.fori_loop(0, batch // PREP_SEQS_PER_ITER, per_group, 0)
        for i in range(half_rows):
            m = jnp.concatenate(
                [xt_ref[N_COEF, :, pl.ds((p * half_rows + i) * B_HEADS, B_HEADS), :]
                 .reshape(batch * B_HEADS, CHAIN_TILE) for p in range(parts)], axis=0)
            v_ref[:, i, :] = m.T

    keys_per_slice = B_HEAD // PREP_KEY_SLICES
    for which in range(N_COEF):
        for jl in range(keys_per_slice):
            rows = pl.ds(pl.multiple_of((key_slice * keys_per_slice + jl) * B_HEADS, B_HEADS),
                         B_HEADS)
            m = xt_ref[which, :, rows, :].reshape(batch * B_HEADS, CHAIN_TILE)
            coef_ref[which, jl] = jnp.concatenate([m] * parts, axis=0).T


def _rwkv_prep_chain(ub, p0, prm, *, batch, seq):
    parts = V7X_LANES // (batch * B_HEADS)
    n_tiles = seq // CHAIN_TILE
    tok = pl.BlockSpec((batch, CHAIN_TILE, B_WIDTH), lambda t, a: (0, t, 0))
    prm = tuple(prm) + (_block_prefix_ones(CHAIN_TILE, TIME_BLOCK),)
    return pl.pallas_call(
        functools.partial(_rwkv_prep_chain_kernel, batch=batch, parts=parts),
        grid=(n_tiles, PREP_KEY_SLICES),
        in_specs=[pl.BlockSpec((batch, CHAIN_TILE, B_COLS), lambda t, a: (0, t, 0)),
                  _full(p0.shape)] + [_full(p.shape) for p in prm],
        out_specs=[pl.BlockSpec((N_COEF, B_HEAD // PREP_KEY_SLICES, CHAIN_TILE, V7X_LANES),
                                lambda t, a: (0, a, t, 0)),
                   pl.BlockSpec((CHAIN_TILE, B_HEAD // parts, V7X_LANES), lambda t, a: (t, 0, 0)),
                   tok, tok],
        out_shape=[jax.ShapeDtypeStruct((N_COEF, B_HEAD, seq, V7X_LANES), F32),
                   jax.ShapeDtypeStruct((seq, B_HEAD // parts, V7X_LANES), F32),
                   jax.ShapeDtypeStruct((batch, seq, B_WIDTH), F32),
                   jax.ShapeDtypeStruct((batch, seq, B_WIDTH), F32)],
        scratch_shapes=[pltpu.VMEM((batch, B_COLS), F32),
                        pltpu.VMEM((N_COEF + 1, batch, B_WIDTH, CHAIN_TILE), F32)],
        compiler_params=_params("arbitrary", "arbitrary"),
        name="rwkv_prep_chain",
    )(ub.reshape(batch, seq, B_COLS), p0, *prm)


def _rwkv_unpack_kernel(y_ref, o_ref, yt_ref, *, batch, parts):
    half_rows = B_HEAD // parts
    for i in range(half_rows):
        nt = y_ref[:, i, :].T
        for p in range(parts):
            rows = slice(p * batch * B_HEADS, (p + 1) * batch * B_HEADS)
            yt_ref[:, pl.ds((p * half_rows + i) * B_HEADS, B_HEADS), :] = (
                nt[rows].reshape(batch, B_HEADS, CHAIN_TILE))
    for b in range(batch):
        o_ref[b] = yt_ref[b].T


def _rwkv_unpack(y, *, batch, seq):
    parts = V7X_LANES // (batch * B_HEADS)
    out = pl.pallas_call(
        functools.partial(_rwkv_unpack_kernel, batch=batch, parts=parts),
        grid=(seq // CHAIN_TILE,),
        in_specs=[pl.BlockSpec((CHAIN_TILE, B_HEAD // parts, V7X_LANES), lambda t: (t, 0, 0))],
        out_specs=pl.BlockSpec((batch, CHAIN_TILE, B_WIDTH), lambda t: (0, t, 0)),
        out_shape=jax.ShapeDtypeStruct((batch, seq, B_WIDTH), F32),
        scratch_shapes=[pltpu.VMEM((batch, B_WIDTH, CHAIN_TILE), F32)],
        compiler_params=_params("arbitrary"),
        name="rwkv_unpack",
    )(y)
    return out.reshape(batch * seq, B_WIDTH)


def _hgrn_prep_kernel(q_ref, f_ref, lbl_ref, qs_ref, fg_ref, kc_ref):
    logits = lbl_ref[...]
    e = jnp.exp(logits - jnp.max(logits, axis=0, keepdims=True))
    lb = e[0:1, :] / jnp.sum(e, axis=0, keepdims=True)
    fz = f_ref[...]
    fg_ref[...] = lb + (1.0 - lb) * jax.nn.sigmoid(fz)
    kc_ref[...] = (1.0 - lb) * jax.nn.sigmoid(-fz)
    qs_ref[...] = jax.nn.silu(q_ref[...])


def _hgrn_prep(ua, lb_logits):
    n = ua.shape[0]
    tm = TOKEN_TILE
    col = lambda j: pl.BlockSpec((tm, A_WIDTH), lambda i: (i, j))
    return pl.pallas_call(
        _hgrn_prep_kernel,
        grid=(n // tm,),
        in_specs=[col(0), col(1), _full(lb_logits.shape)],
        out_specs=[col(0)] * 3,
        out_shape=[jax.ShapeDtypeStruct((n, A_WIDTH), F32)] * 3,
        compiler_params=_params("arbitrary"),
        name="hgrn_prep",
    )(ua, ua, lb_logits)


HGRN_CHUNK = 64


def _hgrn_tables(chunk):
    levels = chunk.bit_length() - 1
    t = np.arange(chunk)
    u, tt = t[None, :], t[:, None]
    rows = [u <= tt, u > tt]
    masks = []
    for level in range(levels):
        m = 1 << level
        anchor = (t // (2 * m)) * (2 * m) + m - 1
        right = (t % (2 * m)) >= m
        rows.append(((u > anchor[:, None]) & (u <= tt) & right[:, None])
                    | ((u > tt) & (u <= anchor[:, None]) & ~right[:, None]))
        masks.append((tt // (2 * m) == u // (2 * m)) & right[:, None] & ~right[None, :])
    return (jnp.asarray(np.concatenate(rows, 0), BF16),
            jnp.asarray(np.stack(masks), F32), levels)


def _split3(x):
    hi = x.astype(BF16)
    r1 = x - hi.astype(F32)
    mid = r1.astype(BF16)
    lo = (r1 - mid.astype(F32)).astype(BF16)
    return hi, mid, lo


def _dot_nt(a, b):
    return lax.dot_general(a.astype(BF16), b.astype(BF16), (((1,), (1,)), ((), ())),
                           preferred_element_type=F32)


def _dot_tn(a, b):
    return lax.dot_general(a.astype(BF16), b.astype(BF16), (((0,), (0,)), ((), ())),
                           preferred_element_type=F32)


def _hgrn_lower_bound(lbl_ref):
    logits = lbl_ref[...]
    e = jnp.exp(logits - jnp.max(logits, axis=0, keepdims=True))
    return e[0:1, :] / jnp.sum(e, axis=0, keepdims=True)


def _hgrn_chunk_kernel(q_ref, f_ref, i_ref, lbl_ref, sums_ref, mask_ref, s0_ref, o_ref,
                       s_out_ref, st_ref, *, chunk, n_chunks, levels):
    n_seqs = st_ref.shape[0]

    @pl.when(pl.program_id(1) == 0)
    def _():
        for s in range(n_seqs):
            for h in range(A_HEADS):
                st_ref[s, h] = s0_ref[s, h].T

    lb = _hgrn_lower_bound(lbl_ref)
    sums = sums_ref[...]
    seg = lambda r: slice(r * chunk, (r + 1) * chunk)

    def one_head(s, h, rows):
        hs = slice(h * A_HEAD, (h + 1) * A_HEAD)
        fz = f_ref[s, rows, hs]
        lbh = lb[:, hs]
        logf = jnp.log(lbh + (1.0 - lbh) * jax.nn.sigmoid(fz))
        kh = (1.0 - lbh) * jax.nn.sigmoid(-fz)
        qh = jax.nn.silu(q_ref[s, rows, hs])
        vh = i_ref[s, rows, hs]
        yield
        decay = jnp.exp(sum(jnp.dot(sums, part, preferred_element_type=F32)
                            for part in _split3(logf)))
        yield
        att = None
        for level in range(levels):
            split = decay[seg(2 + level)]
            term = mask_ref[level] * _dot_nt(qh * split, kh * split)
            att = term if att is None else att + term
            if level % 2 == 1:
                yield
        st = st_ref[s, h]
        o_ref[s, rows, hs] = (_bdot(att, vh.astype(BF16))
                              + jnp.sum(qh * kh, axis=-1, keepdims=True) * vh
                              + _dot_nt(qh * decay[seg(0)], st))
        yield
        st_ref[s, h] = st * decay[chunk - 1:chunk] + _dot_tn(vh, kh * decay[seg(1)])
        yield

    for c in range(n_chunks):
        _round_robin(one_head(s, h, pl.ds(c * chunk, chunk))
                     for h in range(A_HEADS) for s in range(n_seqs))

    @pl.when(pl.program_id(1) == pl.num_programs(1) - 1)
    def _():
        for s in range(n_seqs):
            for h in range(A_HEADS):
                s_out_ref[s, h] = st_ref[s, h].T


HGRN_SEQS_PER_STEP = 2


def _hgrn_chunked(ua, lb_logits, s0, *, batch, seq):
    chunk = HGRN_CHUNK
    tile = TOKEN_TILE
    n_seqs = HGRN_SEQS_PER_STEP
    sums, masks, levels = _hgrn_tables(chunk)
    col = lambda j: pl.BlockSpec((n_seqs, tile, A_WIDTH), lambda b, t: (b, t, j))
    state_spec = pl.BlockSpec((n_seqs, A_HEADS, A_HEAD, A_HEAD), lambda b, t: (b, 0, 0, 0))
    out, state = pl.pallas_call(
        functools.partial(_hgrn_chunk_kernel, chunk=chunk, n_chunks=tile // chunk, levels=levels),
        grid=(batch // n_seqs, seq // tile),
        in_specs=[col(0), col(1), col(2), _full(lb_logits.shape), _full(sums.shape),
                  _full(masks.shape), state_spec],
        out_specs=[col(0), state_spec],
        out_shape=[jax.ShapeDtypeStruct((batch, seq, A_WIDTH), F32),
                   jax.ShapeDtypeStruct(s0.shape, F32)],
        scratch_shapes=[pltpu.VMEM((n_seqs, A_HEADS, A_HEAD, A_HEAD), F32)],
        compiler_params=_params("arbitrary", "arbitrary"),
        name="hgrn_chunked",
    )(*[ua.reshape(batch, seq, A_COLS)] * 3, lb_logits, sums, masks, s0)
    return out.reshape(batch * seq, A_WIDTH), state


def _bcast_row(ref, t, j):
    return jnp.broadcast_to(ref[t, pl.ds(j, 1), :], (V7X_SUBLANES, V7X_LANES))


def _delta_rule_kernel(kk0_ref, kkn_ref, w_ref, b_ref, k_ref, r_ref, v_ref, s0_ref,
                       y_ref, s_ref, sk_ref, *, nib, nj, tb):
    @pl.when(pl.program_id(1) == 0)
    def _():
        s_ref[...] = s0_ref[...]
        for ib in range(nib):
            acc = None
            for j in range(nj):
                kk0 = jnp.broadcast_to(kk0_ref[pl.ds(j, 1), :], (V7X_SUBLANES, V7X_LANES))
                term = s0_ref[ib, j] * kk0
                acc = term if acc is None else acc + term
            sk_ref[ib] = acc

    def step(t, carry):
        sk = [sk_ref[ib] for ib in range(nib)]
        vv = [v_ref[t, ib] for ib in range(nib)]
        yacc = [None] * nib
        skn = [None] * nib
        for j in range(nj):
            wj = _bcast_row(w_ref, t, j)
            bj = _bcast_row(b_ref, t, j)
            kj = _bcast_row(k_ref, t, j)
            rj = _bcast_row(r_ref, t, j)
            nj_kk = _bcast_row(kkn_ref, t, j)
            for ib in range(nib):
                s = s_ref[ib, j] * wj - sk[ib] * bj + vv[ib] * kj
                s_ref[ib, j] = s
                yt = s * rj
                st = s * nj_kk
                yacc[ib] = yt if yacc[ib] is None else yacc[ib] + yt
                skn[ib] = st if skn[ib] is None else skn[ib] + st
        for ib in range(nib):
            y_ref[t, ib] = yacc[ib]
            sk_ref[ib] = skn[ib]
        return carry

    lax.fori_loop(0, tb, step, 0)


def _delta_rule_chain_kernel(kk_ref, gamma_ref, b_ref, k_ref, r_ref, kk_head_ref, v_ref, s0_ref,
                             y_ref, s_ref, sk_ref, kkx_ref, *, nib, nj, tb):
    full = (V7X_SUBLANES, V7X_LANES)
    row = lambda ref, j, t: jnp.broadcast_to(ref[j, pl.ds(t, 1), :], full)

    @pl.when(pl.program_id(0) == 0)
    def _():
        s_ref[...] = s0_ref[...]
        for ib in range(nib):
            acc = None
            for j in range(nj):
                term = s0_ref[ib, j] * row(kk_ref, j, 0)
                acc = term if acc is None else acc + term
            sk_ref[ib] = acc

    kkx_ref[:, 0:tb, :] = kk_ref[...]
    kkx_ref[:, tb:tb + V7X_SUBLANES, :] = (
        kk_head_ref[...] * gamma_ref[:, tb - 1:tb, :])

    def step(t, carry):
        sk = [sk_ref[ib] for ib in range(nib)]
        vv = [v_ref[t, ib] for ib in range(nib)]
        yacc = [None] * nib
        skn = [None] * nib
        for j in range(nj):
            bj, kj, rj = (row(ref, j, t) for ref in (b_ref, k_ref, r_ref))
            kkn = row(kkx_ref, j, t + 1)
            for ib in range(nib):
                s = s_ref[ib, j] + (vv[ib] * kj - sk[ib] * bj)
                s_ref[ib, j] = s
                yt = s * rj
                st = s * kkn
                yacc[ib] = yt if yacc[ib] is None else yacc[ib] + yt
                skn[ib] = st if skn[ib] is None else skn[ib] + st
        for ib in range(nib):
            y_ref[t, ib] = yacc[ib]
            sk_ref[ib] = skn[ib]
        return carry

    lax.fori_loop(0, tb, step, 0)
    for j in range(nj):
        total = row(gamma_ref, j, tb - 1)
        for ib in range(nib):
            s_ref[ib, j] = s_ref[ib, j] * total


def _delta_rule_chain(coef, v, s0):
    _, nj, t_len, _ = coef.shape
    nib = v.shape[1]
    tb = TIME_BLOCK
    n_blocks = t_len // tb
    head_rows = V7X_SUBLANES
    coef_spec = lambda a: pl.BlockSpec((None, nj, tb, V7X_LANES), lambda t: (a, 0, t, 0))
    head_spec = pl.BlockSpec(
        (None, nj, head_rows, V7X_LANES),
        lambda t: (0, 0, jnp.minimum(t + 1, n_blocks - 1) * (tb // head_rows), 0))
    row_spec = pl.BlockSpec((tb, nib, V7X_SUBLANES, V7X_LANES), lambda t: (t, 0, 0, 0))
    return pl.pallas_call(
        functools.partial(_delta_rule_chain_kernel, nib=nib, nj=nj, tb=tb),
        grid=(n_blocks,),
        in_specs=[coef_spec(a) for a in range(N_COEF)]
        + [head_spec, row_spec, _full(s0.shape)],
        out_specs=[row_spec, _full(s0.shape)],
        out_shape=[jax.ShapeDtypeStruct(v.shape, F32), jax.ShapeDtypeStruct(s0.shape, F32)],
        scratch_shapes=[pltpu.VMEM((nib, V7X_SUBLANES, V7X_LANES), F32),
                        pltpu.VMEM((nj, tb + head_rows, V7X_LANES), F32)],
        compiler_params=_params("arbitrary"),
        name="delta_rule_chain",
    )(coef, coef, coef, coef, coef, coef, v, s0)


def _decay_rule_kernel(w_ref, k_ref, r_ref, v_ref, s0_ref, y_ref, s_ref, *, nib, nj, tb):
    @pl.when(pl.program_id(1) == 0)
    def _():
        s_ref[...] = s0_ref[...]

    def step(t, carry):
        vv = [v_ref[t, ib] for ib in range(nib)]
        yacc = [None] * nib
        for j in range(nj):
            wj = _bcast_row(w_ref, t, j)
            kj = _bcast_row(k_ref, t, j)
            rj = _bcast_row(r_ref, t, j)
            for ib in range(nib):
                s = s_ref[ib, j] * wj + vv[ib] * kj
                s_ref[ib, j] = s
                yt = s * rj
                yacc[ib] = yt if yacc[ib] is None else yacc[ib] + yt
        for ib in range(nib):
            y_ref[t, ib] = yacc[ib]
        return carry

    lax.fori_loop(0, tb, step, 0)


def _recurrence(coefs, v, s0, *, kk0=None, groups_per_coef):
    t_len, q, nib = v.shape[0], v.shape[1], v.shape[2]
    nj = coefs[0].shape[2]
    tb = min(TIME_BLOCK, t_len)
    coef_spec = pl.BlockSpec((tb, None, nj, V7X_LANES),
                             lambda g, t: (t, g // groups_per_coef, 0, 0))
    row_spec = pl.BlockSpec((tb, None, nib, V7X_SUBLANES, V7X_LANES),
                            lambda g, t: (t, g, 0, 0, 0))
    state_spec = pl.BlockSpec((None, nib, nj, V7X_SUBLANES, V7X_LANES),
                              lambda g, t: (g, 0, 0, 0, 0))
    delta = kk0 is not None
    if delta:
        body = functools.partial(_delta_rule_kernel, nib=nib, nj=nj, tb=tb)
        kk0_spec = pl.BlockSpec((None, nj, V7X_LANES), lambda g, t: (g // groups_per_coef, 0, 0))
        in_specs = [kk0_spec] + [coef_spec] * 5 + [row_spec, state_spec]
        args = (kk0,) + tuple(coefs) + (v, s0)
        scratch = [pltpu.VMEM((nib, V7X_SUBLANES, V7X_LANES), F32)]
    else:
        body = functools.partial(_decay_rule_kernel, nib=nib, nj=nj, tb=tb)
        in_specs = [coef_spec] * 3 + [row_spec, state_spec]
        args = tuple(coefs) + (v, s0)
        scratch = []
    return pl.pallas_call(
        body,
        grid=(q, t_len // tb),
        in_specs=in_specs,
        out_specs=[row_spec, state_spec],
        out_shape=[jax.ShapeDtypeStruct(v.shape, F32), jax.ShapeDtypeStruct(s0.shape, F32)],
        scratch_shapes=scratch,
        compiler_params=_params("arbitrary", "arbitrary"),
        name="delta_rule" if delta else "decay_rule",
    )(*args)


def _mix_kernel(x_ref, oa_ref, ga_ref, y_ref, bonus_ref, gb_ref, gn_ref, wg_ref, hw_ref, lnw_ref,
                lnb_ref, wa_ref, wb_ref, wo_ref, x1_ref):
    def sub_tile(rows):
        x = x_ref[rows, :]
        gate = jax.nn.sigmoid(_bdot(_rmsnorm(x, gn_ref[...]), wg_ref[...]))
        yield
        oa = oa_ref[rows, :]
        ms = _head_sum_lanes(oa * oa, A_HEAD) * (1.0 / A_HEAD)
        oa = oa * lax.rsqrt(ms + HGRN_NORM_EPS) * hw_ref[...] * jax.nn.silu(ga_ref[rows, :])
        ya = _bdot(oa, wa_ref[...])
        yield
        y = y_ref[rows, :]
        d = y - _head_sum_rolled(y, B_HEADS) * (1.0 / B_HEAD)
        var = _head_sum_rolled(d * d, B_HEADS) * (1.0 / B_HEAD)
        yn = d * lax.rsqrt(var + GN_EPS) * lnw_ref[...] + lnb_ref[...]
        ob = (yn + bonus_ref[rows, :]) * gb_ref[rows, :]
        yb = _bdot(ob, wb_ref[...])
        yield
        merged = gate[:, 0:D_MODEL] * ya + gate[:, D_MODEL:GATE_COLS] * yb
        x1_ref[rows, :] = x + _bdot(merged, wo_ref[...])

    _round_robin(sub_tile(rows) for rows in _sub_tile_rows(x_ref.shape[0]))


def _mix(x, oa, ua, y, bonus, gb, gn, wg, hw, lnw, lnb, wa, wb, wo):
    n = x.shape[0]
    tm = TOKEN_TILE * SUBTILES
    row = lambda c: pl.BlockSpec((tm, c), lambda i: (i, 0))
    vec = _full((1, A_WIDTH))
    return pl.pallas_call(
        _mix_kernel,
        grid=(n // tm,),
        in_specs=[row(D_MODEL), row(A_WIDTH), pl.BlockSpec((tm, A_WIDTH), lambda i: (i, 3)),
                  row(B_WIDTH), row(B_WIDTH), row(B_WIDTH), _full((1, D_MODEL)),
                  _full((D_MODEL, GATE_COLS)), vec, vec, vec,
                  _full((A_WIDTH, D_MODEL)), _full((B_WIDTH, D_MODEL)),
                  _full((D_MODEL, D_MODEL))],
        out_specs=row(D_MODEL),
        out_shape=jax.ShapeDtypeStruct((n, D_MODEL), F32),
        compiler_params=_params("arbitrary"),
        name="mix",
    )(x, oa, ua, y, bonus, gb, gn, wg, hw, lnw, lnb, wa, wb, wo)


def _mlp_kernel(x_ref, g_ref, wu_ref, wd_ref, gf_ref, o_ref):
    def sub_tile(rows):
        x1 = x_ref[rows, :]
        hb = _rmsnorm(x1, g_ref[...]).astype(BF16)
        yield
        act = jnp.square(jnp.maximum(jnp.dot(hb, wu_ref[...], preferred_element_type=F32), 0.0))
        yield
        x2 = x1 + _bdot(act, wd_ref[...])
        yield
        o_ref[rows, :] = _rmsnorm(x2, gf_ref[...])

    _round_robin(sub_tile(rows) for rows in _sub_tile_rows(x_ref.shape[0]))


def _mlp(x1, g, wu, wd, gf):
    n = x1.shape[0]
    tm = TOKEN_TILE * SUBTILES
    row = pl.BlockSpec((tm, D_MODEL), lambda i: (i, 0))
    return pl.pallas_call(
        _mlp_kernel,
        grid=(n // tm,),
        in_specs=[row, _full((1, D_MODEL)), _full((D_MODEL, D_FF)), _full((D_FF, D_MODEL)),
                  _full((1, D_MODEL))],
        out_specs=row,
        out_shape=jax.ShapeDtypeStruct((n, D_MODEL), F32),
        compiler_params=_params("arbitrary"),
        name="mlp",
    )(x1, g, wu, wd, gf)


class _SampleLayout:
    def __init__(self, batch, seq, heads, nj, ni, head_minor):
        assert batch == V7X_LANES
        self.b, self.t, self.h, self.nj, self.ni = batch, seq, heads, nj, ni
        self.head_minor = head_minor
        self.groups_per_coef = ni // ROWS_PER_GROUP
        self.nib = ROWS_PER_GROUP // V7X_SUBLANES

    def _to_chain(self, x, width):
        b, t, h = self.b, self.t, self.h
        if self.head_minor:
            return x.reshape(t, b, width, h).transpose(0, 3, 2, 1)
        return x.reshape(t, b, h, width).transpose(0, 2, 3, 1)

    def coef(self, x):
        return self._to_chain(x, self.nj)

    def rows_in(self, v):
        v = self._to_chain(v, self.ni)
        return v.reshape(self.t, self.h * self.groups_per_coef, self.nib, V7X_SUBLANES, V7X_LANES)

    def rows_out(self, y):
        b, t, h, ni = self.b, self.t, self.h, self.ni
        y = y.reshape(t, h, ni, b)
        y = y.transpose(0, 3, 2, 1) if self.head_minor else y.transpose(0, 3, 1, 2)
        return y.reshape(t * b, h * ni)

    def state_in(self, s, rows_last):
        b, h, nj = self.b, self.h, self.nj
        if not rows_last:
            s = s.transpose(0, 1, 3, 2)
        s = s.reshape(b, h, nj, self.groups_per_coef, self.nib, V7X_SUBLANES)
        s = s.transpose(1, 3, 4, 2, 5, 0)
        return s.reshape(h * self.groups_per_coef, self.nib, nj, V7X_SUBLANES, V7X_LANES)

    def state_out(self, s, rows_last):
        b, h, ni, nj = self.b, self.h, self.ni, self.nj
        s = s.reshape(h, self.groups_per_coef, self.nib, nj, V7X_SUBLANES, b)
        s = s.transpose(5, 0, 3, 1, 2, 4).reshape(b, h, nj, ni)
        return s if rows_last else s.transpose(0, 1, 3, 2)


def _prompt_state_to_chain(s, parts):
    b, h, ni, nj = s.shape
    nib = ni // parts // V7X_SUBLANES
    s = s.reshape(b, h, parts, nib, V7X_SUBLANES, nj).transpose(3, 5, 4, 2, 0, 1)
    return s.reshape(nib, nj, V7X_SUBLANES, V7X_LANES)


def _prompt_state_from_chain(s, batch, heads, parts):
    nib, nj = s.shape[0], s.shape[1]
    s = s.reshape(nib, nj, V7X_SUBLANES, parts, batch, heads).transpose(4, 5, 3, 0, 2, 1)
    return s.reshape(batch, heads, parts * nib * V7X_SUBLANES, nj)


def _same_head_matrix(width, heads):
    idx = np.arange(width) % heads
    return jnp.asarray(idx[:, None] == idx[None, :], BF16)


def _trunk(x, shift0, state_a, state_b, wts, *, time_major):
    batch, seq, _ = x.shape
    n = batch * seq
    if time_major:
        x2 = x.transpose(1, 0, 2).reshape(n, D_MODEL)
        time_stride = batch
    else:
        x2 = x.reshape(n, D_MODEL)
        time_stride = 1

    ua, ub = _norm_proj(x2, wts["norm_mix_g"], wts["w_in_a"], wts["w_in_b"])
    p0 = _prev_proj(shift0, wts["w_in_b"])
    prep_params = (wts["mu_shift"], wts["w_lora"], wts["w_decay0"], wts["a0"], wts["k_k"],
                   wts["k_a"], wts["r_k"], wts["same_head_b"])

    if time_major:
        kk, w, bco, k2, r, v, gb, bonus = _rwkv_prep(ub, p0, prep_params,
                                                     time_stride=time_stride)
        lay_b = _SampleLayout(batch, seq, B_HEADS, B_HEAD, B_HEAD, head_minor=True)
        kk_c = lay_b.coef(kk)
        kk_next = jnp.concatenate([kk_c[1:], jnp.zeros_like(kk_c[:1])], axis=0)
        y_c, sb_c = _recurrence(
            (kk_next, lay_b.coef(w), lay_b.coef(bco), lay_b.coef(k2), lay_b.coef(r)),
            lay_b.rows_in(v), lay_b.state_in(state_b, rows_last=False), kk0=kk_c[0],
            groups_per_coef=lay_b.groups_per_coef)
        y_b = lay_b.rows_out(y_c)
        new_wkv = lay_b.state_out(sb_c, rows_last=False)
    else:
        parts = V7X_LANES // (batch * B_HEADS)
        coef, v_c, gb, bonus = _rwkv_prep_chain(ub, p0, prep_params, batch=batch, seq=seq)
        gb = gb.reshape(n, B_WIDTH)
        bonus = bonus.reshape(n, B_WIDTH)
        nib = B_HEAD // parts // V7X_SUBLANES
        y_c, sb_c = _delta_rule_chain(
            coef, v_c.reshape(seq, nib, V7X_SUBLANES, V7X_LANES),
            _prompt_state_to_chain(state_b, parts))
        o_a, new_hgrn = _hgrn_chunked(ua, wts["lb_logits"], state_a, batch=batch, seq=seq)
        y_b = _rwkv_unpack(y_c.reshape(seq, nib * V7X_SUBLANES, V7X_LANES), batch=batch, seq=seq)
        new_wkv = _prompt_state_from_chain(sb_c, batch, B_HEADS, parts)

    if time_major:
        qs, fg, kc = _hgrn_prep(ua, wts["lb_logits"])
        lay_a = _SampleLayout(batch, seq, A_HEADS, A_HEAD, A_HEAD, head_minor=False)
        o_c, sa_c = _recurrence(
            (lay_a.coef(fg), lay_a.coef(kc), lay_a.coef(qs)),
            lay_a.rows_in(ua[:, 2 * A_WIDTH:3 * A_WIDTH]),
            lay_a.state_in(state_a, rows_last=True), groups_per_coef=lay_a.groups_per_coef)
        o_a = lay_a.rows_out(o_c)
        new_hgrn = lay_a.state_out(sa_c, rows_last=True)

    x1 = _mix(x2, o_a, ua, y_b, bonus, gb, wts["norm_mix_g"], wts["w_in_g"],
              wts["hgrn_norm_w"], wts["ln_x_w"],
              wts["ln_x_b"], wts["w_a_out"], wts["w_b_out"], wts["w_out"])
    y = _mlp(x1, wts["norm_mlp_g"], wts["w_up"], wts["w_down"], wts["norm_final_g"])
    if time_major:
        y = y.reshape(seq, batch, D_MODEL).transpose(1, 0, 2)
    else:
        y = y.reshape(batch, seq, D_MODEL)
    new_shift = _norm_rows(x[:, -1, :], wts["norm_mix_g"])
    return y, new_hgrn[None], new_wkv[None], new_shift[None]


def kernel(x_prompt, x_sample, state_hgrn, state_wkv, state_shift, norm_mix_g, w_in, mu_shift,
           w_decay0, w_decay_up, a0, w_aaa_up, w_gate_up, k_k, k_a, r_k, ln_x_w, ln_x_b,
           lb_logits, hgrn_norm_w, w_a_out, w_b_out, w_out, norm_mlp_g, w_up, w_down,
           norm_final_g):
    assert w_in.shape[0] == 1, "single-layer stack"
    def pcols(a):
        lead = a.shape[:-1]
        return a.reshape(*lead, B_HEADS, B_HEAD).swapaxes(-1, -2).reshape(*lead, B_WIDTH)

    prows = lambda a: a.reshape(B_HEADS, B_HEAD, -1).swapaxes(0, 1).reshape(B_WIDTH, -1)
    w_in0 = w_in[0]
    w_in_b = w_in0[:, A_COLS:A_COLS + B_COLS]
    w_in_b = jnp.concatenate(
        [pcols(w_in_b[:, s * B_WIDTH:(s + 1) * B_WIDTH]) for s in range(3)]
        + [w_in_b[:, 3 * B_WIDTH:]], axis=1)
    mu = mu_shift[0]
    mu = jnp.concatenate([pcols(mu[s * B_WIDTH:(s + 1) * B_WIDTH]) for s in range(3)]
                         + [mu[3 * B_WIDTH:]])
    w_lora = jnp.zeros((LORA_COLS, 3 * B_WIDTH), F32)
    w_lora = w_lora.at[:DECAY_LORA, :B_WIDTH].set(pcols(w_decay_up[0]))
    w_lora = w_lora.at[DECAY_LORA:DECAY_LORA + AAA_LORA, B_WIDTH:2 * B_WIDTH].set(
        pcols(w_aaa_up[0]))
    w_lora = w_lora.at[DECAY_LORA + AAA_LORA:, 2 * B_WIDTH:].set(pcols(w_gate_up[0]))
    row = lambda a: a.reshape(1, -1).astype(F32)
    prow = lambda a: row(pcols(a.reshape(-1)))
    wts = {
        "norm_mix_g": row(norm_mix_g[0]),
        "w_in_a": w_in0[:, :A_COLS].astype(BF16),
        "w_in_b": w_in_b.astype(BF16),
        "w_in_g": w_in0[:, A_COLS + B_COLS:].astype(BF16),
        "mu_shift": row(mu),
        "w_lora": w_lora.astype(BF16),
        "w_decay0": prow(w_decay0[0]),
        "a0": prow(a0[0]),
        "k_k": prow(k_k[0]),
        "k_a": prow(k_a[0]),
        "r_k": prow(r_k[0]),
        "ln_x_w": prow(ln_x_w[0]),
        "ln_x_b": prow(ln_x_b[0]),
        "lb_logits": lb_logits.astype(F32),
        "hgrn_norm_w": row(hgrn_norm_w[0]),
        "w_a_out": w_a_out[0].astype(BF16),
        "w_b_out": prows(w_b_out[0]).astype(BF16),
        "w_out": w_out[0].astype(BF16),
        "norm_mlp_g": row(norm_mlp_g[0]),
        "w_up": w_up[0].astype(BF16),
        "w_down": w_down[0].astype(BF16),
        "norm_final_g": row(norm_final_g),
        "same_head_b": _same_head_matrix(B_WIDTH, B_HEADS),
    }
    bp = x_prompt.shape[0]
    y_p, hgrn_p, wkv_p, shift_p = _trunk(
        x_prompt, jnp.zeros((bp, D_MODEL), F32),
        jnp.zeros((bp, A_HEADS, A_HEAD, A_HEAD), F32),
        jnp.zeros((bp, B_HEADS, B_HEAD, B_HEAD), F32), wts, time_major=False)
    y_s, hgrn_s, wkv_s, shift_s = _trunk(
        x_sample, state_shift[0], state_hgrn[0], state_wkv[0], wts, time_major=True)
    return (y_p, y_s, hgrn_p, wkv_p, shift_p, hgrn_s, wkv_s, shift_s)
```

```python
import functools

import jax
import jax.numpy as jnp
import numpy as np
from jax import lax
from jax.experimental import pallas as pl
from jax.experimental.pallas import tpu as pltpu

F32 = jnp.float32
BF16 = jnp.bfloat16

D_MODEL = 1024
A_WIDTH = 512
A_HEADS = 4
A_HEAD = 128
B_WIDTH = 512
B_HEADS = 8
B_HEAD = 64
DECAY_LORA = 64
AAA_LORA = 64
GATE_LORA = 128
LORA_COLS = DECAY_LORA + AAA_LORA + GATE_LORA
D_FF = 4 * D_MODEL
A_COLS = 4 * A_WIDTH
B_COLS = 3 * B_WIDTH + LORA_COLS
GATE_COLS = 2 * D_MODEL
NORM_EPS = 1e-6
HGRN_NORM_EPS = 1e-5
GN_EPS = 64e-5
DECAY_SCALE = 0.6065306597126334

V7X_LANES = 128
V7X_SUBLANES = 8
V7X_VMEM_LIMIT_BYTES = 56 * 1024 * 1024

TOKEN_TILE = 256
SUBTILES = 2
TIME_BLOCK = 64
ROWS_PER_GROUP = 64


def _params(*semantics):
    return pltpu.CompilerParams(dimension_semantics=semantics,
                                vmem_limit_bytes=V7X_VMEM_LIMIT_BYTES)


def _full(shape):
    return pl.BlockSpec(shape, lambda *_: (0,) * len(shape))


def _sub_tile_rows(rows):
    sub = rows // SUBTILES
    return [pl.ds(part * sub, sub) for part in range(SUBTILES)]


def _round_robin(streams):
    streams = list(streams)
    while streams:
        streams = [g for g in streams if next(g, StopIteration) is not StopIteration]


def _rmsnorm(x, g):
    return x * lax.rsqrt(jnp.mean(x * x, axis=-1, keepdims=True) + NORM_EPS) * g


def _bdot(a, w):
    return jnp.dot(a.astype(BF16), w, preferred_element_type=F32)


def _head_sum_rolled(x, heads):
    tiles = x.shape[1] // V7X_LANES
    t = x[:, 0:V7X_LANES]
    for c in range(1, tiles):
        t = t + x[:, c * V7X_LANES:(c + 1) * V7X_LANES]
    shift = heads
    while shift < V7X_LANES:
        t = t + pltpu.roll(t, shift, 1)
        shift *= 2
    return jnp.concatenate([t] * tiles, axis=1)


def _head_sum_lanes(x, head):
    out = []
    for h in range(x.shape[1] // head):
        seg = x[:, h * head:(h + 1) * head]
        out.append(jnp.broadcast_to(jnp.sum(seg, axis=-1, keepdims=True), seg.shape))
    return jnp.concatenate(out, axis=1)


def _head_sum(a, same_head):
    hi = a.astype(BF16)
    lo = (a - hi.astype(F32)).astype(BF16)
    return (jnp.dot(hi, same_head, preferred_element_type=F32)
            + jnp.dot(lo, same_head, preferred_element_type=F32))


def _norm_proj_kernel(x_ref, g_ref, wa_ref, wb_ref, ua_ref, ub_ref):
    def sub_tile(rows):
        hb = _rmsnorm(x_ref[rows, :], g_ref[...]).astype(BF16)
        yield
        ua_ref[rows, :] = jnp.dot(hb, wa_ref[...], preferred_element_type=F32)
        yield
        ub_ref[rows, :] = jnp.dot(hb, wb_ref[...], preferred_element_type=F32)

    _round_robin(sub_tile(rows) for rows in _sub_tile_rows(x_ref.shape[0]))


def _norm_proj(x, g, wa, wb):
    n = x.shape[0]
    tm = TOKEN_TILE * SUBTILES
    row = lambda c: pl.BlockSpec((tm, c), lambda i: (i, 0))
    return pl.pallas_call(
        _norm_proj_kernel,
        grid=(n // tm,),
        in_specs=[row(D_MODEL), _full((1, D_MODEL)), _full((D_MODEL, A_COLS)),
                  _full((D_MODEL, B_COLS))],
        out_specs=[row(A_COLS), row(B_COLS)],
        out_shape=[jax.ShapeDtypeStruct((n, A_COLS), F32),
                   jax.ShapeDtypeStruct((n, B_COLS), F32)],
        compiler_params=_params("arbitrary"),
        name="norm_proj",
    )(x, g, wa, wb)


def _norm_rows_kernel(x_ref, g_ref, o_ref):
    o_ref[...] = _rmsnorm(x_ref[...], g_ref[...])


def _norm_rows(x, g):
    return pl.pallas_call(
        _norm_rows_kernel,
        out_shape=jax.ShapeDtypeStruct(x.shape, F32),
        name="norm_rows",
    )(x, g)


def _prev_proj_kernel(h_ref, w_ref, o_ref):
    o_ref[...] = _bdot(h_ref[...], w_ref[...])


def _prev_proj(h_prev, wb):
    return pl.pallas_call(
        _prev_proj_kernel,
        out_shape=jax.ShapeDtypeStruct((h_prev.shape[0], B_COLS), F32),
        compiler_params=_params(),
        name="prev_proj",
    )(h_prev, wb)


def _rwkv_coefficients(ub, up, mu_ref, wlora_ref, wd0_ref, a0_ref, kk_w_ref, ka_ref, rk_ref,
                       same_head_ref):
    xm = ub + (up - ub) * mu_ref[...]
    r = xm[:, 0:B_WIDTH]
    k = xm[:, B_WIDTH:2 * B_WIDTH]
    v = xm[:, 2 * B_WIDTH:3 * B_WIDTH]
    lo = xm[:, 3 * B_WIDTH:]
    col = lax.broadcasted_iota(jnp.int32, lo.shape, 1)
    act = jnp.where(col < DECAY_LORA, jnp.tanh(lo),
                    jnp.where(col < DECAY_LORA + AAA_LORA, lo, jax.nn.sigmoid(lo)))
    yield
    up_proj = _bdot(act, wlora_ref[...])
    yield
    logw = -DECAY_SCALE * jax.nn.sigmoid(wd0_ref[...] + up_proj[:, 0:B_WIDTH])
    a = jax.nn.sigmoid(a0_ref[...] + up_proj[:, B_WIDTH:2 * B_WIDTH])
    same_head = same_head_ref[...]
    kk = k * kk_w_ref[...]
    norm2 = _head_sum(kk * kk, same_head)
    yield
    kk = kk / jnp.maximum(jnp.sqrt(norm2), 1e-12)
    k2 = k * (1.0 + (a - 1.0) * ka_ref[...])
    g = up_proj[:, 2 * B_WIDTH:]
    rk_sum = _head_sum(r * k2 * rk_ref[...], same_head)
    yield
    return (kk, logw, kk * a, k2, r, v), g, rk_sum * v


def _run_to_end(gen):
    while True:
        try:
            next(gen)
        except StopIteration as done:
            return done.value


def _rwkv_prep_kernel(ub_ref, p0_ref, mu_ref, wlora_ref, wd0_ref, a0_ref, kk_w_ref, ka_ref,
                      rk_ref, same_head_ref, kk_ref, w_ref, b_ref, k_ref, r_ref, v_ref, g_ref,
                      bonus_ref, *, time_stride):
    ub = ub_ref[...]
    up = jnp.concatenate([p0_ref[...], ub[:ub.shape[0] - time_stride, :]], axis=0)
    coefs, g, bonus = _run_to_end(_rwkv_coefficients(
        ub, up, mu_ref, wlora_ref, wd0_ref, a0_ref, kk_w_ref, ka_ref, rk_ref, same_head_ref))
    kk, logw, bco, k2, r, v = coefs
    for ref, val in zip((kk_ref, w_ref, b_ref, k_ref, r_ref, v_ref),
                        (kk, jnp.exp(logw), bco, k2, r, v)):
        ref[...] = val
    g_ref[...] = g
    bonus_ref[...] = bonus


def _rwkv_prep(ub, p0, prm, *, time_stride):
    n = ub.shape[0]
    return pl.pallas_call(
        functools.partial(_rwkv_prep_kernel, time_stride=time_stride),
        out_shape=[jax.ShapeDtypeStruct((n, B_WIDTH), F32)] * 8,
        compiler_params=_params(),
        name="rwkv_prep",
    )(ub, p0, *prm)


CHAIN_TILE = 128
N_COEF = 5
PREP_SEQS_PER_ITER = 2
PREP_KEY_SLICES = 4


def _block_prefix_ones(rows, block):
    t = np.arange(rows)
    return jnp.asarray((t[None, :] <= t[:, None]) & (t[None, :] // block == t[:, None] // block),
                       BF16)


def _rwkv_prep_chain_kernel(ub_ref, p0_ref, mu_ref, wlora_ref, wd0_ref, a0_ref, kk_w_ref, ka_ref,
                            rk_ref, same_head_ref, prefix_ref, coef_ref, v_ref, g_ref, bonus_ref,
                            carry_ref, xt_ref, *, batch, parts):
    tile = pl.program_id(0)
    key_slice = pl.program_id(1)
    half_rows = B_HEAD // parts

    @pl.when(key_slice == 0)
    def _():
        def per_seq(b):
            ub = ub_ref[b]
            first = jnp.where(tile == 0, p0_ref[pl.ds(b, 1), :], carry_ref[pl.ds(b, 1), :])
            row_id = lax.broadcasted_iota(jnp.int32, ub.shape, 0)
            up = jnp.where(row_id == 0, first, pltpu.roll(ub, 1, 0))
            carry_ref[pl.ds(b, 1), :] = ub[CHAIN_TILE - 1:CHAIN_TILE, :]
            coefs, g, bonus = yield from _rwkv_coefficients(
                ub, up, mu_ref, wlora_ref, wd0_ref, a0_ref, kk_w_ref, ka_ref, rk_ref,
                same_head_ref)
            g_ref[b] = g
            bonus_ref[b] = bonus
            kk, logw, bco, k2, r, v = coefs
            cum = sum(jnp.dot(prefix_ref[...], part, preferred_element_type=F32)
                      for part in _split3(logw))
            yield
            gamma = jnp.exp(cum)
            inv_gamma = jnp.exp(-cum)
            scaled = (kk * jnp.exp(cum - logw), gamma, bco * inv_gamma, k2 * inv_gamma,
                      r * gamma, v)
            for idx, val in enumerate(scaled):
                xt_ref[idx, b] = val.T
                yield

        def per_group(group, carry):
            _round_robin(per_seq(group * PREP_SEQS_PER_ITER + s)
                         for s in range(PREP_SEQS_PER_ITER))
            return carry

        lax.fori_loop(0, batch // PREP_SEQS_PER_ITER, per_group, 0)
        for i in range(half_rows):
            m = jnp.concatenate(
                [xt_ref[N_COEF, :, pl.ds((p * half_rows + i) * B_HEADS, B_HEADS), :]
                 .reshape(batch * B_HEADS, CHAIN_TILE) for p in range(parts)], axis=0)
            v_ref[:, i, :] = m.T

    keys_per_slice = B_HEAD // PREP_KEY_SLICES
    for which in range(N_COEF):
        for jl in range(keys_per_slice):
            rows = pl.ds(pl.multiple_of((key_slice * keys_per_slice + jl) * B_HEADS, B_HEADS),
                         B_HEADS)
            m = xt_ref[which, :, rows, :].reshape(batch * B_HEADS, CHAIN_TILE)
            coef_ref[which, jl] = jnp.concatenate([m] * parts, axis=0).T


def _rwkv_prep_chain(ub, p0, prm, *, batch, seq):
    parts = V7X_LANES // (batch * B_HEADS)
    n_tiles = seq // CHAIN_TILE
    tok = pl.BlockSpec((batch, CHAIN_TILE, B_WIDTH), lambda t, a: (0, t, 0))
    prm = tuple(prm) + (_block_prefix_ones(CHAIN_TILE, TIME_BLOCK),)
    return pl.pallas_call(
        functools.partial(_rwkv_prep_chain_kernel, batch=batch, parts=parts),
        grid=(n_tiles, PREP_KEY_SLICES),
        in_specs=[pl.BlockSpec((batch, CHAIN_TILE, B_COLS), lambda t, a: (0, t, 0)),
                  _full(p0.shape)] + [_full(p.shape) for p in prm],
        out_specs=[pl.BlockSpec((N_COEF, B_HEAD // PREP_KEY_SLICES, CHAIN_TILE, V7X_LANES),
                                lambda t, a: (0, a, t, 0)),
                   pl.BlockSpec((CHAIN_TILE, B_HEAD // parts, V7X_LANES), lambda t, a: (t, 0, 0)),
                   tok, tok],
        out_shape=[jax.ShapeDtypeStruct((N_COEF, B_HEAD, seq, V7X_LANES), F32),
                   jax.ShapeDtypeStruct((seq, B_HEAD // parts, V7X_LANES), F32),
                   jax.ShapeDtypeStruct((batch, seq, B_WIDTH), F32),
                   jax.ShapeDtypeStruct((batch, seq, B_WIDTH), F32)],
        scratch_shapes=[pltpu.VMEM((batch, B_COLS), F32),
                        pltpu.VMEM((N_COEF + 1, batch, B_WIDTH, CHAIN_TILE), F32)],
        compiler_params=_params("arbitrary", "arbitrary"),
        name="rwkv_prep_chain",
    )(ub.reshape(batch, seq, B_COLS), p0, *prm)


def _rwkv_unpack_kernel(y_ref, o_ref, yt_ref, *, batch, parts):
    half_rows = B_HEAD // parts
    for i in range(half_rows):
        nt = y_ref[:, i, :].T
        for p in range(parts):
            rows = slice(p * batch * B_HEADS, (p + 1) * batch * B_HEADS)
            yt_ref[:, pl.ds((p * half_rows + i) * B_HEADS, B_HEADS), :] = (
                nt[rows].reshape(batch, B_HEADS, CHAIN_TILE))
    for b in range(batch):
        o_ref[b] = yt_ref[b].T


def _rwkv_unpack(y, *, batch, seq):
    parts = V7X_LANES // (batch * B_HEADS)
    out = pl.pallas_call(
        functools.partial(_rwkv_unpack_kernel, batch=batch, parts=parts),
        grid=(seq // CHAIN_TILE,),
        in_specs=[pl.BlockSpec((CHAIN_TILE, B_HEAD // parts, V7X_LANES), lambda t: (t, 0, 0))],
        out_specs=pl.BlockSpec((batch, CHAIN_TILE, B_WIDTH), lambda t: (0, t, 0)),
        out_shape=jax.ShapeDtypeStruct((batch, seq, B_WIDTH), F32),
        scratch_shapes=[pltpu.VMEM((batch, B_WIDTH, CHAIN_TILE), F32)],
        compiler_params=_params("arbitrary"),
        name="rwkv_unpack",
    )(y)
    return out.reshape(batch * seq, B_WIDTH)


def _hgrn_prep_kernel(q_ref, f_ref, lbl_ref, qs_ref, fg_ref, kc_ref):
    logits = lbl_ref[...]
    e = jnp.exp(logits - jnp.max(logits, axis=0, keepdims=True))
    lb = e[0:1, :] / jnp.sum(e, axis=0, keepdims=True)
    fz = f_ref[...]
    fg_ref[...] = lb + (1.0 - lb) * jax.nn.sigmoid(fz)
    kc_ref[...] = (1.0 - lb) * jax.nn.sigmoid(-fz)
    qs_ref[...] = jax.nn.silu(q_ref[...])


def _hgrn_prep(ua, lb_logits):
    n = ua.shape[0]
    tm = TOKEN_TILE
    col = lambda j: pl.BlockSpec((tm, A_WIDTH), lambda i: (i, j))
    return pl.pallas_call(
        _hgrn_prep_kernel,
        grid=(n // tm,),
        in_specs=[col(0), col(1), _full(lb_logits.shape)],
        out_specs=[col(0)] * 3,
        out_shape=[jax.ShapeDtypeStruct((n, A_WIDTH), F32)] * 3,
        compiler_params=_params("arbitrary"),
        name="hgrn_prep",
    )(ua, ua, lb_logits)


HGRN_CHUNK = 64


def _hgrn_tables(chunk):
    levels = chunk.bit_length() - 1
    t = np.arange(chunk)
    u, tt = t[None, :], t[:, None]
    rows = [u <= tt]
    masks = []
    for level in range(levels):
        m = 1 << level
        anchor = (t // (2 * m)) * (2 * m) + m - 1
        right = (t % (2 * m)) >= m
        if m < V7X_SUBLANES:
            rows.append(((u > anchor[:, None]) & (u <= tt) & right[:, None])
                        | ((u > tt) & (u <= anchor[:, None]) & ~right[:, None]))
        masks.append((tt // (2 * m) == u // (2 * m)) & right[:, None] & ~right[None, :])
    return (jnp.asarray(np.concatenate(rows, 0), BF16),
            jnp.asarray(np.stack(masks), F32), levels)


def _hgrn_anchor_sums(cum, level):
    m = 1 << level
    groups = []
    for g in range(cum.shape[0] // V7X_SUBLANES):
        start = g * V7X_SUBLANES
        anchor = (start // (2 * m)) * (2 * m) + m - 1
        block = cum[start:start + V7X_SUBLANES]
        ref_row = cum[anchor:anchor + 1]
        groups.append(block - ref_row if start % (2 * m) >= m else ref_row - block)
    return jnp.concatenate(groups, axis=0)


def _split3(x):
    hi = x.astype(BF16)
    r1 = x - hi.astype(F32)
    mid = r1.astype(BF16)
    lo = (r1 - mid.astype(F32)).astype(BF16)
    return hi, mid, lo


def _dot_nt(a, b):
    return lax.dot_general(a.astype(BF16), b.astype(BF16), (((1,), (1,)), ((), ())),
                           preferred_element_type=F32)


def _dot_tn(a, b):
    return lax.dot_general(a.astype(BF16), b.astype(BF16), (((0,), (0,)), ((), ())),
                           preferred_element_type=F32)


def _hgrn_lower_bound(lbl_ref):
    logits = lbl_ref[...]
    e = jnp.exp(logits - jnp.max(logits, axis=0, keepdims=True))
    return e[0:1, :] / jnp.sum(e, axis=0, keepdims=True)


def _hgrn_chunk_kernel(q_ref, f_ref, i_ref, lbl_ref, sums_ref, mask_ref, s0_ref, o_ref,
                       s_out_ref, st_ref, *, chunk, n_chunks, levels):
    n_seqs = st_ref.shape[0]

    @pl.when(pl.program_id(1) == 0)
    def _():
        for s in range(n_seqs):
            for h in range(A_HEADS):
                st_ref[s, h] = s0_ref[s, h].T

    lb = _hgrn_lower_bound(lbl_ref)
    sums = sums_ref[...]
    seg = lambda r: slice(r * chunk, (r + 1) * chunk)

    def one_head(s, h, rows):
        hs = slice(h * A_HEAD, (h + 1) * A_HEAD)
        fz = f_ref[s, rows, hs]
        lbh = lb[:, hs]
        logf = jnp.log(lbh + (1.0 - lbh) * jax.nn.sigmoid(fz))
        kh = (1.0 - lbh) * jax.nn.sigmoid(-fz)
        qh = jax.nn.silu(q_ref[s, rows, hs])
        vh = i_ref[s, rows, hs]
        yield
        sums_out = sum(jnp.dot(sums, part, preferred_element_type=F32)
                       for part in _split3(logf))
        yield
        cum = sums_out[seg(0)]
        from_start = jnp.exp(cum)
        to_end = jnp.exp(cum[chunk - 1:chunk] - cum)
        att = None
        for level in range(levels):
            if (1 << level) < V7X_SUBLANES:
                split = jnp.exp(sums_out[seg(1 + level)])
            else:
                split = jnp.exp(_hgrn_anchor_sums(cum, level))
            term = mask_ref[level] * _dot_nt(qh * split, kh * split)
            att = term if att is None else att + term
            if level % 2 == 1:
                yield
        st = st_ref[s, h]
        o_ref[s, rows, hs] = (_bdot(att, vh.astype(BF16))
                              + jnp.sum(qh * kh, axis=-1, keepdims=True) * vh
                              + _dot_nt(qh * from_start, st))
        yield
        st_ref[s, h] = st * from_start[chunk - 1:chunk] + _dot_tn(vh, kh * to_end)
        yield

    for c in range(n_chunks):
        _round_robin(one_head(s, h, pl.ds(c * chunk, chunk))
                     for h in range(A_HEADS) for s in range(n_seqs))

    @pl.when(pl.program_id(1) == pl.num_programs(1) - 1)
    def _():
        for s in range(n_seqs):
            for h in range(A_HEADS):
                s_out_ref[s, h] = st_ref[s, h].T


HGRN_SEQS_PER_STEP = 2


def _hgrn_chunked(ua, lb_logits, s0, *, batch, seq):
    chunk = HGRN_CHUNK
    tile = TOKEN_TILE
    n_seqs = HGRN_SEQS_PER_STEP
    sums, masks, levels = _hgrn_tables(chunk)
    col = lambda j: pl.BlockSpec((n_seqs, tile, A_WIDTH), lambda b, t: (b, t, j))
    state_spec = pl.BlockSpec((n_seqs, A_HEADS, A_HEAD, A_HEAD), lambda b, t: (b, 0, 0, 0))
    out, state = pl.pallas_call(
        functools.partial(_hgrn_chunk_kernel, chunk=chunk, n_chunks=tile // chunk, levels=levels),
        grid=(batch // n_seqs, seq // tile),
        in_specs=[col(0), col(1), col(2), _full(lb_logits.shape), _full(sums.shape),
                  _full(masks.shape), state_spec],
        out_specs=[col(0), state_spec],
        out_shape=[jax.ShapeDtypeStruct((batch, seq, A_WIDTH), F32),
                   jax.ShapeDtypeStruct(s0.shape, F32)],
        scratch_shapes=[pltpu.VMEM((n_seqs, A_HEADS, A_HEAD, A_HEAD), F32)],
        compiler_params=_params("arbitrary", "arbitrary"),
        name="hgrn_chunked",
    )(*[ua.reshape(batch, seq, A_COLS)] * 3, lb_logits, sums, masks, s0)
    return out.reshape(batch * seq, A_WIDTH), state


def _bcast_row(ref, t, j):
    return jnp.broadcast_to(ref[t, pl.ds(j, 1), :], (V7X_SUBLANES, V7X_LANES))


def _delta_rule_kernel(kk0_ref, kkn_ref, w_ref, b_ref, k_ref, r_ref, v_ref, s0_ref,
                       y_ref, s_ref, sk_ref, *, nib, nj, tb):
    @pl.when(pl.program_id(1) == 0)
    def _():
        s_ref[...] = s0_ref[...]
        for ib in range(nib):
            acc = None
            for j in range(nj):
                kk0 = jnp.broadcast_to(kk0_ref[pl.ds(j, 1), :], (V7X_SUBLANES, V7X_LANES))
                term = s0_ref[ib, j] * kk0
                acc = term if acc is None else acc + term
            sk_ref[ib] = acc

    def step(t, carry):
        sk = [sk_ref[ib] for ib in range(nib)]
        vv = [v_ref[t, ib] for ib in range(nib)]
        yacc = [None] * nib
        skn = [None] * nib
        for j in range(nj):
            wj = _bcast_row(w_ref, t, j)
            bj = _bcast_row(b_ref, t, j)
            kj = _bcast_row(k_ref, t, j)
            rj = _bcast_row(r_ref, t, j)
            nj_kk = _bcast_row(kkn_ref, t, j)
            for ib in range(nib):
                s = s_ref[ib, j] * wj - sk[ib] * bj + vv[ib] * kj
                s_ref[ib, j] = s
                yt = s * rj
                st = s * nj_kk
                yacc[ib] = yt if yacc[ib] is None else yacc[ib] + yt
                skn[ib] = st if skn[ib] is None else skn[ib] + st
        for ib in range(nib):
            y_ref[t, ib] = yacc[ib]
            sk_ref[ib] = skn[ib]
        return carry

    lax.fori_loop(0, tb, step, 0)


def _delta_rule_chain_kernel(kk_ref, gamma_ref, b_ref, k_ref, r_ref, kk_head_ref, v_ref, s0_ref,
                             y_ref, s_ref, sk_ref, kkx_ref, *, nib, nj, tb):
    full = (V7X_SUBLANES, V7X_LANES)
    row = lambda ref, j, t: jnp.broadcast_to(ref[j, pl.ds(t, 1), :], full)

    @pl.when(pl.program_id(0) == 0)
    def _():
        s_ref[...] = s0_ref[...]
        for ib in range(nib):
            acc = None
            for j in range(nj):
                term = s0_ref[ib, j] * row(kk_ref, j, 0)
                acc = term if acc is None else acc + term
            sk_ref[ib] = acc

    kkx_ref[:, 0:tb, :] = kk_ref[...]
    kkx_ref[:, tb:tb + V7X_SUBLANES, :] = (
        kk_head_ref[...] * gamma_ref[:, tb - 1:tb, :])

    def step(t, carry):
        sk = [sk_ref[ib] for ib in range(nib)]
        vv = [v_ref[t, ib] for ib in range(nib)]
        yacc = [None] * nib
        skn = [None] * nib
        for j in range(nj):
            bj, kj, rj = (row(ref, j, t) for ref in (b_ref, k_ref, r_ref))
            kkn = row(kkx_ref, j, t + 1)
            for ib in range(nib):
                s = s_ref[ib, j] + (vv[ib] * kj - sk[ib] * bj)
                s_ref[ib, j] = s
                yt = s * rj
                st = s * kkn
                yacc[ib] = yt if yacc[ib] is None else yacc[ib] + yt
                skn[ib] = st if skn[ib] is None else skn[ib] + st
        for ib in range(nib):
            y_ref[t, ib] = yacc[ib]
            sk_ref[ib] = skn[ib]
        return carry

    lax.fori_loop(0, tb, step, 0)
    for j in range(nj):
        total = row(gamma_ref, j, tb - 1)
        for ib in range(nib):
            s_ref[ib, j] = s_ref[ib, j] * total


def _delta_rule_chain(coef, v, s0):
    _, nj, t_len, _ = coef.shape
    nib = v.shape[1]
    tb = TIME_BLOCK
    n_blocks = t_len // tb
    head_rows = V7X_SUBLANES
    coef_spec = lambda a: pl.BlockSpec((None, nj, tb, V7X_LANES), lambda t: (a, 0, t, 0))
    head_spec = pl.BlockSpec(
        (None, nj, head_rows, V7X_LANES),
        lambda t: (0, 0, jnp.minimum(t + 1, n_blocks - 1) * (tb // head_rows), 0))
    row_spec = pl.BlockSpec((tb, nib, V7X_SUBLANES, V7X_LANES), lambda t: (t, 0, 0, 0))
    return pl.pallas_call(
        functools.partial(_delta_rule_chain_kernel, nib=nib, nj=nj, tb=tb),
        grid=(n_blocks,),
        in_specs=[coef_spec(a) for a in range(N_COEF)]
        + [head_spec, row_spec, _full(s0.shape)],
        out_specs=[row_spec, _full(s0.shape)],
        out_shape=[jax.ShapeDtypeStruct(v.shape, F32), jax.ShapeDtypeStruct(s0.shape, F32)],
        scratch_shapes=[pltpu.VMEM((nib, V7X_SUBLANES, V7X_LANES), F32),
                        pltpu.VMEM((nj, tb + head_rows, V7X_LANES), F32)],
        compiler_params=_params("arbitrary"),
        name="delta_rule_chain",
    )(coef, coef, coef, coef, coef, coef, v, s0)


def _decay_rule_kernel(w_ref, k_ref, r_ref, v_ref, s0_ref, y_ref, s_ref, st_ref,
                       *, nib, nj, t_len):
    full = (V7X_SUBLANES, V7X_LANES)
    for j in range(nj):
        tile = s0_ref[:, j * V7X_LANES:(j + 1) * V7X_LANES]
        st_ref[j] = tile.T.reshape(nib, V7X_SUBLANES, V7X_LANES)
    for t in range(t_len):
        vv = [v_ref[t, ib] for ib in range(nib)]

        def one_key(j, yacc):
            wj, kj, rj = (jnp.broadcast_to(ref[t, pl.ds(j, 1), :], full)
                          for ref in (w_ref, k_ref, r_ref))
            out = []
            for ib in range(nib):
                s = st_ref[j, ib] * wj + vv[ib] * kj
                st_ref[j, ib] = s
                out.append(yacc[ib] + s * rj)
            return tuple(out)

        yacc = lax.fori_loop(0, nj, one_key, tuple(jnp.zeros(full, F32) for _ in range(nib)))
        for ib in range(nib):
            y_ref[t, ib] = yacc[ib]
    for j in range(nj):
        s_ref[:, j * V7X_LANES:(j + 1) * V7X_LANES] = (
            st_ref[j].reshape(nib * V7X_SUBLANES, V7X_LANES).T)


def _decay_rule(coefs, v, s0):
    t_len, heads, nib = v.shape[0], v.shape[1], v.shape[2]
    nj = coefs[0].shape[2]
    chains = s0.shape[0]
    assert chains == V7X_LANES and nib * V7X_SUBLANES == V7X_LANES
    width = nj * V7X_LANES
    coef_spec = pl.BlockSpec((t_len, None, nj, V7X_LANES), lambda h: (0, h, 0, 0))
    row_spec = pl.BlockSpec((t_len, None, nib, V7X_SUBLANES, V7X_LANES),
                            lambda h: (0, h, 0, 0, 0))
    state_spec = pl.BlockSpec((chains, width), lambda h: (0, h))
    y, state = pl.pallas_call(
        functools.partial(_decay_rule_kernel, nib=nib, nj=nj, t_len=t_len),
        grid=(heads,),
        in_specs=[coef_spec] * 3 + [row_spec, state_spec],
        out_specs=[row_spec, state_spec],
        out_shape=[jax.ShapeDtypeStruct(v.shape, F32),
                   jax.ShapeDtypeStruct((chains, heads * width), F32)],
        scratch_shapes=[pltpu.VMEM((nj, nib, V7X_SUBLANES, V7X_LANES), F32)],
        compiler_params=_params("arbitrary"),
        name="decay_rule",
    )(*coefs, v, s0.reshape(chains, heads * width))
    return y, state.reshape(s0.shape)


def _delta_rule(coefs, v, s0, *, kk0, groups_per_coef):
    t_len, q, nib = v.shape[0], v.shape[1], v.shape[2]
    nj = coefs[0].shape[2]
    tb = min(TIME_BLOCK, t_len)
    coef_spec = pl.BlockSpec((tb, None, nj, V7X_LANES),
                             lambda g, t: (t, g // groups_per_coef, 0, 0))
    row_spec = pl.BlockSpec((tb, None, nib, V7X_SUBLANES, V7X_LANES),
                            lambda g, t: (t, g, 0, 0, 0))
    state_spec = pl.BlockSpec((None, nib, nj, V7X_SUBLANES, V7X_LANES),
                              lambda g, t: (g, 0, 0, 0, 0))
    kk0_spec = pl.BlockSpec((None, nj, V7X_LANES), lambda g, t: (g // groups_per_coef, 0, 0))
    return pl.pallas_call(
        functools.partial(_delta_rule_kernel, nib=nib, nj=nj, tb=tb),
        grid=(q, t_len // tb),
        in_specs=[kk0_spec] + [coef_spec] * 5 + [row_spec, state_spec],
        out_specs=[row_spec, state_spec],
        out_shape=[jax.ShapeDtypeStruct(v.shape, F32), jax.ShapeDtypeStruct(s0.shape, F32)],
        scratch_shapes=[pltpu.VMEM((nib, V7X_SUBLANES, V7X_LANES), F32)],
        compiler_params=_params("arbitrary", "arbitrary"),
        name="delta_rule",
    )(kk0, *coefs, v, s0)


def _mix_kernel(x_ref, oa_ref, ga_ref, y_ref, bonus_ref, gb_ref, gn_ref, wg_ref, hw_ref, lnw_ref,
                lnb_ref, wa_ref, wb_ref, wo_ref, x1_ref):
    def sub_tile(rows):
        x = x_ref[rows, :]
        gate = jax.nn.sigmoid(_bdot(_rmsnorm(x, gn_ref[...]), wg_ref[...]))
        yield
        oa = oa_ref[rows, :]
        ms = _head_sum_lanes(oa * oa, A_HEAD) * (1.0 / A_HEAD)
        oa = oa * lax.rsqrt(ms + HGRN_NORM_EPS) * hw_ref[...] * jax.nn.silu(ga_ref[rows, :])
        ya = _bdot(oa, wa_ref[...])
        yield
        y = y_ref[rows, :]
        d = y - _head_sum_rolled(y, B_HEADS) * (1.0 / B_HEAD)
        var = _head_sum_rolled(d * d, B_HEADS) * (1.0 / B_HEAD)
        yn = d * lax.rsqrt(var + GN_EPS) * lnw_ref[...] + lnb_ref[...]
        ob = (yn + bonus_ref[rows, :]) * gb_ref[rows, :]
        yb = _bdot(ob, wb_ref[...])
        yield
        merged = gate[:, 0:D_MODEL] * ya + gate[:, D_MODEL:GATE_COLS] * yb
        x1_ref[rows, :] = x + _bdot(merged, wo_ref[...])

    _round_robin(sub_tile(rows) for rows in _sub_tile_rows(x_ref.shape[0]))


def _mix(x, oa, ua, y, bonus, gb, gn, wg, hw, lnw, lnb, wa, wb, wo):
    n = x.shape[0]
    tm = TOKEN_TILE * SUBTILES
    row = lambda c: pl.BlockSpec((tm, c), lambda i: (i, 0))
    vec = _full((1, A_WIDTH))
    return pl.pallas_call(
        _mix_kernel,
        grid=(n // tm,),
        in_specs=[row(D_MODEL), row(A_WIDTH), pl.BlockSpec((tm, A_WIDTH), lambda i: (i, 3)),
                  row(B_WIDTH), row(B_WIDTH), row(B_WIDTH), _full((1, D_MODEL)),
                  _full((D_MODEL, GATE_COLS)), vec, vec, vec,
                  _full((A_WIDTH, D_MODEL)), _full((B_WIDTH, D_MODEL)),
                  _full((D_MODEL, D_MODEL))],
        out_specs=row(D_MODEL),
        out_shape=jax.ShapeDtypeStruct((n, D_MODEL), F32),
        compiler_params=_params("arbitrary"),
        name="mix",
    )(x, oa, ua, y, bonus, gb, gn, wg, hw, lnw, lnb, wa, wb, wo)


def _mlp_kernel(x_ref, g_ref, wu_ref, wd_ref, gf_ref, o_ref):
    def sub_tile(rows):
        x1 = x_ref[rows, :]
        hb = _rmsnorm(x1, g_ref[...]).astype(BF16)
        yield
        act = jnp.square(jnp.maximum(jnp.dot(hb, wu_ref[...], preferred_element_type=F32), 0.0))
        yield
        x2 = x1 + _bdot(act, wd_ref[...])
        yield
        o_ref[rows, :] = _rmsnorm(x2, gf_ref[...])

    _round_robin(sub_tile(rows) for rows in _sub_tile_rows(x_ref.shape[0]))


def _mlp(x1, g, wu, wd, gf):
    n = x1.shape[0]
    tm = TOKEN_TILE * SUBTILES
    row = pl.BlockSpec((tm, D_MODEL), lambda i: (i, 0))
    return pl.pallas_call(
        _mlp_kernel,
        grid=(n // tm,),
        in_specs=[row, _full((1, D_MODEL)), _full((D_MODEL, D_FF)), _full((D_FF, D_MODEL)),
                  _full((1, D_MODEL))],
        out_specs=row,
        out_shape=jax.ShapeDtypeStruct((n, D_MODEL), F32),
        compiler_params=_params("arbitrary"),
        name="mlp",
    )(x1, g, wu, wd, gf)


class _SampleLayout:
    def __init__(self, batch, seq, heads, nj, ni, head_minor):
        assert batch == V7X_LANES
        self.b, self.t, self.h, self.nj, self.ni = batch, seq, heads, nj, ni
        self.head_minor = head_minor
        self.groups_per_coef = ni // ROWS_PER_GROUP
        self.nib = ROWS_PER_GROUP // V7X_SUBLANES

    def _to_chain(self, x, width):
        b, t, h = self.b, self.t, self.h
        if self.head_minor:
            return x.reshape(t, b, width, h).transpose(0, 3, 2, 1)
        return x.reshape(t, b, h, width).transpose(0, 2, 3, 1)

    def coef(self, x):
        return self._to_chain(x, self.nj)

    def rows_in(self, v):
        v = self._to_chain(v, self.ni)
        return v.reshape(self.t, self.h * self.groups_per_coef, self.nib, V7X_SUBLANES, V7X_LANES)

    def rows_out(self, y):
        b, t, h, ni = self.b, self.t, self.h, self.ni
        y = y.reshape(t, h, ni, b)
        y = y.transpose(0, 3, 2, 1) if self.head_minor else y.transpose(0, 3, 1, 2)
        return y.reshape(t * b, h * ni)

    def state_in(self, s, rows_last):
        b, h, nj = self.b, self.h, self.nj
        if not rows_last:
            s = s.transpose(0, 1, 3, 2)
        s = s.reshape(b, h, nj, self.groups_per_coef, self.nib, V7X_SUBLANES)
        s = s.transpose(1, 3, 4, 2, 5, 0)
        return s.reshape(h * self.groups_per_coef, self.nib, nj, V7X_SUBLANES, V7X_LANES)

    def state_out(self, s, rows_last):
        b, h, ni, nj = self.b, self.h, self.ni, self.nj
        s = s.reshape(h, self.groups_per_coef, self.nib, nj, V7X_SUBLANES, b)
        s = s.transpose(5, 0, 3, 1, 2, 4).reshape(b, h, nj, ni)
        return s if rows_last else s.transpose(0, 1, 3, 2)


def _prompt_state_to_chain(s, parts):
    b, h, ni, nj = s.shape
    nib = ni // parts // V7X_SUBLANES
    s = s.reshape(b, h, parts, nib, V7X_SUBLANES, nj).transpose(3, 5, 4, 2, 0, 1)
    return s.reshape(nib, nj, V7X_SUBLANES, V7X_LANES)


def _prompt_state_from_chain(s, batch, heads, parts):
    nib, nj = s.shape[0], s.shape[1]
    s = s.reshape(nib, nj, V7X_SUBLANES, parts, batch, heads).transpose(4, 5, 3, 0, 2, 1)
    return s.reshape(batch, heads, parts * nib * V7X_SUBLANES, nj)


def _same_head_matrix(width, heads):
    idx = np.arange(width) % heads
    return jnp.asarray(idx[:, None] == idx[None, :], BF16)


def _trunk(x, shift0, state_a, state_b, wts, *, time_major):
    batch, seq, _ = x.shape
    n = batch * seq
    if time_major:
        x2 = x.transpose(1, 0, 2).reshape(n, D_MODEL)
        time_stride = batch
    else:
        x2 = x.reshape(n, D_MODEL)
        time_stride = 1

    ua, ub = _norm_proj(x2, wts["norm_mix_g"], wts["w_in_a"], wts["w_in_b"])
    p0 = _prev_proj(shift0, wts["w_in_b"])
    prep_params = (wts["mu_shift"], wts["w_lora"], wts["w_decay0"], wts["a0"], wts["k_k"],
                   wts["k_a"], wts["r_k"], wts["same_head_b"])

    if time_major:
        kk, w, bco, k2, r, v, gb, bonus = _rwkv_prep(ub, p0, prep_params,
                                                     time_stride=time_stride)
        lay_b = _SampleLayout(batch, seq, B_HEADS, B_HEAD, B_HEAD, head_minor=True)
        kk_c = lay_b.coef(kk)
        kk_next = jnp.concatenate([kk_c[1:], jnp.zeros_like(kk_c[:1])], axis=0)
        y_c, sb_c = _delta_rule(
            (kk_next, lay_b.coef(w), lay_b.coef(bco), lay_b.coef(k2), lay_b.coef(r)),
            lay_b.rows_in(v), lay_b.state_in(state_b, rows_last=False), kk0=kk_c[0],
            groups_per_coef=lay_b.groups_per_coef)
        y_b = lay_b.rows_out(y_c)
        new_wkv = lay_b.state_out(sb_c, rows_last=False)
    else:
        parts = V7X_LANES // (batch * B_HEADS)
        coef, v_c, gb, bonus = _rwkv_prep_chain(ub, p0, prep_params, batch=batch, seq=seq)
        gb = gb.reshape(n, B_WIDTH)
        bonus = bonus.reshape(n, B_WIDTH)
        nib = B_HEAD // parts // V7X_SUBLANES
        y_c, sb_c = _delta_rule_chain(
            coef, v_c.reshape(seq, nib, V7X_SUBLANES, V7X_LANES),
            _prompt_state_to_chain(state_b, parts))
        o_a, new_hgrn = _hgrn_chunked(ua, wts["lb_logits"], state_a, batch=batch, seq=seq)
        y_b = _rwkv_unpack(y_c.reshape(seq, nib * V7X_SUBLANES, V7X_LANES), batch=batch, seq=seq)
        new_wkv = _prompt_state_from_chain(sb_c, batch, B_HEADS, parts)

    if time_major:
        qs, fg, kc = _hgrn_prep(ua, wts["lb_logits"])
        lay_a = _SampleLayout(batch, seq, A_HEADS, A_HEAD, A_HEAD, head_minor=False)
        vi = lay_a.coef(ua[:, 2 * A_WIDTH:3 * A_WIDTH])
        o_c, new_hgrn = _decay_rule(
            (lay_a.coef(fg), lay_a.coef(kc), lay_a.coef(qs)),
            vi.reshape(seq, A_HEADS, A_HEAD // V7X_SUBLANES, V7X_SUBLANES, V7X_LANES), state_a)
        o_a = lay_a.rows_out(o_c)

    x1 = _mix(x2, o_a, ua, y_b, bonus, gb, wts["norm_mix_g"], wts["w_in_g"],
              wts["hgrn_norm_w"], wts["ln_x_w"],
              wts["ln_x_b"], wts["w_a_out"], wts["w_b_out"], wts["w_out"])
    y = _mlp(x1, wts["norm_mlp_g"], wts["w_up"], wts["w_down"], wts["norm_final_g"])
    if time_major:
        y = y.reshape(seq, batch, D_MODEL).transpose(1, 0, 2)
    else:
        y = y.reshape(batch, seq, D_MODEL)
    new_shift = _norm_rows(x[:, -1, :], wts["norm_mix_g"])
    return y, new_hgrn[None], new_wkv[None], new_shift[None]


def kernel(x_prompt, x_sample, state_hgrn, state_wkv, state_shift, norm_mix_g, w_in, mu_shift,
           w_decay0, w_decay_up, a0, w_aaa_up, w_gate_up, k_k, k_a, r_k, ln_x_w, ln_x_b,
           lb_logits, hgrn_norm_w, w_a_out, w_b_out, w_out, norm_mlp_g, w_up, w_down,
           norm_final_g):
    assert w_in.shape[0] == 1, "single-layer stack"
    def pcols(a):
        lead = a.shape[:-1]
        return a.reshape(*lead, B_HEADS, B_HEAD).swapaxes(-1, -2).reshape(*lead, B_WIDTH)

    prows = lambda a: a.reshape(B_HEADS, B_HEAD, -1).swapaxes(0, 1).reshape(B_WIDTH, -1)
    w_in0 = w_in[0]
    w_in_b = w_in0[:, A_COLS:A_COLS + B_COLS]
    w_in_b = jnp.concatenate(
        [pcols(w_in_b[:, s * B_WIDTH:(s + 1) * B_WIDTH]) for s in range(3)]
        + [w_in_b[:, 3 * B_WIDTH:]], axis=1)
    mu = mu_shift[0]
    mu = jnp.concatenate([pcols(mu[s * B_WIDTH:(s + 1) * B_WIDTH]) for s in range(3)]
                         + [mu[3 * B_WIDTH:]])
    w_lora = jnp.zeros((LORA_COLS, 3 * B_WIDTH), F32)
    w_lora = w_lora.at[:DECAY_LORA, :B_WIDTH].set(pcols(w_decay_up[0]))
    w_lora = w_lora.at[DECAY_LORA:DECAY_LORA + AAA_LORA, B_WIDTH:2 * B_WIDTH].set(
        pcols(w_aaa_up[0]))
    w_lora = w_lora.at[DECAY_LORA + AAA_LORA:, 2 * B_WIDTH:].set(pcols(w_gate_up[0]))
    row = lambda a: a.reshape(1, -1).astype(F32)
    prow = lambda a: row(pcols(a.reshape(-1)))
    wts = {
        "norm_mix_g": row(norm_mix_g[0]),
        "w_in_a": w_in0[:, :A_COLS].astype(BF16),
        "w_in_b": w_in_b.astype(BF16),
        "w_in_g": w_in0[:, A_COLS + B_COLS:].astype(BF16),
        "mu_shift": row(mu),
        "w_lora": w_lora.astype(BF16),
        "w_decay0": prow(w_decay0[0]),
        "a0": prow(a0[0]),
        "k_k": prow(k_k[0]),
        "k_a": prow(k_a[0]),
        "r_k": prow(r_k[0]),
        "ln_x_w": prow(ln_x_w[0]),
        "ln_x_b": prow(ln_x_b[0]),
        "lb_logits": lb_logits.astype(F32),
        "hgrn_norm_w": row(hgrn_norm_w[0]),
        "w_a_out": w_a_out[0].astype(BF16),
        "w_b_out": prows(w_b_out[0]).astype(BF16),
        "w_out": w_out[0].astype(BF16),
        "norm_mlp_g": row(norm_mlp_g[0]),
        "w_up": w_up[0].astype(BF16),
        "w_down": w_down[0].astype(BF16),
        "norm_final_g": row(norm_final_g),
        "same_head_b": _same_head_matrix(B_WIDTH, B_HEADS),
    }
    bp = x_prompt.shape[0]
    y_p, hgrn_p, wkv_p, shift_p = _trunk(
        x_prompt, jnp.zeros((bp, D_MODEL), F32),
        jnp.zeros((bp, A_HEADS, A_HEAD, A_HEAD), F32),
        jnp.zeros((bp, B_HEADS, B_HEAD, B_HEAD), F32), wts, time_major=False)
    y_s, hgrn_s, wkv_s, shift_s = _trunk(
        x_sample, state_shift[0], state_hgrn[0], state_wkv[0], wts, time_major=True)
    return (y_p, y_s, hgrn_p, wkv_p, shift_p, hgrn_s, wkv_s, shift_s)
```

```python
import functools

import jax
import jax.numpy as jnp
import numpy as np
from jax import lax
from jax.experimental import pallas as pl
from jax.experimental.pallas import tpu as pltpu

F32 = jnp.float32
BF16 = jnp.bfloat16

D_MODEL = 1024
A_WIDTH = 512
A_HEADS = 4
A_HEAD = 128
B_WIDTH = 512
B_HEADS = 8
B_HEAD = 64
DECAY_LORA = 64
AAA_LORA = 64
GATE_LORA = 128
LORA_COLS = DECAY_LORA + AAA_LORA + GATE_LORA
D_FF = 4 * D_MODEL
A_COLS = 4 * A_WIDTH
B_COLS = 3 * B_WIDTH + LORA_COLS
GATE_COLS = 2 * D_MODEL
NORM_EPS = 1e-6
HGRN_NORM_EPS = 1e-5
GN_EPS = 64e-5
DECAY_SCALE = 0.6065306597126334

V7X_LANES = 128
V7X_SUBLANES = 8
V7X_VMEM_LIMIT_BYTES = 56 * 1024 * 1024

TOKEN_TILE = 256
SUBTILES = 2
TIME_BLOCK = 64
ROWS_PER_GROUP = 64


def _params(*semantics):
    return pltpu.CompilerParams(dimension_semantics=semantics,
                                vmem_limit_bytes=V7X_VMEM_LIMIT_BYTES)


def _full(shape):
    return pl.BlockSpec(shape, lambda *_: (0,) * len(shape))


def _sub_tile_rows(rows):
    sub = rows // SUBTILES
    return [pl.ds(part * sub, sub) for part in range(SUBTILES)]


def _round_robin(streams):
    streams = list(streams)
    while streams:
        streams = [g for g in streams if next(g, StopIteration) is not StopIteration]


def _rmsnorm(x, g):
    return x * lax.rsqrt(jnp.mean(x * x, axis=-1, keepdims=True) + NORM_EPS) * g


def _bdot(a, w):
    return jnp.dot(a.astype(BF16), w, preferred_element_type=F32)


def _head_sum_rolled(x, heads):
    tiles = x.shape[1] // V7X_LANES
    t = x[:, 0:V7X_LANES]
    for c in range(1, tiles):
        t = t + x[:, c * V7X_LANES:(c + 1) * V7X_LANES]
    shift = heads
    while shift < V7X_LANES:
        t = t + pltpu.roll(t, shift, 1)
        shift *= 2
    return jnp.concatenate([t] * tiles, axis=1)


def _head_sum_lanes(x, head):
    out = []
    for h in range(x.shape[1] // head):
        seg = x[:, h * head:(h + 1) * head]
        out.append(jnp.broadcast_to(jnp.sum(seg, axis=-1, keepdims=True), seg.shape))
    return jnp.concatenate(out, axis=1)


def _head_sum(a, same_head):
    hi = a.astype(BF16)
    lo = (a - hi.astype(F32)).astype(BF16)
    return (jnp.dot(hi, same_head, preferred_element_type=F32)
            + jnp.dot(lo, same_head, preferred_element_type=F32))


def _norm_proj_kernel(x_ref, g_ref, *refs):
    w_refs, out_refs = refs[:len(refs) // 2], refs[len(refs) // 2:]

    def sub_tile(rows):
        hb = _rmsnorm(x_ref[rows, :], g_ref[...]).astype(BF16)
        for w_ref, out_ref in zip(w_refs, out_refs):
            yield
            out_ref[rows, :] = jnp.dot(hb, w_ref[...], preferred_element_type=F32)

    _round_robin(sub_tile(rows) for rows in _sub_tile_rows(x_ref.shape[0]))


def _norm_proj(x, g, weights):
    n = x.shape[0]
    tm = TOKEN_TILE * SUBTILES
    row = lambda c: pl.BlockSpec((tm, c), lambda i: (i, 0))
    return pl.pallas_call(
        _norm_proj_kernel,
        grid=(n // tm,),
        in_specs=[row(D_MODEL), _full((1, D_MODEL))] + [_full(w.shape) for w in weights],
        out_specs=[row(w.shape[1]) for w in weights],
        out_shape=[jax.ShapeDtypeStruct((n, w.shape[1]), F32) for w in weights],
        compiler_params=_params("arbitrary"),
        name="norm_proj",
    )(x, g, *weights)


def _norm_rows_kernel(x_ref, g_ref, o_ref):
    o_ref[...] = _rmsnorm(x_ref[...], g_ref[...])


def _norm_rows(x, g):
    return pl.pallas_call(
        _norm_rows_kernel,
        out_shape=jax.ShapeDtypeStruct(x.shape, F32),
        name="norm_rows",
    )(x, g)


def _prev_proj_kernel(h_ref, w_ref, o_ref):
    o_ref[...] = _bdot(h_ref[...], w_ref[...])


def _prev_proj(h_prev, wb):
    return pl.pallas_call(
        _prev_proj_kernel,
        out_shape=jax.ShapeDtypeStruct((h_prev.shape[0], B_COLS), F32),
        compiler_params=_params(),
        name="prev_proj",
    )(h_prev, wb)


def _rwkv_coefficients(ub, up, mu_ref, wlora_ref, wd0_ref, a0_ref, kk_w_ref, ka_ref, rk_ref,
                       same_head_ref):
    xm = ub + (up - ub) * mu_ref[...]
    r = xm[:, 0:B_WIDTH]
    k = xm[:, B_WIDTH:2 * B_WIDTH]
    v = xm[:, 2 * B_WIDTH:3 * B_WIDTH]
    lo = xm[:, 3 * B_WIDTH:]
    col = lax.broadcasted_iota(jnp.int32, lo.shape, 1)
    act = jnp.where(col < DECAY_LORA, jnp.tanh(lo),
                    jnp.where(col < DECAY_LORA + AAA_LORA, lo, jax.nn.sigmoid(lo)))
    yield
    up_proj = _bdot(act, wlora_ref[...])
    yield
    logw = -DECAY_SCALE * jax.nn.sigmoid(wd0_ref[...] + up_proj[:, 0:B_WIDTH])
    a = jax.nn.sigmoid(a0_ref[...] + up_proj[:, B_WIDTH:2 * B_WIDTH])
    same_head = same_head_ref[...]
    kk = k * kk_w_ref[...]
    norm2 = _head_sum(kk * kk, same_head)
    yield
    kk = kk / jnp.maximum(jnp.sqrt(norm2), 1e-12)
    k2 = k * (1.0 + (a - 1.0) * ka_ref[...])
    g = up_proj[:, 2 * B_WIDTH:]
    rk_sum = _head_sum(r * k2 * rk_ref[...], same_head)
    yield
    return (kk, logw, kk * a, k2, r, v), g, rk_sum * v


def _run_to_end(gen):
    while True:
        try:
            next(gen)
        except StopIteration as done:
            return done.value


def _rwkv_prep_kernel(ub_ref, p0_ref, mu_ref, wlora_ref, wd0_ref, a0_ref, kk_w_ref, ka_ref,
                      rk_ref, same_head_ref, kk_ref, w_ref, b_ref, k_ref, r_ref, v_ref, g_ref,
                      bonus_ref, *, time_stride):
    ub = ub_ref[...]
    up = jnp.concatenate([p0_ref[...], ub[:ub.shape[0] - time_stride, :]], axis=0)
    coefs, g, bonus = _run_to_end(_rwkv_coefficients(
        ub, up, mu_ref, wlora_ref, wd0_ref, a0_ref, kk_w_ref, ka_ref, rk_ref, same_head_ref))
    kk, logw, bco, k2, r, v = coefs
    for ref, val in zip((kk_ref, w_ref, b_ref, k_ref, r_ref, v_ref),
                        (kk, jnp.exp(logw), bco, k2, r, v)):
        ref[...] = val
    g_ref[...] = g
    bonus_ref[...] = bonus


def _rwkv_prep(ub, p0, prm, *, time_stride):
    n = ub.shape[0]
    return pl.pallas_call(
        functools.partial(_rwkv_prep_kernel, time_stride=time_stride),
        out_shape=[jax.ShapeDtypeStruct((n, B_WIDTH), F32)] * 8,
        compiler_params=_params(),
        name="rwkv_prep",
    )(ub, p0, *prm)


CHAIN_TILE = 128
N_COEF = 5
PREP_SEQS_PER_ITER = 2
PREP_KEY_SLICES = 4


def _block_prefix_ones(rows, block):
    t = np.arange(rows)
    return jnp.asarray((t[None, :] <= t[:, None]) & (t[None, :] // block == t[:, None] // block),
                       BF16)


def _rwkv_prep_chain_kernel(x_ref, gn_ref, wb_ref, p0_ref, mu_ref, wlora_ref, wd0_ref, a0_ref,
                            kk_w_ref, ka_ref, rk_ref, same_head_ref, prefix_ref, coef_ref, v_ref,
                            g_ref, bonus_ref, carry_ref, xt_ref, *, batch, parts):
    tile = pl.program_id(0)
    key_slice = pl.program_id(1)
    half_rows = B_HEAD // parts

    @pl.when(key_slice == 0)
    def _():
        def per_seq(b):
            ub = _bdot(_rmsnorm(x_ref[b], gn_ref[...]), wb_ref[...])
            yield
            first = jnp.where(tile == 0, p0_ref[pl.ds(b, 1), :], carry_ref[pl.ds(b, 1), :])
            row_id = lax.broadcasted_iota(jnp.int32, ub.shape, 0)
            up = jnp.where(row_id == 0, first, pltpu.roll(ub, 1, 0))
            carry_ref[pl.ds(b, 1), :] = ub[CHAIN_TILE - 1:CHAIN_TILE, :]
            coefs, g, bonus = yield from _rwkv_coefficients(
                ub, up, mu_ref, wlora_ref, wd0_ref, a0_ref, kk_w_ref, ka_ref, rk_ref,
                same_head_ref)
            g_ref[b] = g
            bonus_ref[b] = bonus
            kk, logw, bco, k2, r, v = coefs
            cum = sum(jnp.dot(prefix_ref[...], part, preferred_element_type=F32)
                      for part in _split3(logw))
            yield
            gamma = jnp.exp(cum)
            inv_gamma = jnp.exp(-cum)
            scaled = (kk * jnp.exp(cum - logw), gamma, bco * inv_gamma, k2 * inv_gamma,
                      r * gamma, v)
            for idx, val in enumerate(scaled):
                xt_ref[idx, b] = val.T
                yield

        def per_group(group, carry):
            _round_robin(per_seq(group * PREP_SEQS_PER_ITER + s)
                         for s in range(PREP_SEQS_PER_ITER))
            return carry

        lax.fori_loop(0, batch // PREP_SEQS_PER_ITER, per_group, 0)
        for i in range(half_rows):
            m = jnp.concatenate(
                [xt_ref[N_COEF, :, pl.ds((p * half_rows + i) * B_HEADS, B_HEADS), :]
                 .reshape(batch * B_HEADS, CHAIN_TILE) for p in range(parts)], axis=0)
            v_ref[:, i, :] = m.T

    keys_per_slice = B_HEAD // PREP_KEY_SLICES
    for which in range(N_COEF):
        for jl in range(keys_per_slice):
            rows = pl.ds(pl.multiple_of((key_slice * keys_per_slice + jl) * B_HEADS, B_HEADS),
                         B_HEADS)
            m = xt_ref[which, :, rows, :].reshape(batch * B_HEADS, CHAIN_TILE)
            coef_ref[which, jl] = jnp.concatenate([m] * parts, axis=0).T


def _rwkv_prep_chain(x, gn, wb, p0, prm, *, batch, seq):
    parts = V7X_LANES // (batch * B_HEADS)
    n_tiles = seq // CHAIN_TILE
    tok = pl.BlockSpec((batch, CHAIN_TILE, B_WIDTH), lambda t, a: (0, t, 0))
    prm = (gn, wb, p0) + tuple(prm) + (_block_prefix_ones(CHAIN_TILE, TIME_BLOCK),)
    return pl.pallas_call(
        functools.partial(_rwkv_prep_chain_kernel, batch=batch, parts=parts),
        grid=(n_tiles, PREP_KEY_SLICES),
        in_specs=[pl.BlockSpec((batch, CHAIN_TILE, D_MODEL), lambda t, a: (0, t, 0))]
        + [_full(p.shape) for p in prm],
        out_specs=[pl.BlockSpec((N_COEF, B_HEAD // PREP_KEY_SLICES, CHAIN_TILE, V7X_LANES),
                                lambda t, a: (0, a, t, 0)),
                   pl.BlockSpec((CHAIN_TILE, B_HEAD // parts, V7X_LANES), lambda t, a: (t, 0, 0)),
                   tok, tok],
        out_shape=[jax.ShapeDtypeStruct((N_COEF, B_HEAD, seq, V7X_LANES), F32),
                   jax.ShapeDtypeStruct((seq, B_HEAD // parts, V7X_LANES), F32),
                   jax.ShapeDtypeStruct((batch, seq, B_WIDTH), F32),
                   jax.ShapeDtypeStruct((batch, seq, B_WIDTH), F32)],
        scratch_shapes=[pltpu.VMEM((batch, B_COLS), F32),
                        pltpu.VMEM((N_COEF + 1, batch, B_WIDTH, CHAIN_TILE), F32)],
        compiler_params=_params("arbitrary", "arbitrary"),
        name="rwkv_prep_chain",
    )(x.reshape(batch, seq, D_MODEL), *prm)


def _rwkv_unpack_kernel(y_ref, o_ref, yt_ref, *, batch, parts):
    half_rows = B_HEAD // parts
    for i in range(half_rows):
        nt = y_ref[:, i, :].T
        for p in range(parts):
            rows = slice(p * batch * B_HEADS, (p + 1) * batch * B_HEADS)
            yt_ref[:, pl.ds((p * half_rows + i) * B_HEADS, B_HEADS), :] = (
                nt[rows].reshape(batch, B_HEADS, CHAIN_TILE))
    for b in range(batch):
        o_ref[b] = yt_ref[b].T


def _rwkv_unpack(y, *, batch, seq):
    parts = V7X_LANES // (batch * B_HEADS)
    out = pl.pallas_call(
        functools.partial(_rwkv_unpack_kernel, batch=batch, parts=parts),
        grid=(seq // CHAIN_TILE,),
        in_specs=[pl.BlockSpec((CHAIN_TILE, B_HEAD // parts, V7X_LANES), lambda t: (t, 0, 0))],
        out_specs=pl.BlockSpec((batch, CHAIN_TILE, B_WIDTH), lambda t: (0, t, 0)),
        out_shape=jax.ShapeDtypeStruct((batch, seq, B_WIDTH), F32),
        scratch_shapes=[pltpu.VMEM((batch, B_WIDTH, CHAIN_TILE), F32)],
        compiler_params=_params("arbitrary"),
        name="rwkv_unpack",
    )(y)
    return out.reshape(batch * seq, B_WIDTH)


def _hgrn_prep_kernel(q_ref, f_ref, lbl_ref, qs_ref, fg_ref, kc_ref):
    logits = lbl_ref[...]
    e = jnp.exp(logits - jnp.max(logits, axis=0, keepdims=True))
    lb = e[0:1, :] / jnp.sum(e, axis=0, keepdims=True)
    fz = f_ref[...]
    fg_ref[...] = lb + (1.0 - lb) * jax.nn.sigmoid(fz)
    kc_ref[...] = (1.0 - lb) * jax.nn.sigmoid(-fz)
    qs_ref[...] = jax.nn.silu(q_ref[...])


def _hgrn_prep(ua, lb_logits):
    n = ua.shape[0]
    tm = TOKEN_TILE
    col = lambda j: pl.BlockSpec((tm, A_WIDTH), lambda i: (i, j))
    return pl.pallas_call(
        _hgrn_prep_kernel,
        grid=(n // tm,),
        in_specs=[col(0), col(1), _full(lb_logits.shape)],
        out_specs=[col(0)] * 3,
        out_shape=[jax.ShapeDtypeStruct((n, A_WIDTH), F32)] * 3,
        compiler_params=_params("arbitrary"),
        name="hgrn_prep",
    )(ua, ua, lb_logits)


HGRN_CHUNK = 64


def _hgrn_tables(chunk):
    levels = chunk.bit_length() - 1
    t = np.arange(chunk)
    u, tt = t[None, :], t[:, None]
    rows = [u <= tt]
    masks = []
    for level in range(levels):
        m = 1 << level
        anchor = (t // (2 * m)) * (2 * m) + m - 1
        right = (t % (2 * m)) >= m
        if m < V7X_SUBLANES:
            rows.append(((u > anchor[:, None]) & (u <= tt) & right[:, None])
                        | ((u > tt) & (u <= anchor[:, None]) & ~right[:, None]))
        masks.append((tt // (2 * m) == u // (2 * m)) & right[:, None] & ~right[None, :])
    return (jnp.asarray(np.concatenate(rows, 0), BF16),
            jnp.asarray(np.stack(masks), F32), levels)


def _hgrn_anchor_sums(cum, level):
    m = 1 << level
    groups = []
    for g in range(cum.shape[0] // V7X_SUBLANES):
        start = g * V7X_SUBLANES
        anchor = (start // (2 * m)) * (2 * m) + m - 1
        block = cum[start:start + V7X_SUBLANES]
        ref_row = cum[anchor:anchor + 1]
        groups.append(block - ref_row if start % (2 * m) >= m else ref_row - block)
    return jnp.concatenate(groups, axis=0)


def _split3(x):
    hi = x.astype(BF16)
    r1 = x - hi.astype(F32)
    mid = r1.astype(BF16)
    lo = (r1 - mid.astype(F32)).astype(BF16)
    return hi, mid, lo


def _dot_nt(a, b):
    return lax.dot_general(a.astype(BF16), b.astype(BF16), (((1,), (1,)), ((), ())),
                           preferred_element_type=F32)


def _dot_tn(a, b):
    return lax.dot_general(a.astype(BF16), b.astype(BF16), (((0,), (0,)), ((), ())),
                           preferred_element_type=F32)


def _hgrn_lower_bound(lbl_ref):
    logits = lbl_ref[...]
    e = jnp.exp(logits - jnp.max(logits, axis=0, keepdims=True))
    return e[0:1, :] / jnp.sum(e, axis=0, keepdims=True)


def _hgrn_chunk_kernel(q_ref, f_ref, i_ref, lbl_ref, sums_ref, mask_ref, s0_ref, o_ref,
                       s_out_ref, st_ref, *, chunk, n_chunks, levels):
    n_seqs = st_ref.shape[0]

    @pl.when(pl.program_id(1) == 0)
    def _():
        for s in range(n_seqs):
            for h in range(A_HEADS):
                st_ref[s, h] = s0_ref[s, h].T

    lb = _hgrn_lower_bound(lbl_ref)
    sums = sums_ref[...]
    seg = lambda r: slice(r * chunk, (r + 1) * chunk)

    def one_head(s, h, rows):
        hs = slice(h * A_HEAD, (h + 1) * A_HEAD)
        fz = f_ref[s, rows, hs]
        lbh = lb[:, hs]
        logf = jnp.log(lbh + (1.0 - lbh) * jax.nn.sigmoid(fz))
        kh = (1.0 - lbh) * jax.nn.sigmoid(-fz)
        qh = jax.nn.silu(q_ref[s, rows, hs])
        vh = i_ref[s, rows, hs]
        yield
        sums_out = sum(jnp.dot(sums, part, preferred_element_type=F32)
                       for part in _split3(logf))
        yield
        cum = sums_out[seg(0)]
        from_start = jnp.exp(cum)
        to_end = jnp.exp(cum[chunk - 1:chunk] - cum)
        att = None
        for level in range(levels):
            if (1 << level) < V7X_SUBLANES:
                split = jnp.exp(sums_out[seg(1 + level)])
            else:
                split = jnp.exp(_hgrn_anchor_sums(cum, level))
            term = mask_ref[level] * _dot_nt(qh * split, kh * split)
            att = term if att is None else att + term
            if level % 2 == 1:
                yield
        st = st_ref[s, h]
        o_ref[s, rows, hs] = (_bdot(att, vh.astype(BF16))
                              + jnp.sum(qh * kh, axis=-1, keepdims=True) * vh
                              + _dot_nt(qh * from_start, st))
        yield
        st_ref[s, h] = st * from_start[chunk - 1:chunk] + _dot_tn(vh, kh * to_end)
        yield

    for c in range(n_chunks):
        _round_robin(one_head(s, h, pl.ds(c * chunk, chunk))
                     for h in range(A_HEADS) for s in range(n_seqs))

    @pl.when(pl.program_id(1) == pl.num_programs(1) - 1)
    def _():
        for s in range(n_seqs):
            for h in range(A_HEADS):
                s_out_ref[s, h] = st_ref[s, h].T


HGRN_SEQS_PER_STEP = 2


def _hgrn_chunked(ua, lb_logits, s0, *, batch, seq):
    chunk = HGRN_CHUNK
    tile = TOKEN_TILE
    n_seqs = HGRN_SEQS_PER_STEP
    sums, masks, levels = _hgrn_tables(chunk)
    col = lambda j: pl.BlockSpec((n_seqs, tile, A_WIDTH), lambda b, t: (b, t, j))
    state_spec = pl.BlockSpec((n_seqs, A_HEADS, A_HEAD, A_HEAD), lambda b, t: (b, 0, 0, 0))
    out, state = pl.pallas_call(
        functools.partial(_hgrn_chunk_kernel, chunk=chunk, n_chunks=tile // chunk, levels=levels),
        grid=(batch // n_seqs, seq // tile),
        in_specs=[col(0), col(1), col(2), _full(lb_logits.shape), _full(sums.shape),
                  _full(masks.shape), state_spec],
        out_specs=[col(0), state_spec],
        out_shape=[jax.ShapeDtypeStruct((batch, seq, A_WIDTH), F32),
                   jax.ShapeDtypeStruct(s0.shape, F32)],
        scratch_shapes=[pltpu.VMEM((n_seqs, A_HEADS, A_HEAD, A_HEAD), F32)],
        compiler_params=_params("arbitrary", "arbitrary"),
        name="hgrn_chunked",
    )(*[ua.reshape(batch, seq, A_COLS)] * 3, lb_logits, sums, masks, s0)
    return out.reshape(batch * seq, A_WIDTH), state


def _bcast_row(ref, t, j):
    return jnp.broadcast_to(ref[t, pl.ds(j, 1), :], (V7X_SUBLANES, V7X_LANES))


def _delta_rule_kernel(kk0_ref, kkn_ref, w_ref, b_ref, k_ref, r_ref, v_ref, s0_ref,
                       y_ref, s_ref, sk_ref, *, nib, nj, tb):
    @pl.when(pl.program_id(1) == 0)
    def _():
        s_ref[...] = s0_ref[...]
        for ib in range(nib):
            acc = None
            for j in range(nj):
                kk0 = jnp.broadcast_to(kk0_ref[pl.ds(j, 1), :], (V7X_SUBLANES, V7X_LANES))
                term = s0_ref[ib, j] * kk0
                acc = term if acc is None else acc + term
            sk_ref[ib] = acc

    def step(t, carry):
        sk = [sk_ref[ib] for ib in range(nib)]
        vv = [v_ref[t, ib] for ib in range(nib)]
        yacc = [None] * nib
        skn = [None] * nib
        for j in range(nj):
            wj = _bcast_row(w_ref, t, j)
            bj = _bcast_row(b_ref, t, j)
            kj = _bcast_row(k_ref, t, j)
            rj = _bcast_row(r_ref, t, j)
            nj_kk = _bcast_row(kkn_ref, t, j)
            for ib in range(nib):
                s = s_ref[ib, j] * wj - sk[ib] * bj + vv[ib] * kj
                s_ref[ib, j] = s
                yt = s * rj
                st = s * nj_kk
                yacc[ib] = yt if yacc[ib] is None else yacc[ib] + yt
                skn[ib] = st if skn[ib] is None else skn[ib] + st
        for ib in range(nib):
            y_ref[t, ib] = yacc[ib]
            sk_ref[ib] = skn[ib]
        return carry

    lax.fori_loop(0, tb, step, 0)


def _delta_rule_chain_kernel(kk_ref, gamma_ref, b_ref, k_ref, r_ref, kk_head_ref, v_ref, s0_ref,
                             y_ref, s_ref, sk_ref, kkx_ref, *, nib, nj, tb):
    full = (V7X_SUBLANES, V7X_LANES)
    row = lambda ref, j, t: jnp.broadcast_to(ref[j, pl.ds(t, 1), :], full)

    @pl.when(pl.program_id(0) == 0)
    def _():
        s_ref[...] = s0_ref[...]
        for ib in range(nib):
            acc = None
            for j in range(nj):
                term = s0_ref[ib, j] * row(kk_ref, j, 0)
                acc = term if acc is None else acc + term
            sk_ref[ib] = acc

    kkx_ref[:, 0:tb, :] = kk_ref[...]
    kkx_ref[:, tb:tb + V7X_SUBLANES, :] = (
        kk_head_ref[...] * gamma_ref[:, tb - 1:tb, :])

    def step(t, carry):
        sk = [sk_ref[ib] for ib in range(nib)]
        vv = [v_ref[t, ib] for ib in range(nib)]
        yacc = [None] * nib
        skn = [None] * nib
        for j in range(nj):
            bj, kj, rj = (row(ref, j, t) for ref in (b_ref, k_ref, r_ref))
            kkn = row(kkx_ref, j, t + 1)
            for ib in range(nib):
                s = s_ref[ib, j] + (vv[ib] * kj - sk[ib] * bj)
                s_ref[ib, j] = s
                yt = s * rj
                st = s * kkn
                yacc[ib] = yt if yacc[ib] is None else yacc[ib] + yt
                skn[ib] = st if skn[ib] is None else skn[ib] + st
        for ib in range(nib):
            y_ref[t, ib] = yacc[ib]
            sk_ref[ib] = skn[ib]
        return carry

    lax.fori_loop(0, tb, step, 0)
    for j in range(nj):
        total = row(gamma_ref, j, tb - 1)
        for ib in range(nib):
            s_ref[ib, j] = s_ref[ib, j] * total


def _delta_rule_chain(coef, v, s0):
    _, nj, t_len, _ = coef.shape
    nib = v.shape[1]
    tb = TIME_BLOCK
    n_blocks = t_len // tb
    head_rows = V7X_SUBLANES
    coef_spec = lambda a: pl.BlockSpec((None, nj, tb, V7X_LANES), lambda t: (a, 0, t, 0))
    head_spec = pl.BlockSpec(
        (None, nj, head_rows, V7X_LANES),
        lambda t: (0, 0, jnp.minimum(t + 1, n_blocks - 1) * (tb // head_rows), 0))
    row_spec = pl.BlockSpec((tb, nib, V7X_SUBLANES, V7X_LANES), lambda t: (t, 0, 0, 0))
    return pl.pallas_call(
        functools.partial(_delta_rule_chain_kernel, nib=nib, nj=nj, tb=tb),
        grid=(n_blocks,),
        in_specs=[coef_spec(a) for a in range(N_COEF)]
        + [head_spec, row_spec, _full(s0.shape)],
        out_specs=[row_spec, _full(s0.shape)],
        out_shape=[jax.ShapeDtypeStruct(v.shape, F32), jax.ShapeDtypeStruct(s0.shape, F32)],
        scratch_shapes=[pltpu.VMEM((nib, V7X_SUBLANES, V7X_LANES), F32),
                        pltpu.VMEM((nj, tb + head_rows, V7X_LANES), F32)],
        compiler_params=_params("arbitrary"),
        name="delta_rule_chain",
    )(coef, coef, coef, coef, coef, coef, v, s0)


def _decay_rule_kernel(w_ref, k_ref, r_ref, v_ref, s0_ref, y_ref, s_ref, st_ref,
                       *, nib, nj, t_len):
    full = (V7X_SUBLANES, V7X_LANES)
    for j in range(nj):
        tile = s0_ref[:, j * V7X_LANES:(j + 1) * V7X_LANES]
        st_ref[j] = tile.T.reshape(nib, V7X_SUBLANES, V7X_LANES)
    for t in range(t_len):
        vv = [v_ref[t, ib] for ib in range(nib)]

        def one_key(j, yacc):
            wj, kj, rj = (jnp.broadcast_to(ref[t, pl.ds(j, 1), :], full)
                          for ref in (w_ref, k_ref, r_ref))
            out = []
            for ib in range(nib):
                s = st_ref[j, ib] * wj + vv[ib] * kj
                st_ref[j, ib] = s
                out.append(yacc[ib] + s * rj)
            return tuple(out)

        yacc = lax.fori_loop(0, nj, one_key, tuple(jnp.zeros(full, F32) for _ in range(nib)))
        for ib in range(nib):
            y_ref[t, ib] = yacc[ib]
    for j in range(nj):
        s_ref[:, j * V7X_LANES:(j + 1) * V7X_LANES] = (
            st_ref[j].reshape(nib * V7X_SUBLANES, V7X_LANES).T)


def _decay_rule(coefs, v, s0):
    t_len, heads, nib = v.shape[0], v.shape[1], v.shape[2]
    nj = coefs[0].shape[2]
    chains = s0.shape[0]
    assert chains == V7X_LANES and nib * V7X_SUBLANES == V7X_LANES
    width = nj * V7X_LANES
    coef_spec = pl.BlockSpec((t_len, None, nj, V7X_LANES), lambda h: (0, h, 0, 0))
    row_spec = pl.BlockSpec((t_len, None, nib, V7X_SUBLANES, V7X_LANES),
                            lambda h: (0, h, 0, 0, 0))
    state_spec = pl.BlockSpec((chains, width), lambda h: (0, h))
    y, state = pl.pallas_call(
        functools.partial(_decay_rule_kernel, nib=nib, nj=nj, t_len=t_len),
        grid=(heads,),
        in_specs=[coef_spec] * 3 + [row_spec, state_spec],
        out_specs=[row_spec, state_spec],
        out_shape=[jax.ShapeDtypeStruct(v.shape, F32),
                   jax.ShapeDtypeStruct((chains, heads * width), F32)],
        scratch_shapes=[pltpu.VMEM((nj, nib, V7X_SUBLANES, V7X_LANES), F32)],
        compiler_params=_params("arbitrary"),
        name="decay_rule",
    )(*coefs, v, s0.reshape(chains, heads * width))
    return y, state.reshape(s0.shape)


def _delta_rule(coefs, v, s0, *, kk0, groups_per_coef):
    t_len, q, nib = v.shape[0], v.shape[1], v.shape[2]
    nj = coefs[0].shape[2]
    tb = min(TIME_BLOCK, t_len)
    coef_spec = pl.BlockSpec((tb, None, nj, V7X_LANES),
                             lambda g, t: (t, g // groups_per_coef, 0, 0))
    row_spec = pl.BlockSpec((tb, None, nib, V7X_SUBLANES, V7X_LANES),
                            lambda g, t: (t, g, 0, 0, 0))
    state_spec = pl.BlockSpec((None, nib, nj, V7X_SUBLANES, V7X_LANES),
                              lambda g, t: (g, 0, 0, 0, 0))
    kk0_spec = pl.BlockSpec((None, nj, V7X_LANES), lambda g, t: (g // groups_per_coef, 0, 0))
    return pl.pallas_call(
        functools.partial(_delta_rule_kernel, nib=nib, nj=nj, tb=tb),
        grid=(q, t_len // tb),
        in_specs=[kk0_spec] + [coef_spec] * 5 + [row_spec, state_spec],
        out_specs=[row_spec, state_spec],
        out_shape=[jax.ShapeDtypeStruct(v.shape, F32), jax.ShapeDtypeStruct(s0.shape, F32)],
        scratch_shapes=[pltpu.VMEM((nib, V7X_SUBLANES, V7X_LANES), F32)],
        compiler_params=_params("arbitrary", "arbitrary"),
        name="delta_rule",
    )(kk0, *coefs, v, s0)


def _mix_kernel(x_ref, oa_ref, ga_ref, y_ref, bonus_ref, gb_ref, gn_ref, wg_ref, hw_ref, lnw_ref,
                lnb_ref, wa_ref, wb_ref, wo_ref, x1_ref):
    def sub_tile(rows):
        x = x_ref[rows, :]
        gate = jax.nn.sigmoid(_bdot(_rmsnorm(x, gn_ref[...]), wg_ref[...]))
        yield
        oa = oa_ref[rows, :]
        ms = _head_sum_lanes(oa * oa, A_HEAD) * (1.0 / A_HEAD)
        oa = oa * lax.rsqrt(ms + HGRN_NORM_EPS) * hw_ref[...] * jax.nn.silu(ga_ref[rows, :])
        ya = _bdot(oa, wa_ref[...])
        yield
        y = y_ref[rows, :]
        d = y - _head_sum_rolled(y, B_HEADS) * (1.0 / B_HEAD)
        var = _head_sum_rolled(d * d, B_HEADS) * (1.0 / B_HEAD)
        yn = d * lax.rsqrt(var + GN_EPS) * lnw_ref[...] + lnb_ref[...]
        ob = (yn + bonus_ref[rows, :]) * gb_ref[rows, :]
        yb = _bdot(ob, wb_ref[...])
        yield
        merged = gate[:, 0:D_MODEL] * ya + gate[:, D_MODEL:GATE_COLS] * yb
        x1_ref[rows, :] = x + _bdot(merged, wo_ref[...])

    _round_robin(sub_tile(rows) for rows in _sub_tile_rows(x_ref.shape[0]))


def _mix(x, oa, ua, y, bonus, gb, gn, wg, hw, lnw, lnb, wa, wb, wo):
    n = x.shape[0]
    tm = TOKEN_TILE * SUBTILES
    row = lambda c: pl.BlockSpec((tm, c), lambda i: (i, 0))
    vec = _full((1, A_WIDTH))
    return pl.pallas_call(
        _mix_kernel,
        grid=(n // tm,),
        in_specs=[row(D_MODEL), row(A_WIDTH), pl.BlockSpec((tm, A_WIDTH), lambda i: (i, 3)),
                  row(B_WIDTH), row(B_WIDTH), row(B_WIDTH), _full((1, D_MODEL)),
                  _full((D_MODEL, GATE_COLS)), vec, vec, vec,
                  _full((A_WIDTH, D_MODEL)), _full((B_WIDTH, D_MODEL)),
                  _full((D_MODEL, D_MODEL))],
        out_specs=row(D_MODEL),
        out_shape=jax.ShapeDtypeStruct((n, D_MODEL), F32),
        compiler_params=_params("arbitrary"),
        name="mix",
    )(x, oa, ua, y, bonus, gb, gn, wg, hw, lnw, lnb, wa, wb, wo)


def _mlp_kernel(x_ref, g_ref, wu_ref, wd_ref, gf_ref, o_ref):
    def sub_tile(rows):
        x1 = x_ref[rows, :]
        hb = _rmsnorm(x1, g_ref[...]).astype(BF16)
        yield
        act = jnp.square(jnp.maximum(jnp.dot(hb, wu_ref[...], preferred_element_type=F32), 0.0))
        yield
        x2 = x1 + _bdot(act, wd_ref[...])
        yield
        o_ref[rows, :] = _rmsnorm(x2, gf_ref[...])

    _round_robin(sub_tile(rows) for rows in _sub_tile_rows(x_ref.shape[0]))


def _mlp(x1, g, wu, wd, gf):
    n = x1.shape[0]
    tm = TOKEN_TILE * SUBTILES
    row = pl.BlockSpec((tm, D_MODEL), lambda i: (i, 0))
    return pl.pallas_call(
        _mlp_kernel,
        grid=(n // tm,),
        in_specs=[row, _full((1, D_MODEL)), _full((D_MODEL, D_FF)), _full((D_FF, D_MODEL)),
                  _full((1, D_MODEL))],
        out_specs=row,
        out_shape=jax.ShapeDtypeStruct((n, D_MODEL), F32),
        compiler_params=_params("arbitrary"),
        name="mlp",
    )(x1, g, wu, wd, gf)


class _SampleLayout:
    def __init__(self, batch, seq, heads, nj, ni, head_minor):
        assert batch == V7X_LANES
        self.b, self.t, self.h, self.nj, self.ni = batch, seq, heads, nj, ni
        self.head_minor = head_minor
        self.groups_per_coef = ni // ROWS_PER_GROUP
        self.nib = ROWS_PER_GROUP // V7X_SUBLANES

    def _to_chain(self, x, width):
        b, t, h = self.b, self.t, self.h
        if self.head_minor:
            return x.reshape(t, b, width, h).transpose(0, 3, 2, 1)
        return x.reshape(t, b, h, width).transpose(0, 2, 3, 1)

    def coef(self, x):
        return self._to_chain(x, self.nj)

    def rows_in(self, v):
        v = self._to_chain(v, self.ni)
        return v.reshape(self.t, self.h * self.groups_per_coef, self.nib, V7X_SUBLANES, V7X_LANES)

    def rows_out(self, y):
        b, t, h, ni = self.b, self.t, self.h, self.ni
        y = y.reshape(t, h, ni, b)
        y = y.transpose(0, 3, 2, 1) if self.head_minor else y.transpose(0, 3, 1, 2)
        return y.reshape(t * b, h * ni)

    def state_in(self, s, rows_last):
        b, h, nj = self.b, self.h, self.nj
        if not rows_last:
            s = s.transpose(0, 1, 3, 2)
        s = s.reshape(b, h, nj, self.groups_per_coef, self.nib, V7X_SUBLANES)
        s = s.transpose(1, 3, 4, 2, 5, 0)
        return s.reshape(h * self.groups_per_coef, self.nib, nj, V7X_SUBLANES, V7X_LANES)

    def state_out(self, s, rows_last):
        b, h, ni, nj = self.b, self.h, self.ni, self.nj
        s = s.reshape(h, self.groups_per_coef, self.nib, nj, V7X_SUBLANES, b)
        s = s.transpose(5, 0, 3, 1, 2, 4).reshape(b, h, nj, ni)
        return s if rows_last else s.transpose(0, 1, 3, 2)


def _prompt_state_to_chain(s, parts):
    b, h, ni, nj = s.shape
    nib = ni // parts // V7X_SUBLANES
    s = s.reshape(b, h, parts, nib, V7X_SUBLANES, nj).transpose(3, 5, 4, 2, 0, 1)
    return s.reshape(nib, nj, V7X_SUBLANES, V7X_LANES)


def _prompt_state_from_chain(s, batch, heads, parts):
    nib, nj = s.shape[0], s.shape[1]
    s = s.reshape(nib, nj, V7X_SUBLANES, parts, batch, heads).transpose(4, 5, 3, 0, 2, 1)
    return s.reshape(batch, heads, parts * nib * V7X_SUBLANES, nj)


def _same_head_matrix(width, heads):
    idx = np.arange(width) % heads
    return jnp.asarray(idx[:, None] == idx[None, :], BF16)


def _trunk(x, shift0, state_a, state_b, wts, *, time_major):
    batch, seq, _ = x.shape
    n = batch * seq
    if time_major:
        x2 = x.transpose(1, 0, 2).reshape(n, D_MODEL)
        time_stride = batch
    else:
        x2 = x.reshape(n, D_MODEL)
        time_stride = 1

    if time_major:
        ua, ub = _norm_proj(x2, wts["norm_mix_g"], (wts["w_in_a"], wts["w_in_b"]))
    else:
        ua, = _norm_proj(x2, wts["norm_mix_g"], (wts["w_in_a"],))
    p0 = _prev_proj(shift0, wts["w_in_b"])
    prep_params = (wts["mu_shift"], wts["w_lora"], wts["w_decay0"], wts["a0"], wts["k_k"],
                   wts["k_a"], wts["r_k"], wts["same_head_b"])

    if time_major:
        kk, w, bco, k2, r, v, gb, bonus = _rwkv_prep(ub, p0, prep_params,
                                                     time_stride=time_stride)
        lay_b = _SampleLayout(batch, seq, B_HEADS, B_HEAD, B_HEAD, head_minor=True)
        kk_c = lay_b.coef(kk)
        kk_next = jnp.concatenate([kk_c[1:], jnp.zeros_like(kk_c[:1])], axis=0)
        y_c, sb_c = _delta_rule(
            (kk_next, lay_b.coef(w), lay_b.coef(bco), lay_b.coef(k2), lay_b.coef(r)),
            lay_b.rows_in(v), lay_b.state_in(state_b, rows_last=False), kk0=kk_c[0],
            groups_per_coef=lay_b.groups_per_coef)
        y_b = lay_b.rows_out(y_c)
        new_wkv = lay_b.state_out(sb_c, rows_last=False)
    else:
        parts = V7X_LANES // (batch * B_HEADS)
        coef, v_c, gb, bonus = _rwkv_prep_chain(x2, wts["norm_mix_g"], wts["w_in_b"], p0,
                                                prep_params, batch=batch, seq=seq)
        gb = gb.reshape(n, B_WIDTH)
        bonus = bonus.reshape(n, B_WIDTH)
        nib = B_HEAD // parts // V7X_SUBLANES
        y_c, sb_c = _delta_rule_chain(
            coef, v_c.reshape(seq, nib, V7X_SUBLANES, V7X_LANES),
            _prompt_state_to_chain(state_b, parts))
        o_a, new_hgrn = _hgrn_chunked(ua, wts["lb_logits"], state_a, batch=batch, seq=seq)
        y_b = _rwkv_unpack(y_c.reshape(seq, nib * V7X_SUBLANES, V7X_LANES), batch=batch, seq=seq)
        new_wkv = _prompt_state_from_chain(sb_c, batch, B_HEADS, parts)

    if time_major:
        qs, fg, kc = _hgrn_prep(ua, wts["lb_logits"])
        lay_a = _SampleLayout(batch, seq, A_HEADS, A_HEAD, A_HEAD, head_minor=False)
        vi = lay_a.coef(ua[:, 2 * A_WIDTH:3 * A_WIDTH])
        o_c, new_hgrn = _decay_rule(
            (lay_a.coef(fg), lay_a.coef(kc), lay_a.coef(qs)),
            vi.reshape(seq, A_HEADS, A_HEAD // V7X_SUBLANES, V7X_SUBLANES, V7X_LANES), state_a)
        o_a = lay_a.rows_out(o_c)

    x1 = _mix(x2, o_a, ua, y_b, bonus, gb, wts["norm_mix_g"], wts["w_in_g"],
              wts["hgrn_norm_w"], wts["ln_x_w"],
              wts["ln_x_b"], wts["w_a_out"], wts["w_b_out"], wts["w_out"])
    y = _mlp(x1, wts["norm_mlp_g"], wts["w_up"], wts["w_down"], wts["norm_final_g"])
    if time_major:
        y = y.reshape(seq, batch, D_MODEL).transpose(1, 0, 2)
    else:
        y = y.reshape(batch, seq, D_MODEL)
    new_shift = _norm_rows(x[:, -1, :], wts["norm_mix_g"])
    return y, new_hgrn[None], new_wkv[None], new_shift[None]


def kernel(x_prompt, x_sample, state_hgrn, state_wkv, state_shift, norm_mix_g, w_in, mu_shift,
           w_decay0, w_decay_up, a0, w_aaa_up, w_gate_up, k_k, k_a, r_k, ln_x_w, ln_x_b,
           lb_logits, hgrn_norm_w, w_a_out, w_b_out, w_out, norm_mlp_g, w_up, w_down,
           norm_final_g):
    assert w_in.shape[0] == 1, "single-layer stack"
    def pcols(a):
        lead = a.shape[:-1]
        return a.reshape(*lead, B_HEADS, B_HEAD).swapaxes(-1, -2).reshape(*lead, B_WIDTH)

    prows = lambda a: a.reshape(B_HEADS, B_HEAD, -1).swapaxes(0, 1).reshape(B_WIDTH, -1)
    w_in0 = w_in[0]
    w_in_b = w_in0[:, A_COLS:A_COLS + B_COLS]
    w_in_b = jnp.concatenate(
        [pcols(w_in_b[:, s * B_WIDTH:(s + 1) * B_WIDTH]) for s in range(3)]
        + [w_in_b[:, 3 * B_WIDTH:]], axis=1)
    mu = mu_shift[0]
    mu = jnp.concatenate([pcols(mu[s * B_WIDTH:(s + 1) * B_WIDTH]) for s in range(3)]
                         + [mu[3 * B_WIDTH:]])
    w_lora = jnp.zeros((LORA_COLS, 3 * B_WIDTH), F32)
    w_lora = w_lora.at[:DECAY_LORA, :B_WIDTH].set(pcols(w_decay_up[0]))
    w_lora = w_lora.at[DECAY_LORA:DECAY_LORA + AAA_LORA, B_WIDTH:2 * B_WIDTH].set(
        pcols(w_aaa_up[0]))
    w_lora = w_lora.at[DECAY_LORA + AAA_LORA:, 2 * B_WIDTH:].set(pcols(w_gate_up[0]))
    row = lambda a: a.reshape(1, -1).astype(F32)
    prow = lambda a: row(pcols(a.reshape(-1)))
    wts = {
        "norm_mix_g": row(norm_mix_g[0]),
        "w_in_a": w_in0[:, :A_COLS].astype(BF16),
        "w_in_b": w_in_b.astype(BF16),
        "w_in_g": w_in0[:, A_COLS + B_COLS:].astype(BF16),
        "mu_shift": row(mu),
        "w_lora": w_lora.astype(BF16),
        "w_decay0": prow(w_decay0[0]),
        "a0": prow(a0[0]),
        "k_k": prow(k_k[0]),
        "k_a": prow(k_a[0]),
        "r_k": prow(r_k[0]),
        "ln_x_w": prow(ln_x_w[0]),
        "ln_x_b": prow(ln_x_b[0]),
        "lb_logits": lb_logits.astype(F32),
        "hgrn_norm_w": row(hgrn_norm_w[0]),
        "w_a_out": w_a_out[0].astype(BF16),
        "w_b_out": prows(w_b_out[0]).astype(BF16),
        "w_out": w_out[0].astype(BF16),
        "norm_mlp_g": row(norm_mlp_g[0]),
        "w_up": w_up[0].astype(BF16),
        "w_down": w_down[0].astype(BF16),
        "norm_final_g": row(norm_final_g),
        "same_head_b": _same_head_matrix(B_WIDTH, B_HEADS),
    }
    bp = x_prompt.shape[0]
    y_p, hgrn_p, wkv_p, shift_p = _trunk(
        x_prompt, jnp.zeros((bp, D_MODEL), F32),
        jnp.zeros((bp, A_HEADS, A_HEAD, A_HEAD), F32),
        jnp.zeros((bp, B_HEADS, B_HEAD, B_HEAD), F32), wts, time_major=False)
    y_s, hgrn_s, wkv_s, shift_s = _trunk(
        x_sample, state_shift[0], state_hgrn[0], state_wkv[0], wts, time_major=True)
    return (y_p, y_s, hgrn_p, wkv_p, shift_p, hgrn_s, wkv_s, shift_s)
```

```python
import functools

import jax
import jax.numpy as jnp
import numpy as np
from jax import lax
from jax.experimental import pallas as pl
from jax.experimental.pallas import tpu as pltpu

F32 = jnp.float32
BF16 = jnp.bfloat16

D_MODEL = 1024
A_WIDTH = 512
A_HEADS = 4
A_HEAD = 128
B_WIDTH = 512
B_HEADS = 8
B_HEAD = 64
DECAY_LORA = 64
AAA_LORA = 64
GATE_LORA = 128
LORA_COLS = DECAY_LORA + AAA_LORA + GATE_LORA
D_FF = 4 * D_MODEL
A_COLS = 4 * A_WIDTH
B_COLS = 3 * B_WIDTH + LORA_COLS
GATE_COLS = 2 * D_MODEL
NORM_EPS = 1e-6
HGRN_NORM_EPS = 1e-5
GN_EPS = 64e-5
DECAY_SCALE = 0.6065306597126334

V7X_LANES = 128
V7X_SUBLANES = 8
V7X_VMEM_LIMIT_BYTES = 56 * 1024 * 1024

TOKEN_TILE = 256
SUBTILES = 2
TIME_BLOCK = 64
ROWS_PER_GROUP = 64


def _params(*semantics):
    return pltpu.CompilerParams(dimension_semantics=semantics,
                                vmem_limit_bytes=V7X_VMEM_LIMIT_BYTES)


def _full(shape):
    return pl.BlockSpec(shape, lambda *_: (0,) * len(shape))


def _sub_tile_rows(rows):
    sub = rows // SUBTILES
    return [pl.ds(part * sub, sub) for part in range(SUBTILES)]


def _round_robin(streams):
    streams = list(streams)
    while streams:
        streams = [g for g in streams if next(g, StopIteration) is not StopIteration]


def _rmsnorm(x, g):
    return x * lax.rsqrt(jnp.mean(x * x, axis=-1, keepdims=True) + NORM_EPS) * g


def _bdot(a, w):
    return jnp.dot(a.astype(BF16), w, preferred_element_type=F32)


def _head_sum_rolled(x, heads):
    tiles = x.shape[1] // V7X_LANES
    t = x[:, 0:V7X_LANES]
    for c in range(1, tiles):
        t = t + x[:, c * V7X_LANES:(c + 1) * V7X_LANES]
    shift = heads
    while shift < V7X_LANES:
        t = t + pltpu.roll(t, shift, 1)
        shift *= 2
    return jnp.concatenate([t] * tiles, axis=1)


def _head_sum_lanes(x, head):
    out = []
    for h in range(x.shape[1] // head):
        seg = x[:, h * head:(h + 1) * head]
        out.append(jnp.broadcast_to(jnp.sum(seg, axis=-1, keepdims=True), seg.shape))
    return jnp.concatenate(out, axis=1)


def _head_sum(a, same_head):
    hi = a.astype(BF16)
    lo = (a - hi.astype(F32)).astype(BF16)
    return (jnp.dot(hi, same_head, preferred_element_type=F32)
            + jnp.dot(lo, same_head, preferred_element_type=F32))


def _norm_proj_kernel(x_ref, g_ref, *refs):
    w_refs, out_refs = refs[:len(refs) // 2], refs[len(refs) // 2:]

    def sub_tile(rows):
        hb = _rmsnorm(x_ref[rows, :], g_ref[...]).astype(BF16)
        for w_ref, out_ref in zip(w_refs, out_refs):
            yield
            out_ref[rows, :] = jnp.dot(hb, w_ref[...], preferred_element_type=F32)

    _round_robin(sub_tile(rows) for rows in _sub_tile_rows(x_ref.shape[0]))


def _norm_proj(x, g, weights):
    n = x.shape[0]
    tm = TOKEN_TILE * SUBTILES
    row = lambda c: pl.BlockSpec((tm, c), lambda i: (i, 0))
    return pl.pallas_call(
        _norm_proj_kernel,
        grid=(n // tm,),
        in_specs=[row(D_MODEL), _full((1, D_MODEL))] + [_full(w.shape) for w in weights],
        out_specs=[row(w.shape[1]) for w in weights],
        out_shape=[jax.ShapeDtypeStruct((n, w.shape[1]), F32) for w in weights],
        compiler_params=_params("arbitrary"),
        name="norm_proj",
    )(x, g, *weights)


def _norm_rows_kernel(x_ref, g_ref, o_ref):
    o_ref[...] = _rmsnorm(x_ref[...], g_ref[...])


def _norm_rows(x, g):
    return pl.pallas_call(
        _norm_rows_kernel,
        out_shape=jax.ShapeDtypeStruct(x.shape, F32),
        name="norm_rows",
    )(x, g)


def _prev_proj_kernel(h_ref, w_ref, o_ref):
    o_ref[...] = _bdot(h_ref[...], w_ref[...])


def _prev_proj(h_prev, wb):
    return pl.pallas_call(
        _prev_proj_kernel,
        out_shape=jax.ShapeDtypeStruct((h_prev.shape[0], B_COLS), F32),
        compiler_params=_params(),
        name="prev_proj",
    )(h_prev, wb)


def _rwkv_coefficients(ub, up, mu_ref, wlora_ref, wd0_ref, a0_ref, kk_w_ref, ka_ref, rk_ref,
                       same_head_ref):
    xm = ub + (up - ub) * mu_ref[...]
    r = xm[:, 0:B_WIDTH]
    k = xm[:, B_WIDTH:2 * B_WIDTH]
    v = xm[:, 2 * B_WIDTH:3 * B_WIDTH]
    lo = xm[:, 3 * B_WIDTH:]
    col = lax.broadcasted_iota(jnp.int32, lo.shape, 1)
    act = jnp.where(col < DECAY_LORA, jnp.tanh(lo),
                    jnp.where(col < DECAY_LORA + AAA_LORA, lo, jax.nn.sigmoid(lo)))
    yield
    up_proj = _bdot(act, wlora_ref[...])
    yield
    logw = -DECAY_SCALE * jax.nn.sigmoid(wd0_ref[...] + up_proj[:, 0:B_WIDTH])
    a = jax.nn.sigmoid(a0_ref[...] + up_proj[:, B_WIDTH:2 * B_WIDTH])
    same_head = same_head_ref[...]
    kk = k * kk_w_ref[...]
    norm2 = _head_sum(kk * kk, same_head)
    yield
    kk = kk / jnp.maximum(jnp.sqrt(norm2), 1e-12)
    k2 = k * (1.0 + (a - 1.0) * ka_ref[...])
    g = up_proj[:, 2 * B_WIDTH:]
    rk_sum = _head_sum(r * k2 * rk_ref[...], same_head)
    yield
    return (kk, logw, kk * a, k2, r, v), g, rk_sum * v


def _run_to_end(gen):
    while True:
        try:
            next(gen)
        except StopIteration as done:
            return done.value


def _rwkv_prep_kernel(ub_ref, p0_ref, mu_ref, wlora_ref, wd0_ref, a0_ref, kk_w_ref, ka_ref,
                      rk_ref, same_head_ref, kk_ref, w_ref, b_ref, k_ref, r_ref, v_ref, g_ref,
                      bonus_ref, *, time_stride):
    ub = ub_ref[...]
    up = jnp.concatenate([p0_ref[...], ub[:ub.shape[0] - time_stride, :]], axis=0)
    coefs, g, bonus = _run_to_end(_rwkv_coefficients(
        ub, up, mu_ref, wlora_ref, wd0_ref, a0_ref, kk_w_ref, ka_ref, rk_ref, same_head_ref))
    kk, logw, bco, k2, r, v = coefs
    for ref, val in zip((kk_ref, w_ref, b_ref, k_ref, r_ref, v_ref),
                        (kk, jnp.exp(logw), bco, k2, r, v)):
        ref[...] = val
    g_ref[...] = g
    bonus_ref[...] = bonus


def _rwkv_prep(ub, p0, prm, *, time_stride):
    n = ub.shape[0]
    return pl.pallas_call(
        functools.partial(_rwkv_prep_kernel, time_stride=time_stride),
        out_shape=[jax.ShapeDtypeStruct((n, B_WIDTH), F32)] * 8,
        compiler_params=_params(),
        name="rwkv_prep",
    )(ub, p0, *prm)


CHAIN_TILE = 128
N_COEF = 5
PREP_SEQS_PER_ITER = 2
PREP_KEY_SLICES = 4


def _block_prefix_ones(rows, block):
    t = np.arange(rows)
    return jnp.asarray((t[None, :] <= t[:, None]) & (t[None, :] // block == t[:, None] // block),
                       BF16)


def _rwkv_prep_chain_kernel(x_ref, gn_ref, wb_ref, p0_ref, mu_ref, wlora_ref, wd0_ref, a0_ref,
                            kk_w_ref, ka_ref, rk_ref, same_head_ref, prefix_ref, coef_ref, v_ref,
                            g_ref, bonus_ref, carry_ref, xt_ref, *, batch, parts):
    tile = pl.program_id(0)
    key_slice = pl.program_id(1)
    half_rows = B_HEAD // parts

    @pl.when(key_slice == 0)
    def _():
        def per_seq(b):
            ub = _bdot(_rmsnorm(x_ref[b], gn_ref[...]), wb_ref[...])
            yield
            first = jnp.where(tile == 0, p0_ref[pl.ds(b, 1), :], carry_ref[pl.ds(b, 1), :])
            row_id = lax.broadcasted_iota(jnp.int32, ub.shape, 0)
            up = jnp.where(row_id == 0, first, pltpu.roll(ub, 1, 0))
            carry_ref[pl.ds(b, 1), :] = ub[CHAIN_TILE - 1:CHAIN_TILE, :]
            coefs, g, bonus = yield from _rwkv_coefficients(
                ub, up, mu_ref, wlora_ref, wd0_ref, a0_ref, kk_w_ref, ka_ref, rk_ref,
                same_head_ref)
            g_ref[b] = g
            bonus_ref[b] = bonus
            kk, logw, bco, k2, r, v = coefs
            cum = sum(jnp.dot(prefix_ref[...], part, preferred_element_type=F32)
                      for part in _split3(logw))
            yield
            gamma = jnp.exp(cum)
            inv_gamma = jnp.exp(-cum)
            scaled = (kk * jnp.exp(cum - logw), gamma, bco * inv_gamma, k2 * inv_gamma,
                      r * gamma, v)
            for idx, val in enumerate(scaled):
                xt_ref[idx, b] = val.T
                yield

        def per_group(group, carry):
            _round_robin(per_seq(group * PREP_SEQS_PER_ITER + s)
                         for s in range(PREP_SEQS_PER_ITER))
            return carry

        lax.fori_loop(0, batch // PREP_SEQS_PER_ITER, per_group, 0)
        for i in range(half_rows):
            m = jnp.concatenate(
                [xt_ref[N_COEF, :, pl.ds((p * half_rows + i) * B_HEADS, B_HEADS), :]
                 .reshape(batch * B_HEADS, CHAIN_TILE) for p in range(parts)], axis=0)
            v_ref[:, i, :] = m.T

    keys_per_slice = B_HEAD // PREP_KEY_SLICES
    for which in range(N_COEF):
        for jl in range(keys_per_slice):
            rows = pl.ds(pl.multiple_of((key_slice * keys_per_slice + jl) * B_HEADS, B_HEADS),
                         B_HEADS)
            m = xt_ref[which, :, rows, :].reshape(batch * B_HEADS, CHAIN_TILE)
            coef_ref[which, jl] = jnp.concatenate([m] * parts, axis=0).T


def _rwkv_prep_chain(x, gn, wb, p0, prm, *, batch, seq):
    parts = V7X_LANES // (batch * B_HEADS)
    n_tiles = seq // CHAIN_TILE
    tok = pl.BlockSpec((batch, CHAIN_TILE, B_WIDTH), lambda t, a: (0, t, 0))
    prm = (gn, wb, p0) + tuple(prm) + (_block_prefix_ones(CHAIN_TILE, TIME_BLOCK),)
    return pl.pallas_call(
        functools.partial(_rwkv_prep_chain_kernel, batch=batch, parts=parts),
        grid=(n_tiles, PREP_KEY_SLICES),
        in_specs=[pl.BlockSpec((batch, CHAIN_TILE, D_MODEL), lambda t, a: (0, t, 0))]
        + [_full(p.shape) for p in prm],
        out_specs=[pl.BlockSpec((N_COEF, B_HEAD // PREP_KEY_SLICES, CHAIN_TILE, V7X_LANES),
                                lambda t, a: (0, a, t, 0)),
                   pl.BlockSpec((CHAIN_TILE, B_HEAD // parts, V7X_LANES), lambda t, a: (t, 0, 0)),
                   tok, tok],
        out_shape=[jax.ShapeDtypeStruct((N_COEF, B_HEAD, seq, V7X_LANES), F32),
                   jax.ShapeDtypeStruct((seq, B_HEAD // parts, V7X_LANES), F32),
                   jax.ShapeDtypeStruct((batch, seq, B_WIDTH), F32),
                   jax.ShapeDtypeStruct((batch, seq, B_WIDTH), F32)],
        scratch_shapes=[pltpu.VMEM((batch, B_COLS), F32),
                        pltpu.VMEM((N_COEF + 1, batch, B_WIDTH, CHAIN_TILE), F32)],
        compiler_params=_params("arbitrary", "arbitrary"),
        name="rwkv_prep_chain",
    )(x.reshape(batch, seq, D_MODEL), *prm)


def _rwkv_unpack_kernel(y_ref, o_ref, yt_ref, *, batch, parts):
    half_rows = B_HEAD // parts
    for i in range(half_rows):
        nt = y_ref[:, i, :].T
        for p in range(parts):
            rows = slice(p * batch * B_HEADS, (p + 1) * batch * B_HEADS)
            yt_ref[:, pl.ds((p * half_rows + i) * B_HEADS, B_HEADS), :] = (
                nt[rows].reshape(batch, B_HEADS, CHAIN_TILE))
    for b in range(batch):
        o_ref[b] = yt_ref[b].T


def _rwkv_unpack(y, *, batch, seq):
    parts = V7X_LANES // (batch * B_HEADS)
    out = pl.pallas_call(
        functools.partial(_rwkv_unpack_kernel, batch=batch, parts=parts),
        grid=(seq // CHAIN_TILE,),
        in_specs=[pl.BlockSpec((CHAIN_TILE, B_HEAD // parts, V7X_LANES), lambda t: (t, 0, 0))],
        out_specs=pl.BlockSpec((batch, CHAIN_TILE, B_WIDTH), lambda t: (0, t, 0)),
        out_shape=jax.ShapeDtypeStruct((batch, seq, B_WIDTH), F32),
        scratch_shapes=[pltpu.VMEM((batch, B_WIDTH, CHAIN_TILE), F32)],
        compiler_params=_params("arbitrary"),
        name="rwkv_unpack",
    )(y)
    return out.reshape(batch * seq, B_WIDTH)


def _hgrn_prep_kernel(q_ref, f_ref, lbl_ref, qs_ref, fg_ref, kc_ref):
    logits = lbl_ref[...]
    e = jnp.exp(logits - jnp.max(logits, axis=0, keepdims=True))
    lb = e[0:1, :] / jnp.sum(e, axis=0, keepdims=True)
    fz = f_ref[...]
    fg_ref[...] = lb + (1.0 - lb) * jax.nn.sigmoid(fz)
    kc_ref[...] = (1.0 - lb) * jax.nn.sigmoid(-fz)
    qs_ref[...] = jax.nn.silu(q_ref[...])


def _hgrn_prep(ua, lb_logits):
    n = ua.shape[0]
    tm = TOKEN_TILE
    col = lambda j: pl.BlockSpec((tm, A_WIDTH), lambda i: (i, j))
    return pl.pallas_call(
        _hgrn_prep_kernel,
        grid=(n // tm,),
        in_specs=[col(0), col(1), _full(lb_logits.shape)],
        out_specs=[col(0)] * 3,
        out_shape=[jax.ShapeDtypeStruct((n, A_WIDTH), F32)] * 3,
        compiler_params=_params("arbitrary"),
        name="hgrn_prep",
    )(ua, ua, lb_logits)


HGRN_CHUNK = 64


def _hgrn_tables(chunk):
    levels = chunk.bit_length() - 1
    t = np.arange(chunk)
    u, tt = t[None, :], t[:, None]
    rows = [u <= tt]
    masks = []
    for level in range(levels):
        m = 1 << level
        anchor = (t // (2 * m)) * (2 * m) + m - 1
        right = (t % (2 * m)) >= m
        if m < V7X_SUBLANES:
            rows.append(((u > anchor[:, None]) & (u <= tt) & right[:, None])
                        | ((u > tt) & (u <= anchor[:, None]) & ~right[:, None]))
        masks.append((tt // (2 * m) == u // (2 * m)) & right[:, None] & ~right[None, :])
    return (jnp.asarray(np.concatenate(rows, 0), BF16),
            jnp.asarray(np.stack(masks), F32), levels)


def _hgrn_anchor_sums(cum, level):
    m = 1 << level
    groups = []
    for g in range(cum.shape[0] // V7X_SUBLANES):
        start = g * V7X_SUBLANES
        anchor = (start // (2 * m)) * (2 * m) + m - 1
        block = cum[start:start + V7X_SUBLANES]
        ref_row = cum[anchor:anchor + 1]
        groups.append(block - ref_row if start % (2 * m) >= m else ref_row - block)
    return jnp.concatenate(groups, axis=0)


def _split3(x):
    hi = x.astype(BF16)
    r1 = x - hi.astype(F32)
    mid = r1.astype(BF16)
    lo = (r1 - mid.astype(F32)).astype(BF16)
    return hi, mid, lo


def _dot_nt(a, b):
    return lax.dot_general(a.astype(BF16), b.astype(BF16), (((1,), (1,)), ((), ())),
                           preferred_element_type=F32)


def _dot_tn(a, b):
    return lax.dot_general(a.astype(BF16), b.astype(BF16), (((0,), (0,)), ((), ())),
                           preferred_element_type=F32)


def _hgrn_lower_bound(lbl_ref):
    logits = lbl_ref[...]
    e = jnp.exp(logits - jnp.max(logits, axis=0, keepdims=True))
    return e[0:1, :] / jnp.sum(e, axis=0, keepdims=True)


def _hgrn_chunk_kernel(x_ref, gn_ref, wqfi_ref, lbl_ref, sums_ref, mask_ref, s0_ref, o_ref,
                       s_out_ref, st_ref, qfi_ref, *, chunk, n_chunks, levels):
    n_seqs = st_ref.shape[0]

    @pl.when(pl.program_id(1) == 0)
    def _():
        for s in range(n_seqs):
            for h in range(A_HEADS):
                st_ref[s, h] = s0_ref[s, h].T

    for s in range(n_seqs):
        qfi_ref[s] = _bdot(_rmsnorm(x_ref[s], gn_ref[...]), wqfi_ref[...])
    q_cols, f_cols, i_cols = (lambda hs, c=c: slice(c * A_WIDTH + hs.start, c * A_WIDTH + hs.stop)
                              for c in range(3))

    lb = _hgrn_lower_bound(lbl_ref)
    sums = sums_ref[...]
    seg = lambda r: slice(r * chunk, (r + 1) * chunk)

    def one_head(s, h, rows):
        hs = slice(h * A_HEAD, (h + 1) * A_HEAD)
        fz = qfi_ref[s, rows, f_cols(hs)]
        lbh = lb[:, hs]
        logf = jnp.log(lbh + (1.0 - lbh) * jax.nn.sigmoid(fz))
        kh = (1.0 - lbh) * jax.nn.sigmoid(-fz)
        qh = jax.nn.silu(qfi_ref[s, rows, q_cols(hs)])
        vh = qfi_ref[s, rows, i_cols(hs)]
        yield
        sums_out = sum(jnp.dot(sums, part, preferred_element_type=F32)
                       for part in _split3(logf))
        yield
        cum = sums_out[seg(0)]
        from_start = jnp.exp(cum)
        to_end = jnp.exp(cum[chunk - 1:chunk] - cum)
        att = None
        for level in range(levels):
            if (1 << level) < V7X_SUBLANES:
                split = jnp.exp(sums_out[seg(1 + level)])
            else:
                split = jnp.exp(_hgrn_anchor_sums(cum, level))
            term = mask_ref[level] * _dot_nt(qh * split, kh * split)
            att = term if att is None else att + term
            if level % 2 == 1:
                yield
        st = st_ref[s, h]
        o_ref[s, rows, hs] = (_bdot(att, vh.astype(BF16))
                              + jnp.sum(qh * kh, axis=-1, keepdims=True) * vh
                              + _dot_nt(qh * from_start, st))
        yield
        st_ref[s, h] = st * from_start[chunk - 1:chunk] + _dot_tn(vh, kh * to_end)
        yield

    for c in range(n_chunks):
        _round_robin(one_head(s, h, pl.ds(c * chunk, chunk))
                     for h in range(A_HEADS) for s in range(n_seqs))

    @pl.when(pl.program_id(1) == pl.num_programs(1) - 1)
    def _():
        for s in range(n_seqs):
            for h in range(A_HEADS):
                s_out_ref[s, h] = st_ref[s, h].T


HGRN_SEQS_PER_STEP = 2


def _hgrn_chunked(x, gn, wqfi, lb_logits, s0, *, batch, seq):
    chunk = HGRN_CHUNK
    tile = TOKEN_TILE
    n_seqs = HGRN_SEQS_PER_STEP
    sums, masks, levels = _hgrn_tables(chunk)
    tok = lambda c: pl.BlockSpec((n_seqs, tile, c), lambda b, t: (b, t, 0))
    state_spec = pl.BlockSpec((n_seqs, A_HEADS, A_HEAD, A_HEAD), lambda b, t: (b, 0, 0, 0))
    out, state = pl.pallas_call(
        functools.partial(_hgrn_chunk_kernel, chunk=chunk, n_chunks=tile // chunk, levels=levels),
        grid=(batch // n_seqs, seq // tile),
        in_specs=[tok(D_MODEL), _full(gn.shape), _full(wqfi.shape), _full(lb_logits.shape),
                  _full(sums.shape), _full(masks.shape), state_spec],
        out_specs=[tok(A_WIDTH), state_spec],
        out_shape=[jax.ShapeDtypeStruct((batch, seq, A_WIDTH), F32),
                   jax.ShapeDtypeStruct(s0.shape, F32)],
        scratch_shapes=[pltpu.VMEM((n_seqs, A_HEADS, A_HEAD, A_HEAD), F32),
                        pltpu.VMEM((n_seqs, tile, 3 * A_WIDTH), F32)],
        compiler_params=_params("arbitrary", "arbitrary"),
        name="hgrn_chunked",
    )(x.reshape(batch, seq, D_MODEL), gn, wqfi, lb_logits, sums, masks, s0)
    return out.reshape(batch * seq, A_WIDTH), state


def _bcast_row(ref, t, j):
    return jnp.broadcast_to(ref[t, pl.ds(j, 1), :], (V7X_SUBLANES, V7X_LANES))


def _delta_rule_kernel(kk0_ref, kkn_ref, w_ref, b_ref, k_ref, r_ref, v_ref, s0_ref,
                       y_ref, s_ref, sk_ref, *, nib, nj, tb):
    @pl.when(pl.program_id(1) == 0)
    def _():
        s_ref[...] = s0_ref[...]
        for ib in range(nib):
            acc = None
            for j in range(nj):
                kk0 = jnp.broadcast_to(kk0_ref[pl.ds(j, 1), :], (V7X_SUBLANES, V7X_LANES))
                term = s0_ref[ib, j] * kk0
                acc = term if acc is None else acc + term
            sk_ref[ib] = acc

    def step(t, carry):
        sk = [sk_ref[ib] for ib in range(nib)]
        vv = [v_ref[t, ib] for ib in range(nib)]
        yacc = [None] * nib
        skn = [None] * nib
        for j in range(nj):
            wj = _bcast_row(w_ref, t, j)
            bj = _bcast_row(b_ref, t, j)
            kj = _bcast_row(k_ref, t, j)
            rj = _bcast_row(r_ref, t, j)
            nj_kk = _bcast_row(kkn_ref, t, j)
            for ib in range(nib):
                s = s_ref[ib, j] * wj - sk[ib] * bj + vv[ib] * kj
                s_ref[ib, j] = s
                yt = s * rj
                st = s * nj_kk
                yacc[ib] = yt if yacc[ib] is None else yacc[ib] + yt
                skn[ib] = st if skn[ib] is None else skn[ib] + st
        for ib in range(nib):
            y_ref[t, ib] = yacc[ib]
            sk_ref[ib] = skn[ib]
        return carry

    lax.fori_loop(0, tb, step, 0)


def _delta_rule_chain_kernel(kk_ref, gamma_ref, b_ref, k_ref, r_ref, kk_head_ref, v_ref, s0_ref,
                             y_ref, s_ref, sk_ref, kkx_ref, *, nib, nj, tb):
    full = (V7X_SUBLANES, V7X_LANES)
    row = lambda ref, j, t: jnp.broadcast_to(ref[j, pl.ds(t, 1), :], full)

    @pl.when(pl.program_id(0) == 0)
    def _():
        s_ref[...] = s0_ref[...]
        for ib in range(nib):
            acc = None
            for j in range(nj):
                term = s0_ref[ib, j] * row(kk_ref, j, 0)
                acc = term if acc is None else acc + term
            sk_ref[ib] = acc

    kkx_ref[:, 0:tb, :] = kk_ref[...]
    kkx_ref[:, tb:tb + V7X_SUBLANES, :] = (
        kk_head_ref[...] * gamma_ref[:, tb - 1:tb, :])

    def step(t, carry):
        sk = [sk_ref[ib] for ib in range(nib)]
        vv = [v_ref[t, ib] for ib in range(nib)]
        yacc = [None] * nib
        skn = [None] * nib
        for j in range(nj):
            bj, kj, rj = (row(ref, j, t) for ref in (b_ref, k_ref, r_ref))
            kkn = row(kkx_ref, j, t + 1)
            for ib in range(nib):
                s = s_ref[ib, j] + (vv[ib] * kj - sk[ib] * bj)
                s_ref[ib, j] = s
                yt = s * rj
                st = s * kkn
                yacc[ib] = yt if yacc[ib] is None else yacc[ib] + yt
                skn[ib] = st if skn[ib] is None else skn[ib] + st
        for ib in range(nib):
            y_ref[t, ib] = yacc[ib]
            sk_ref[ib] = skn[ib]
        return carry

    lax.fori_loop(0, tb, step, 0)
    for j in range(nj):
        total = row(gamma_ref, j, tb - 1)
        for ib in range(nib):
            s_ref[ib, j] = s_ref[ib, j] * total


def _delta_rule_chain(coef, v, s0):
    _, nj, t_len, _ = coef.shape
    nib = v.shape[1]
    tb = TIME_BLOCK
    n_blocks = t_len // tb
    head_rows = V7X_SUBLANES
    coef_spec = lambda a: pl.BlockSpec((None, nj, tb, V7X_LANES), lambda t: (a, 0, t, 0))
    head_spec = pl.BlockSpec(
        (None, nj, head_rows, V7X_LANES),
        lambda t: (0, 0, jnp.minimum(t + 1, n_blocks - 1) * (tb // head_rows), 0))
    row_spec = pl.BlockSpec((tb, nib, V7X_SUBLANES, V7X_LANES), lambda t: (t, 0, 0, 0))
    return pl.pallas_call(
        functools.partial(_delta_rule_chain_kernel, nib=nib, nj=nj, tb=tb),
        grid=(n_blocks,),
        in_specs=[coef_spec(a) for a in range(N_COEF)]
        + [head_spec, row_spec, _full(s0.shape)],
        out_specs=[row_spec, _full(s0.shape)],
        out_shape=[jax.ShapeDtypeStruct(v.shape, F32), jax.ShapeDtypeStruct(s0.shape, F32)],
        scratch_shapes=[pltpu.VMEM((nib, V7X_SUBLANES, V7X_LANES), F32),
                        pltpu.VMEM((nj, tb + head_rows, V7X_LANES), F32)],
        compiler_params=_params("arbitrary"),
        name="delta_rule_chain",
    )(coef, coef, coef, coef, coef, coef, v, s0)


def _decay_rule_kernel(w_ref, k_ref, r_ref, v_ref, s0_ref, y_ref, s_ref, st_ref,
                       *, nib, nj, t_len):
    full = (V7X_SUBLANES, V7X_LANES)
    for j in range(nj):
        tile = s0_ref[:, j * V7X_LANES:(j + 1) * V7X_LANES]
        st_ref[j] = tile.T.reshape(nib, V7X_SUBLANES, V7X_LANES)
    for t in range(t_len):
        vv = [v_ref[t, ib] for ib in range(nib)]

        def one_key(j, yacc):
            wj, kj, rj = (jnp.broadcast_to(ref[t, pl.ds(j, 1), :], full)
                          for ref in (w_ref, k_ref, r_ref))
            out = []
            for ib in range(nib):
                s = st_ref[j, ib] * wj + vv[ib] * kj
                st_ref[j, ib] = s
                out.append(yacc[ib] + s * rj)
            return tuple(out)

        yacc = lax.fori_loop(0, nj, one_key, tuple(jnp.zeros(full, F32) for _ in range(nib)))
        for ib in range(nib):
            y_ref[t, ib] = yacc[ib]
    for j in range(nj):
        s_ref[:, j * V7X_LANES:(j + 1) * V7X_LANES] = (
            st_ref[j].reshape(nib * V7X_SUBLANES, V7X_LANES).T)


def _decay_rule(coefs, v, s0):
    t_len, heads, nib = v.shape[0], v.shape[1], v.shape[2]
    nj = coefs[0].shape[2]
    chains = s0.shape[0]
    assert chains == V7X_LANES and nib * V7X_SUBLANES == V7X_LANES
    width = nj * V7X_LANES
    coef_spec = pl.BlockSpec((t_len, None, nj, V7X_LANES), lambda h: (0, h, 0, 0))
    row_spec = pl.BlockSpec((t_len, None, nib, V7X_SUBLANES, V7X_LANES),
                            lambda h: (0, h, 0, 0, 0))
    state_spec = pl.BlockSpec((chains, width), lambda h: (0, h))
    y, state = pl.pallas_call(
        functools.partial(_decay_rule_kernel, nib=nib, nj=nj, t_len=t_len),
        grid=(heads,),
        in_specs=[coef_spec] * 3 + [row_spec, state_spec],
        out_specs=[row_spec, state_spec],
        out_shape=[jax.ShapeDtypeStruct(v.shape, F32),
                   jax.ShapeDtypeStruct((chains, heads * width), F32)],
        scratch_shapes=[pltpu.VMEM((nj, nib, V7X_SUBLANES, V7X_LANES), F32)],
        compiler_params=_params("arbitrary"),
        name="decay_rule",
    )(*coefs, v, s0.reshape(chains, heads * width))
    return y, state.reshape(s0.shape)


def _delta_rule(coefs, v, s0, *, kk0, groups_per_coef):
    t_len, q, nib = v.shape[0], v.shape[1], v.shape[2]
    nj = coefs[0].shape[2]
    tb = min(TIME_BLOCK, t_len)
    coef_spec = pl.BlockSpec((tb, None, nj, V7X_LANES),
                             lambda g, t: (t, g // groups_per_coef, 0, 0))
    row_spec = pl.BlockSpec((tb, None, nib, V7X_SUBLANES, V7X_LANES),
                            lambda g, t: (t, g, 0, 0, 0))
    state_spec = pl.BlockSpec((None, nib, nj, V7X_SUBLANES, V7X_LANES),
                              lambda g, t: (g, 0, 0, 0, 0))
    kk0_spec = pl.BlockSpec((None, nj, V7X_LANES), lambda g, t: (g // groups_per_coef, 0, 0))
    return pl.pallas_call(
        functools.partial(_delta_rule_kernel, nib=nib, nj=nj, tb=tb),
        grid=(q, t_len // tb),
        in_specs=[kk0_spec] + [coef_spec] * 5 + [row_spec, state_spec],
        out_specs=[row_spec, state_spec],
        out_shape=[jax.ShapeDtypeStruct(v.shape, F32), jax.ShapeDtypeStruct(s0.shape, F32)],
        scratch_shapes=[pltpu.VMEM((nib, V7X_SUBLANES, V7X_LANES), F32)],
        compiler_params=_params("arbitrary", "arbitrary"),
        name="delta_rule",
    )(kk0, *coefs, v, s0)


def _mix_kernel(x_ref, oa_ref, y_ref, bonus_ref, gb_ref, gn_ref, wga_ref, wg_ref, hw_ref, lnw_ref,
                lnb_ref, wa_ref, wb_ref, wo_ref, x1_ref):
    def sub_tile(rows):
        x = x_ref[rows, :]
        hb = _rmsnorm(x, gn_ref[...]).astype(BF16)
        gate = jax.nn.sigmoid(jnp.dot(hb, wg_ref[...], preferred_element_type=F32))
        yield
        ga = jnp.dot(hb, wga_ref[...], preferred_element_type=F32)
        oa = oa_ref[rows, :]
        ms = _head_sum_lanes(oa * oa, A_HEAD) * (1.0 / A_HEAD)
        oa = oa * lax.rsqrt(ms + HGRN_NORM_EPS) * hw_ref[...] * jax.nn.silu(ga)
        ya = _bdot(oa, wa_ref[...])
        yield
        y = y_ref[rows, :]
        d = y - _head_sum_rolled(y, B_HEADS) * (1.0 / B_HEAD)
        var = _head_sum_rolled(d * d, B_HEADS) * (1.0 / B_HEAD)
        yn = d * lax.rsqrt(var + GN_EPS) * lnw_ref[...] + lnb_ref[...]
        ob = (yn + bonus_ref[rows, :]) * gb_ref[rows, :]
        yb = _bdot(ob, wb_ref[...])
        yield
        merged = gate[:, 0:D_MODEL] * ya + gate[:, D_MODEL:GATE_COLS] * yb
        x1_ref[rows, :] = x + _bdot(merged, wo_ref[...])

    _round_robin(sub_tile(rows) for rows in _sub_tile_rows(x_ref.shape[0]))


def _mix(x, oa, y, bonus, gb, gn, wga, wg, hw, lnw, lnb, wa, wb, wo):
    n = x.shape[0]
    tm = TOKEN_TILE * SUBTILES
    row = lambda c: pl.BlockSpec((tm, c), lambda i: (i, 0))
    vec = _full((1, A_WIDTH))
    return pl.pallas_call(
        _mix_kernel,
        grid=(n // tm,),
        in_specs=[row(D_MODEL), row(A_WIDTH),
                  row(B_WIDTH), row(B_WIDTH), row(B_WIDTH), _full((1, D_MODEL)),
                  _full((D_MODEL, A_WIDTH)), _full((D_MODEL, GATE_COLS)), vec, vec, vec,
                  _full((A_WIDTH, D_MODEL)), _full((B_WIDTH, D_MODEL)),
                  _full((D_MODEL, D_MODEL))],
        out_specs=row(D_MODEL),
        out_shape=jax.ShapeDtypeStruct((n, D_MODEL), F32),
        compiler_params=_params("arbitrary"),
        name="mix",
    )(x, oa, y, bonus, gb, gn, wga, wg, hw, lnw, lnb, wa, wb, wo)


def _mlp_kernel(x_ref, g_ref, wu_ref, wd_ref, gf_ref, o_ref):
    def sub_tile(rows):
        x1 = x_ref[rows, :]
        hb = _rmsnorm(x1, g_ref[...]).astype(BF16)
        yield
        act = jnp.square(jnp.maximum(jnp.dot(hb, wu_ref[...], preferred_element_type=F32), 0.0))
        yield
        x2 = x1 + _bdot(act, wd_ref[...])
        yield
        o_ref[rows, :] = _rmsnorm(x2, gf_ref[...])

    _round_robin(sub_tile(rows) for rows in _sub_tile_rows(x_ref.shape[0]))


def _mlp(x1, g, wu, wd, gf):
    n = x1.shape[0]
    tm = TOKEN_TILE * SUBTILES
    row = pl.BlockSpec((tm, D_MODEL), lambda i: (i, 0))
    return pl.pallas_call(
        _mlp_kernel,
        grid=(n // tm,),
        in_specs=[row, _full((1, D_MODEL)), _full((D_MODEL, D_FF)), _full((D_FF, D_MODEL)),
                  _full((1, D_MODEL))],
        out_specs=row,
        out_shape=jax.ShapeDtypeStruct((n, D_MODEL), F32),
        compiler_params=_params("arbitrary"),
        name="mlp",
    )(x1, g, wu, wd, gf)


class _SampleLayout:
    def __init__(self, batch, seq, heads, nj, ni, head_minor):
        assert batch == V7X_LANES
        self.b, self.t, self.h, self.nj, self.ni = batch, seq, heads, nj, ni
        self.head_minor = head_minor
        self.groups_per_coef = ni // ROWS_PER_GROUP
        self.nib = ROWS_PER_GROUP // V7X_SUBLANES

    def _to_chain(self, x, width):
        b, t, h = self.b, self.t, self.h
        if self.head_minor:
            return x.reshape(t, b, width, h).transpose(0, 3, 2, 1)
        return x.reshape(t, b, h, width).transpose(0, 2, 3, 1)

    def coef(self, x):
        return self._to_chain(x, self.nj)

    def rows_in(self, v):
        v = self._to_chain(v, self.ni)
        return v.reshape(self.t, self.h * self.groups_per_coef, self.nib, V7X_SUBLANES, V7X_LANES)

    def rows_out(self, y):
        b, t, h, ni = self.b, self.t, self.h, self.ni
        y = y.reshape(t, h, ni, b)
        y = y.transpose(0, 3, 2, 1) if self.head_minor else y.transpose(0, 3, 1, 2)
        return y.reshape(t * b, h * ni)

    def state_in(self, s, rows_last):
        b, h, nj = self.b, self.h, self.nj
        if not rows_last:
            s = s.transpose(0, 1, 3, 2)
        s = s.reshape(b, h, nj, self.groups_per_coef, self.nib, V7X_SUBLANES)
        s = s.transpose(1, 3, 4, 2, 5, 0)
        return s.reshape(h * self.groups_per_coef, self.nib, nj, V7X_SUBLANES, V7X_LANES)

    def state_out(self, s, rows_last):
        b, h, ni, nj = self.b, self.h, self.ni, self.nj
        s = s.reshape(h, self.groups_per_coef, self.nib, nj, V7X_SUBLANES, b)
        s = s.transpose(5, 0, 3, 1, 2, 4).reshape(b, h, nj, ni)
        return s if rows_last else s.transpose(0, 1, 3, 2)


def _prompt_state_to_chain(s, parts):
    b, h, ni, nj = s.shape
    nib = ni // parts // V7X_SUBLANES
    s = s.reshape(b, h, parts, nib, V7X_SUBLANES, nj).transpose(3, 5, 4, 2, 0, 1)
    return s.reshape(nib, nj, V7X_SUBLANES, V7X_LANES)


def _prompt_state_from_chain(s, batch, heads, parts):
    nib, nj = s.shape[0], s.shape[1]
    s = s.reshape(nib, nj, V7X_SUBLANES, parts, batch, heads).transpose(4, 5, 3, 0, 2, 1)
    return s.reshape(batch, heads, parts * nib * V7X_SUBLANES, nj)


def _same_head_matrix(width, heads):
    idx = np.arange(width) % heads
    return jnp.asarray(idx[:, None] == idx[None, :], BF16)


def _trunk(x, shift0, state_a, state_b, wts, *, time_major):
    batch, seq, _ = x.shape
    n = batch * seq
    if time_major:
        x2 = x.transpose(1, 0, 2).reshape(n, D_MODEL)
        time_stride = batch
    else:
        x2 = x.reshape(n, D_MODEL)
        time_stride = 1

    if time_major:
        ua, ub = _norm_proj(x2, wts["norm_mix_g"], (wts["w_in_qfi"], wts["w_in_b"]))
    p0 = _prev_proj(shift0, wts["w_in_b"])
    prep_params = (wts["mu_shift"], wts["w_lora"], wts["w_decay0"], wts["a0"], wts["k_k"],
                   wts["k_a"], wts["r_k"], wts["same_head_b"])

    if time_major:
        kk, w, bco, k2, r, v, gb, bonus = _rwkv_prep(ub, p0, prep_params,
                                                     time_stride=time_stride)
        lay_b = _SampleLayout(batch, seq, B_HEADS, B_HEAD, B_HEAD, head_minor=True)
        kk_c = lay_b.coef(kk)
        kk_next = jnp.concatenate([kk_c[1:], jnp.zeros_like(kk_c[:1])], axis=0)
        y_c, sb_c = _delta_rule(
            (kk_next, lay_b.coef(w), lay_b.coef(bco), lay_b.coef(k2), lay_b.coef(r)),
            lay_b.rows_in(v), lay_b.state_in(state_b, rows_last=False), kk0=kk_c[0],
            groups_per_coef=lay_b.groups_per_coef)
        y_b = lay_b.rows_out(y_c)
        new_wkv = lay_b.state_out(sb_c, rows_last=False)
    else:
        parts = V7X_LANES // (batch * B_HEADS)
        coef, v_c, gb, bonus = _rwkv_prep_chain(x2, wts["norm_mix_g"], wts["w_in_b"], p0,
                                                prep_params, batch=batch, seq=seq)
        gb = gb.reshape(n, B_WIDTH)
        bonus = bonus.reshape(n, B_WIDTH)
        nib = B_HEAD // parts // V7X_SUBLANES
        y_c, sb_c = _delta_rule_chain(
            coef, v_c.reshape(seq, nib, V7X_SUBLANES, V7X_LANES),
            _prompt_state_to_chain(state_b, parts))
        o_a, new_hgrn = _hgrn_chunked(x2, wts["norm_mix_g"], wts["w_in_qfi"], wts["lb_logits"],
                                      state_a, batch=batch, seq=seq)
        y_b = _rwkv_unpack(y_c.reshape(seq, nib * V7X_SUBLANES, V7X_LANES), batch=batch, seq=seq)
        new_wkv = _prompt_state_from_chain(sb_c, batch, B_HEADS, parts)

    if time_major:
        qs, fg, kc = _hgrn_prep(ua, wts["lb_logits"])
        lay_a = _SampleLayout(batch, seq, A_HEADS, A_HEAD, A_HEAD, head_minor=False)
        vi = lay_a.coef(ua[:, 2 * A_WIDTH:3 * A_WIDTH])
        o_c, new_hgrn = _decay_rule(
            (lay_a.coef(fg), lay_a.coef(kc), lay_a.coef(qs)),
            vi.reshape(seq, A_HEADS, A_HEAD // V7X_SUBLANES, V7X_SUBLANES, V7X_LANES), state_a)
        o_a = lay_a.rows_out(o_c)

    x1 = _mix(x2, o_a, y_b, bonus, gb, wts["norm_mix_g"], wts["w_in_ga"], wts["w_in_g"],
              wts["hgrn_norm_w"], wts["ln_x_w"],
              wts["ln_x_b"], wts["w_a_out"], wts["w_b_out"], wts["w_out"])
    y = _mlp(x1, wts["norm_mlp_g"], wts["w_up"], wts["w_down"], wts["norm_final_g"])
    if time_major:
        y = y.reshape(seq, batch, D_MODEL).transpose(1, 0, 2)
    else:
        y = y.reshape(batch, seq, D_MODEL)
    new_shift = _norm_rows(x[:, -1, :], wts["norm_mix_g"])
    return y, new_hgrn[None], new_wkv[None], new_shift[None]


def kernel(x_prompt, x_sample, state_hgrn, state_wkv, state_shift, norm_mix_g, w_in, mu_shift,
           w_decay0, w_decay_up, a0, w_aaa_up, w_gate_up, k_k, k_a, r_k, ln_x_w, ln_x_b,
           lb_logits, hgrn_norm_w, w_a_out, w_b_out, w_out, norm_mlp_g, w_up, w_down,
           norm_final_g):
    assert w_in.shape[0] == 1, "single-layer stack"
    def pcols(a):
        lead = a.shape[:-1]
        return a.reshape(*lead, B_HEADS, B_HEAD).swapaxes(-1, -2).reshape(*lead, B_WIDTH)

    prows = lambda a: a.reshape(B_HEADS, B_HEAD, -1).swapaxes(0, 1).reshape(B_WIDTH, -1)
    w_in0 = w_in[0]
    w_in_b = w_in0[:, A_COLS:A_COLS + B_COLS]
    w_in_b = jnp.concatenate(
        [pcols(w_in_b[:, s * B_WIDTH:(s + 1) * B_WIDTH]) for s in range(3)]
        + [w_in_b[:, 3 * B_WIDTH:]], axis=1)
    mu = mu_shift[0]
    mu = jnp.concatenate([pcols(mu[s * B_WIDTH:(s + 1) * B_WIDTH]) for s in range(3)]
                         + [mu[3 * B_WIDTH:]])
    w_lora = jnp.zeros((LORA_COLS, 3 * B_WIDTH), F32)
    w_lora = w_lora.at[:DECAY_LORA, :B_WIDTH].set(pcols(w_decay_up[0]))
    w_lora = w_lora.at[DECAY_LORA:DECAY_LORA + AAA_LORA, B_WIDTH:2 * B_WIDTH].set(
        pcols(w_aaa_up[0]))
    w_lora = w_lora.at[DECAY_LORA + AAA_LORA:, 2 * B_WIDTH:].set(pcols(w_gate_up[0]))
    row = lambda a: a.reshape(1, -1).astype(F32)
    prow = lambda a: row(pcols(a.reshape(-1)))
    wts = {
        "norm_mix_g": row(norm_mix_g[0]),
        "w_in_qfi": w_in0[:, :3 * A_WIDTH].astype(BF16),
        "w_in_ga": w_in0[:, 3 * A_WIDTH:A_COLS].astype(BF16),
        "w_in_b": w_in_b.astype(BF16),
        "w_in_g": w_in0[:, A_COLS + B_COLS:].astype(BF16),
        "mu_shift": row(mu),
        "w_lora": w_lora.astype(BF16),
        "w_decay0": prow(w_decay0[0]),
        "a0": prow(a0[0]),
        "k_k": prow(k_k[0]),
        "k_a": prow(k_a[0]),
        "r_k": prow(r_k[0]),
        "ln_x_w": prow(ln_x_w[0]),
        "ln_x_b": prow(ln_x_b[0]),
        "lb_logits": lb_logits.astype(F32),
        "hgrn_norm_w": row(hgrn_norm_w[0]),
        "w_a_out": w_a_out[0].astype(BF16),
        "w_b_out": prows(w_b_out[0]).astype(BF16),
        "w_out": w_out[0].astype(BF16),
        "norm_mlp_g": row(norm_mlp_g[0]),
        "w_up": w_up[0].astype(BF16),
        "w_down": w_down[0].astype(BF16),
        "norm_final_g": row(norm_final_g),
        "same_head_b": _same_head_matrix(B_WIDTH, B_HEADS),
    }
    bp = x_prompt.shape[0]
    y_p, hgrn_p, wkv_p, shift_p = _trunk(
        x_prompt, jnp.zeros((bp, D_MODEL), F32),
        jnp.zeros((bp, A_HEADS, A_HEAD, A_HEAD), F32),
        jnp.zeros((bp, B_HEADS, B_HEAD, B_HEAD), F32), wts, time_major=False)
    y_s, hgrn_s, wkv_s, shift_s = _trunk(
        x_sample, state_shift[0], state_hgrn[0], state_wkv[0], wts, time_major=True)
    return (y_p, y_s, hgrn_p, wkv_p, shift_p, hgrn_s, wkv_s, shift_s)
```

```python
import functools

import jax
import jax.numpy as jnp
import numpy as np
from jax import lax
from jax.experimental import pallas as pl
from jax.experimental.pallas import tpu as pltpu

F32 = jnp.float32
BF16 = jnp.bfloat16

D_MODEL = 1024
A_WIDTH = 512
A_HEADS = 4
A_HEAD = 128
B_WIDTH = 512
B_HEADS = 8
B_HEAD = 64
DECAY_LORA = 64
AAA_LORA = 64
GATE_LORA = 128
LORA_COLS = DECAY_LORA + AAA_LORA + GATE_LORA
D_FF = 4 * D_MODEL
A_COLS = 4 * A_WIDTH
B_COLS = 3 * B_WIDTH + LORA_COLS
GATE_COLS = 2 * D_MODEL
NORM_EPS = 1e-6
HGRN_NORM_EPS = 1e-5
GN_EPS = 64e-5
DECAY_SCALE = 0.6065306597126334

V7X_LANES = 128
V7X_SUBLANES = 8
V7X_VMEM_LIMIT_BYTES = 56 * 1024 * 1024

TOKEN_TILE = 256
SUBTILES = 2
TIME_BLOCK = 64
ROWS_PER_GROUP = 64


def _params(*semantics):
    return pltpu.CompilerParams(dimension_semantics=semantics,
                                vmem_limit_bytes=V7X_VMEM_LIMIT_BYTES)


def _full(shape):
    return pl.BlockSpec(shape, lambda *_: (0,) * len(shape))


def _sub_tile_rows(rows):
    sub = rows // SUBTILES
    return [pl.ds(part * sub, sub) for part in range(SUBTILES)]


def _round_robin(streams):
    streams = list(streams)
    while streams:
        streams = [g for g in streams if next(g, StopIteration) is not StopIteration]


def _rmsnorm(x, g):
    return x * lax.rsqrt(jnp.mean(x * x, axis=-1, keepdims=True) + NORM_EPS) * g


def _bdot(a, w):
    return jnp.dot(a.astype(BF16), w, preferred_element_type=F32)


def _head_sum_rolled(x, heads):
    tiles = x.shape[1] // V7X_LANES
    t = x[:, 0:V7X_LANES]
    for c in range(1, tiles):
        t = t + x[:, c * V7X_LANES:(c + 1) * V7X_LANES]
    shift = heads
    while shift < V7X_LANES:
        t = t + pltpu.roll(t, shift, 1)
        shift *= 2
    return jnp.concatenate([t] * tiles, axis=1)


def _head_sum_lanes(x, head):
    out = []
    for h in range(x.shape[1] // head):
        seg = x[:, h * head:(h + 1) * head]
        out.append(jnp.broadcast_to(jnp.sum(seg, axis=-1, keepdims=True), seg.shape))
    return jnp.concatenate(out, axis=1)


def _head_sum(a, same_head):
    hi = a.astype(BF16)
    lo = (a - hi.astype(F32)).astype(BF16)
    return (jnp.dot(hi, same_head, preferred_element_type=F32)
            + jnp.dot(lo, same_head, preferred_element_type=F32))


def _norm_proj_kernel(x_ref, g_ref, *refs):
    w_refs, out_refs = refs[:len(refs) // 2], refs[len(refs) // 2:]

    def sub_tile(rows):
        hb = _rmsnorm(x_ref[rows, :], g_ref[...]).astype(BF16)
        for w_ref, out_ref in zip(w_refs, out_refs):
            yield
            out_ref[rows, :] = jnp.dot(hb, w_ref[...], preferred_element_type=F32)

    _round_robin(sub_tile(rows) for rows in _sub_tile_rows(x_ref.shape[0]))


def _norm_proj(x, g, weights):
    n = x.shape[0]
    tm = TOKEN_TILE * SUBTILES
    row = lambda c: pl.BlockSpec((tm, c), lambda i: (i, 0))
    return pl.pallas_call(
        _norm_proj_kernel,
        grid=(n // tm,),
        in_specs=[row(D_MODEL), _full((1, D_MODEL))] + [_full(w.shape) for w in weights],
        out_specs=[row(w.shape[1]) for w in weights],
        out_shape=[jax.ShapeDtypeStruct((n, w.shape[1]), F32) for w in weights],
        compiler_params=_params("arbitrary"),
        name="norm_proj",
    )(x, g, *weights)


def _norm_rows_kernel(x_ref, g_ref, o_ref):
    o_ref[...] = _rmsnorm(x_ref[...], g_ref[...])


def _norm_rows(x, g):
    return pl.pallas_call(
        _norm_rows_kernel,
        out_shape=jax.ShapeDtypeStruct(x.shape, F32),
        name="norm_rows",
    )(x, g)


def _prev_proj_kernel(h_ref, w_ref, o_ref):
    o_ref[...] = _bdot(h_ref[...], w_ref[...])


def _prev_proj(h_prev, wb):
    return pl.pallas_call(
        _prev_proj_kernel,
        out_shape=jax.ShapeDtypeStruct((h_prev.shape[0], B_COLS), F32),
        compiler_params=_params(),
        name="prev_proj",
    )(h_prev, wb)


def _rwkv_coefficients(ub, up, mu_ref, wlora_ref, wd0_ref, a0_ref, kk_w_ref, ka_ref, rk_ref,
                       same_head_ref):
    xm = ub + (up - ub) * mu_ref[...]
    r = xm[:, 0:B_WIDTH]
    k = xm[:, B_WIDTH:2 * B_WIDTH]
    v = xm[:, 2 * B_WIDTH:3 * B_WIDTH]
    lo = xm[:, 3 * B_WIDTH:]
    col = lax.broadcasted_iota(jnp.int32, lo.shape, 1)
    act = jnp.where(col < DECAY_LORA, jnp.tanh(lo),
                    jnp.where(col < DECAY_LORA + AAA_LORA, lo, jax.nn.sigmoid(lo)))
    yield
    up_proj = _bdot(act, wlora_ref[...])
    yield
    logw = -DECAY_SCALE * jax.nn.sigmoid(wd0_ref[...] + up_proj[:, 0:B_WIDTH])
    a = jax.nn.sigmoid(a0_ref[...] + up_proj[:, B_WIDTH:2 * B_WIDTH])
    same_head = same_head_ref[...]
    kk = k * kk_w_ref[...]
    norm2 = _head_sum(kk * kk, same_head)
    yield
    kk = kk / jnp.maximum(jnp.sqrt(norm2), 1e-12)
    k2 = k * (1.0 + (a - 1.0) * ka_ref[...])
    g = up_proj[:, 2 * B_WIDTH:]
    rk_sum = _head_sum(r * k2 * rk_ref[...], same_head)
    yield
    return (kk, logw, kk * a, k2, r, v), g, rk_sum * v


def _run_to_end(gen):
    while True:
        try:
            next(gen)
        except StopIteration as done:
            return done.value


def _rwkv_prep_kernel(ub_ref, p0_ref, mu_ref, wlora_ref, wd0_ref, a0_ref, kk_w_ref, ka_ref,
                      rk_ref, same_head_ref, kk_ref, w_ref, b_ref, k_ref, r_ref, v_ref, g_ref,
                      bonus_ref, *, time_stride):
    ub = ub_ref[...]
    up = jnp.concatenate([p0_ref[...], ub[:ub.shape[0] - time_stride, :]], axis=0)
    coefs, g, bonus = _run_to_end(_rwkv_coefficients(
        ub, up, mu_ref, wlora_ref, wd0_ref, a0_ref, kk_w_ref, ka_ref, rk_ref, same_head_ref))
    kk, logw, bco, k2, r, v = coefs
    for ref, val in zip((kk_ref, w_ref, b_ref, k_ref, r_ref, v_ref),
                        (kk, jnp.exp(logw), bco, k2, r, v)):
        ref[...] = val
    g_ref[...] = g
    bonus_ref[...] = bonus


def _rwkv_prep(ub, p0, prm, *, time_stride):
    n = ub.shape[0]
    return pl.pallas_call(
        functools.partial(_rwkv_prep_kernel, time_stride=time_stride),
        out_shape=[jax.ShapeDtypeStruct((n, B_WIDTH), F32)] * 8,
        compiler_params=_params(),
        name="rwkv_prep",
    )(ub, p0, *prm)


CHAIN_TILE = 128
N_COEF = 5
PREP_SEQS_PER_ITER = 2
PREP_KEY_SLICES = 4


def _block_prefix_ones(rows, block):
    t = np.arange(rows)
    return jnp.asarray((t[None, :] <= t[:, None]) & (t[None, :] // block == t[:, None] // block),
                       BF16)


def _rwkv_prep_chain_kernel(x_ref, gn_ref, wb_ref, p0_ref, mu_ref, wlora_ref, wd0_ref, a0_ref,
                            kk_w_ref, ka_ref, rk_ref, same_head_ref, prefix_ref, coef_ref, v_ref,
                            g_ref, bonus_ref, carry_ref, xt_ref, *, batch, parts):
    tile = pl.program_id(0)
    key_slice = pl.program_id(1)
    half_rows = B_HEAD // parts

    @pl.when(key_slice == 0)
    def _():
        def per_seq(b):
            ub = _bdot(_rmsnorm(x_ref[b], gn_ref[...]), wb_ref[...])
            yield
            first = jnp.where(tile == 0, p0_ref[pl.ds(b, 1), :], carry_ref[pl.ds(b, 1), :])
            row_id = lax.broadcasted_iota(jnp.int32, ub.shape, 0)
            up = jnp.where(row_id == 0, first, pltpu.roll(ub, 1, 0))
            carry_ref[pl.ds(b, 1), :] = ub[CHAIN_TILE - 1:CHAIN_TILE, :]
            coefs, g, bonus = yield from _rwkv_coefficients(
                ub, up, mu_ref, wlora_ref, wd0_ref, a0_ref, kk_w_ref, ka_ref, rk_ref,
                same_head_ref)
            g_ref[b] = g
            bonus_ref[b] = bonus
            kk, logw, bco, k2, r, v = coefs
            cum = sum(jnp.dot(prefix_ref[...], part, preferred_element_type=F32)
                      for part in _split3(logw))
            yield
            gamma = jnp.exp(cum)
            inv_gamma = jnp.exp(-cum)
            scaled = (kk * jnp.exp(cum - logw), gamma, bco * inv_gamma, k2 * inv_gamma,
                      r * gamma, v)
            for idx, val in enumerate(scaled):
                xt_ref[idx, b] = val.T
                yield

        def per_group(group, carry):
            _round_robin(per_seq(group * PREP_SEQS_PER_ITER + s)
                         for s in range(PREP_SEQS_PER_ITER))
            return carry

        lax.fori_loop(0, batch // PREP_SEQS_PER_ITER, per_group, 0)
        for i in range(half_rows):
            m = jnp.concatenate(
                [xt_ref[N_COEF, :, pl.ds((p * half_rows + i) * B_HEADS, B_HEADS), :]
                 .reshape(batch * B_HEADS, CHAIN_TILE) for p in range(parts)], axis=0)
            v_ref[:, i, :] = m.T

    keys_per_slice = B_HEAD // PREP_KEY_SLICES
    for which in range(N_COEF):
        for jl in range(keys_per_slice):
            rows = pl.ds(pl.multiple_of((key_slice * keys_per_slice + jl) * B_HEADS, B_HEADS),
                         B_HEADS)
            m = xt_ref[which, :, rows, :].reshape(batch * B_HEADS, CHAIN_TILE)
            coef_ref[which, jl] = jnp.concatenate([m] * parts, axis=0).T


def _rwkv_prep_chain(x, gn, wb, p0, prm, *, batch, seq):
    parts = V7X_LANES // (batch * B_HEADS)
    n_tiles = seq // CHAIN_TILE
    tok = pl.BlockSpec((batch, CHAIN_TILE, B_WIDTH), lambda t, a: (0, t, 0))
    prm = (gn, wb, p0) + tuple(prm) + (_block_prefix_ones(CHAIN_TILE, TIME_BLOCK),)
    return pl.pallas_call(
        functools.partial(_rwkv_prep_chain_kernel, batch=batch, parts=parts),
        grid=(n_tiles, PREP_KEY_SLICES),
        in_specs=[pl.BlockSpec((batch, CHAIN_TILE, D_MODEL), lambda t, a: (0, t, 0))]
        + [_full(p.shape) for p in prm],
        out_specs=[pl.BlockSpec((N_COEF, B_HEAD // PREP_KEY_SLICES, CHAIN_TILE, V7X_LANES),
                                lambda t, a: (0, a, t, 0)),
                   pl.BlockSpec((CHAIN_TILE, B_HEAD // parts, V7X_LANES), lambda t, a: (t, 0, 0)),
                   tok, tok],
        out_shape=[jax.ShapeDtypeStruct((N_COEF, B_HEAD, seq, V7X_LANES), F32),
                   jax.ShapeDtypeStruct((seq, B_HEAD // parts, V7X_LANES), F32),
                   jax.ShapeDtypeStruct((batch, seq, B_WIDTH), F32),
                   jax.ShapeDtypeStruct((batch, seq, B_WIDTH), F32)],
        scratch_shapes=[pltpu.VMEM((batch, B_COLS), F32),
                        pltpu.VMEM((N_COEF + 1, batch, B_WIDTH, CHAIN_TILE), F32)],
        compiler_params=_params("arbitrary", "arbitrary"),
        name="rwkv_prep_chain",
    )(x.reshape(batch, seq, D_MODEL), *prm)


def _rwkv_unpack_kernel(y_ref, o_ref, yt_ref, *, batch, parts):
    half_rows = B_HEAD // parts
    for i in range(half_rows):
        nt = y_ref[:, i, :].T
        for p in range(parts):
            rows = slice(p * batch * B_HEADS, (p + 1) * batch * B_HEADS)
            yt_ref[:, pl.ds((p * half_rows + i) * B_HEADS, B_HEADS), :] = (
                nt[rows].reshape(batch, B_HEADS, CHAIN_TILE))
    for b in range(batch):
        o_ref[b] = yt_ref[b].T


def _rwkv_unpack(y, *, batch, seq):
    parts = V7X_LANES // (batch * B_HEADS)
    out = pl.pallas_call(
        functools.partial(_rwkv_unpack_kernel, batch=batch, parts=parts),
        grid=(seq // CHAIN_TILE,),
        in_specs=[pl.BlockSpec((CHAIN_TILE, B_HEAD // parts, V7X_LANES), lambda t: (t, 0, 0))],
        out_specs=pl.BlockSpec((batch, CHAIN_TILE, B_WIDTH), lambda t: (0, t, 0)),
        out_shape=jax.ShapeDtypeStruct((batch, seq, B_WIDTH), F32),
        scratch_shapes=[pltpu.VMEM((batch, B_WIDTH, CHAIN_TILE), F32)],
        compiler_params=_params("arbitrary"),
        name="rwkv_unpack",
    )(y)
    return out.reshape(batch * seq, B_WIDTH)


def _hgrn_prep_kernel(q_ref, f_ref, lbl_ref, qs_ref, fg_ref, kc_ref):
    logits = lbl_ref[...]
    e = jnp.exp(logits - jnp.max(logits, axis=0, keepdims=True))
    lb = e[0:1, :] / jnp.sum(e, axis=0, keepdims=True)
    fz = f_ref[...]
    fg_ref[...] = lb + (1.0 - lb) * jax.nn.sigmoid(fz)
    kc_ref[...] = (1.0 - lb) * jax.nn.sigmoid(-fz)
    qs_ref[...] = jax.nn.silu(q_ref[...])


def _hgrn_prep(ua, lb_logits):
    n = ua.shape[0]
    tm = TOKEN_TILE
    col = lambda j: pl.BlockSpec((tm, A_WIDTH), lambda i: (i, j))
    return pl.pallas_call(
        _hgrn_prep_kernel,
        grid=(n // tm,),
        in_specs=[col(0), col(1), _full(lb_logits.shape)],
        out_specs=[col(0)] * 3,
        out_shape=[jax.ShapeDtypeStruct((n, A_WIDTH), F32)] * 3,
        compiler_params=_params("arbitrary"),
        name="hgrn_prep",
    )(ua, ua, lb_logits)


HGRN_CHUNK = 64


def _hgrn_tables(chunk):
    levels = chunk.bit_length() - 1
    t = np.arange(chunk)
    u, tt = t[None, :], t[:, None]
    rows = [u <= tt]
    masks = []
    for level in range(levels):
        m = 1 << level
        anchor = (t // (2 * m)) * (2 * m) + m - 1
        right = (t % (2 * m)) >= m
        if m < V7X_SUBLANES:
            rows.append(((u > anchor[:, None]) & (u <= tt) & right[:, None])
                        | ((u > tt) & (u <= anchor[:, None]) & ~right[:, None]))
        masks.append((tt // (2 * m) == u // (2 * m)) & right[:, None] & ~right[None, :])
    return (jnp.asarray(np.concatenate(rows, 0), BF16),
            jnp.asarray(np.stack(masks), F32), levels)


def _hgrn_anchor_sums(cum, level):
    m = 1 << level
    groups = []
    for g in range(cum.shape[0] // V7X_SUBLANES):
        start = g * V7X_SUBLANES
        anchor = (start // (2 * m)) * (2 * m) + m - 1
        block = cum[start:start + V7X_SUBLANES]
        ref_row = cum[anchor:anchor + 1]
        groups.append(block - ref_row if start % (2 * m) >= m else ref_row - block)
    return jnp.concatenate(groups, axis=0)


def _split3(x):
    hi = x.astype(BF16)
    r1 = x - hi.astype(F32)
    mid = r1.astype(BF16)
    lo = (r1 - mid.astype(F32)).astype(BF16)
    return hi, mid, lo


def _dot_nt(a, b):
    return lax.dot_general(a.astype(BF16), b.astype(BF16), (((1,), (1,)), ((), ())),
                           preferred_element_type=F32)


def _dot_tn(a, b):
    return lax.dot_general(a.astype(BF16), b.astype(BF16), (((0,), (0,)), ((), ())),
                           preferred_element_type=F32)


def _hgrn_lower_bound(lbl_ref):
    logits = lbl_ref[...]
    e = jnp.exp(logits - jnp.max(logits, axis=0, keepdims=True))
    return e[0:1, :] / jnp.sum(e, axis=0, keepdims=True)


def _hgrn_chunk_kernel(q_ref, f_ref, i_ref, lbl_ref, sums_ref, mask_ref, s0_ref, o_ref,
                       s_out_ref, st_ref, *, chunk, n_chunks, levels):
    n_seqs = st_ref.shape[0]

    @pl.when(pl.program_id(1) == 0)
    def _():
        for s in range(n_seqs):
            for h in range(A_HEADS):
                st_ref[s, h] = s0_ref[s, h].T

    lb = _hgrn_lower_bound(lbl_ref)
    sums = sums_ref[...]
    seg = lambda r: slice(r * chunk, (r + 1) * chunk)

    def one_head(s, h, rows):
        hs = slice(h * A_HEAD, (h + 1) * A_HEAD)
        fz = f_ref[s, rows, hs]
        lbh = lb[:, hs]
        logf = jnp.log(lbh + (1.0 - lbh) * jax.nn.sigmoid(fz))
        kh = (1.0 - lbh) * jax.nn.sigmoid(-fz)
        qh = jax.nn.silu(q_ref[s, rows, hs])
        vh = i_ref[s, rows, hs]
        yield
        sums_out = sum(jnp.dot(sums, part, preferred_element_type=F32)
                       for part in _split3(logf))
        yield
        cum = sums_out[seg(0)]
        from_start = jnp.exp(cum)
        to_end = jnp.exp(cum[chunk - 1:chunk] - cum)
        att = None
        for level in range(levels):
            if (1 << level) < V7X_SUBLANES:
                split = jnp.exp(sums_out[seg(1 + level)])
            else:
                split = jnp.exp(_hgrn_anchor_sums(cum, level))
            term = mask_ref[level] * _dot_nt(qh * split, kh * split)
            att = term if att is None else att + term
            if level % 2 == 1:
                yield
        st = st_ref[s, h]
        o_ref[s, rows, hs] = (_bdot(att, vh.astype(BF16))
                              + jnp.sum(qh * kh, axis=-1, keepdims=True) * vh
                              + _dot_nt(qh * from_start, st))
        yield
        st_ref[s, h] = st * from_start[chunk - 1:chunk] + _dot_tn(vh, kh * to_end)
        yield

    for c in range(n_chunks):
        _round_robin(one_head(s, h, pl.ds(c * chunk, chunk))
                     for h in range(A_HEADS) for s in range(n_seqs))

    @pl.when(pl.program_id(1) == pl.num_programs(1) - 1)
    def _():
        for s in range(n_seqs):
            for h in range(A_HEADS):
                s_out_ref[s, h] = st_ref[s, h].T


HGRN_SEQS_PER_STEP = 4


def _hgrn_chunked(ua, lb_logits, s0, *, batch, seq):
    chunk = HGRN_CHUNK
    tile = TOKEN_TILE
    n_seqs = HGRN_SEQS_PER_STEP
    sums, masks, levels = _hgrn_tables(chunk)
    col = lambda j: pl.BlockSpec((n_seqs, tile, A_WIDTH), lambda b, t: (b, t, j))
    state_spec = pl.BlockSpec((n_seqs, A_HEADS, A_HEAD, A_HEAD), lambda b, t: (b, 0, 0, 0))
    out, state = pl.pallas_call(
        functools.partial(_hgrn_chunk_kernel, chunk=chunk, n_chunks=tile // chunk, levels=levels),
        grid=(batch // n_seqs, seq // tile),
        in_specs=[col(0), col(1), col(2), _full(lb_logits.shape), _full(sums.shape),
                  _full(masks.shape), state_spec],
        out_specs=[col(0), state_spec],
        out_shape=[jax.ShapeDtypeStruct((batch, seq, A_WIDTH), F32),
                   jax.ShapeDtypeStruct(s0.shape, F32)],
        scratch_shapes=[pltpu.VMEM((n_seqs, A_HEADS, A_HEAD, A_HEAD), F32)],
        compiler_params=_params("arbitrary", "arbitrary"),
        name="hgrn_chunked",
    )(*[ua.reshape(batch, seq, A_COLS)] * 3, lb_logits, sums, masks, s0)
    return out.reshape(batch * seq, A_WIDTH), state


def _bcast_row(ref, t, j):
    return jnp.broadcast_to(ref[t, pl.ds(j, 1), :], (V7X_SUBLANES, V7X_LANES))


def _delta_rule_kernel(kk0_ref, kkn_ref, w_ref, b_ref, k_ref, r_ref, v_ref, s0_ref,
                       y_ref, s_ref, sk_ref, *, nib, nj, tb):
    @pl.when(pl.program_id(1) == 0)
    def _():
        s_ref[...] = s0_ref[...]
        for ib in range(nib):
            acc = None
            for j in range(nj):
                kk0 = jnp.broadcast_to(kk0_ref[pl.ds(j, 1), :], (V7X_SUBLANES, V7X_LANES))
                term = s0_ref[ib, j] * kk0
                acc = term if acc is None else acc + term
            sk_ref[ib] = acc

    def step(t, carry):
        sk = [sk_ref[ib] for ib in range(nib)]
        vv = [v_ref[t, ib] for ib in range(nib)]
        yacc = [None] * nib
        skn = [None] * nib
        for j in range(nj):
            wj = _bcast_row(w_ref, t, j)
            bj = _bcast_row(b_ref, t, j)
            kj = _bcast_row(k_ref, t, j)
            rj = _bcast_row(r_ref, t, j)
            nj_kk = _bcast_row(kkn_ref, t, j)
            for ib in range(nib):
                s = s_ref[ib, j] * wj - sk[ib] * bj + vv[ib] * kj
                s_ref[ib, j] = s
                yt = s * rj
                st = s * nj_kk
                yacc[ib] = yt if yacc[ib] is None else yacc[ib] + yt
                skn[ib] = st if skn[ib] is None else skn[ib] + st
        for ib in range(nib):
            y_ref[t, ib] = yacc[ib]
            sk_ref[ib] = skn[ib]
        return carry

    lax.fori_loop(0, tb, step, 0)


def _delta_rule_chain_kernel(kk_ref, gamma_ref, b_ref, k_ref, r_ref, kk_head_ref, v_ref, s0_ref,
                             y_ref, s_ref, sk_ref, *, nib, nj, tb):
    full = (V7X_SUBLANES, V7X_LANES)
    row = lambda ref, j, t: jnp.broadcast_to(ref[j, pl.ds(t, 1), :], full)

    @pl.when(pl.program_id(0) == 0)
    def _():
        s_ref[...] = s0_ref[...]
        for ib in range(nib):
            acc = None
            for j in range(nj):
                term = s0_ref[ib, j] * row(kk_ref, j, 0)
                acc = term if acc is None else acc + term
            sk_ref[ib] = acc

    def step(t, last):
        sk = [sk_ref[ib] for ib in range(nib)]
        vv = [v_ref[t, ib] for ib in range(nib)]
        yacc = [None] * nib
        skn = [None] * nib
        for j in range(nj):
            bj, kj, rj = (row(ref, j, t) for ref in (b_ref, k_ref, r_ref))
            if last:
                total = row(gamma_ref, j, t)
                kkn = row(kk_head_ref, j, 0) * total
            else:
                kkn = row(kk_ref, j, t + 1)
            for ib in range(nib):
                s = s_ref[ib, j] + (vv[ib] * kj - sk[ib] * bj)
                s_ref[ib, j] = s * total if last else s
                yt = s * rj
                st = s * kkn
                yacc[ib] = yt if yacc[ib] is None else yacc[ib] + yt
                skn[ib] = st if skn[ib] is None else skn[ib] + st
        for ib in range(nib):
            y_ref[t, ib] = yacc[ib]
            sk_ref[ib] = skn[ib]

    def body(t, carry):
        step(t, last=False)
        return carry

    lax.fori_loop(0, tb - 1, body, 0)
    step(tb - 1, last=True)


def _delta_rule_chain(coef, v, s0):
    _, nj, t_len, _ = coef.shape
    nib = v.shape[1]
    tb = TIME_BLOCK
    n_blocks = t_len // tb
    head_rows = V7X_SUBLANES
    coef_spec = lambda a: pl.BlockSpec((None, nj, tb, V7X_LANES), lambda t: (a, 0, t, 0))
    head_spec = pl.BlockSpec(
        (None, nj, head_rows, V7X_LANES),
        lambda t: (0, 0, jnp.minimum(t + 1, n_blocks - 1) * (tb // head_rows), 0))
    row_spec = pl.BlockSpec((tb, nib, V7X_SUBLANES, V7X_LANES), lambda t: (t, 0, 0, 0))
    return pl.pallas_call(
        functools.partial(_delta_rule_chain_kernel, nib=nib, nj=nj, tb=tb),
        grid=(n_blocks,),
        in_specs=[coef_spec(a) for a in range(N_COEF)]
        + [head_spec, row_spec, _full(s0.shape)],
        out_specs=[row_spec, _full(s0.shape)],
        out_shape=[jax.ShapeDtypeStruct(v.shape, F32), jax.ShapeDtypeStruct(s0.shape, F32)],
        scratch_shapes=[pltpu.VMEM((nib, V7X_SUBLANES, V7X_LANES), F32)],
        compiler_params=_params("arbitrary"),
        name="delta_rule_chain",
    )(coef, coef, coef, coef, coef, coef, v, s0)


def _decay_rule_kernel(w_ref, k_ref, r_ref, v_ref, s0_ref, y_ref, s_ref, st_ref,
                       *, nib, nj, t_len):
    full = (V7X_SUBLANES, V7X_LANES)
    for j in range(nj):
        tile = s0_ref[:, j * V7X_LANES:(j + 1) * V7X_LANES]
        st_ref[j] = tile.T.reshape(nib, V7X_SUBLANES, V7X_LANES)
    for t in range(t_len):
        vv = [v_ref[t, ib] for ib in range(nib)]

        def one_key(j, yacc):
            wj, kj, rj = (jnp.broadcast_to(ref[t, pl.ds(j, 1), :], full)
                          for ref in (w_ref, k_ref, r_ref))
            out = []
            for ib in range(nib):
                s = st_ref[j, ib] * wj + vv[ib] * kj
                st_ref[j, ib] = s
                out.append(yacc[ib] + s * rj)
            return tuple(out)

        yacc = lax.fori_loop(0, nj, one_key, tuple(jnp.zeros(full, F32) for _ in range(nib)))
        for ib in range(nib):
            y_ref[t, ib] = yacc[ib]
    for j in range(nj):
        s_ref[:, j * V7X_LANES:(j + 1) * V7X_LANES] = (
            st_ref[j].reshape(nib * V7X_SUBLANES, V7X_LANES).T)


def _decay_rule(coefs, v, s0):
    t_len, heads, nib = v.shape[0], v.shape[1], v.shape[2]
    nj = coefs[0].shape[2]
    chains = s0.shape[0]
    assert chains == V7X_LANES and nib * V7X_SUBLANES == V7X_LANES
    width = nj * V7X_LANES
    coef_spec = pl.BlockSpec((t_len, None, nj, V7X_LANES), lambda h: (0, h, 0, 0))
    row_spec = pl.BlockSpec((t_len, None, nib, V7X_SUBLANES, V7X_LANES),
                            lambda h: (0, h, 0, 0, 0))
    state_spec = pl.BlockSpec((chains, width), lambda h: (0, h))
    y, state = pl.pallas_call(
        functools.partial(_decay_rule_kernel, nib=nib, nj=nj, t_len=t_len),
        grid=(heads,),
        in_specs=[coef_spec] * 3 + [row_spec, state_spec],
        out_specs=[row_spec, state_spec],
        out_shape=[jax.ShapeDtypeStruct(v.shape, F32),
                   jax.ShapeDtypeStruct((chains, heads * width), F32)],
        scratch_shapes=[pltpu.VMEM((nj, nib, V7X_SUBLANES, V7X_LANES), F32)],
        compiler_params=_params("arbitrary"),
        name="decay_rule",
    )(*coefs, v, s0.reshape(chains, heads * width))
    return y, state.reshape(s0.shape)


def _delta_rule(coefs, v, s0, *, kk0, groups_per_coef):
    t_len, q, nib = v.shape[0], v.shape[1], v.shape[2]
    nj = coefs[0].shape[2]
    tb = min(TIME_BLOCK, t_len)
    coef_spec = pl.BlockSpec((tb, None, nj, V7X_LANES),
                             lambda g, t: (t, g // groups_per_coef, 0, 0))
    row_spec = pl.BlockSpec((tb, None, nib, V7X_SUBLANES, V7X_LANES),
                            lambda g, t: (t, g, 0, 0, 0))
    state_spec = pl.BlockSpec((None, nib, nj, V7X_SUBLANES, V7X_LANES),
                              lambda g, t: (g, 0, 0, 0, 0))
    kk0_spec = pl.BlockSpec((None, nj, V7X_LANES), lambda g, t: (g // groups_per_coef, 0, 0))
    return pl.pallas_call(
        functools.partial(_delta_rule_kernel, nib=nib, nj=nj, tb=tb),
        grid=(q, t_len // tb),
        in_specs=[kk0_spec] + [coef_spec] * 5 + [row_spec, state_spec],
        out_specs=[row_spec, state_spec],
        out_shape=[jax.ShapeDtypeStruct(v.shape, F32), jax.ShapeDtypeStruct(s0.shape, F32)],
        scratch_shapes=[pltpu.VMEM((nib, V7X_SUBLANES, V7X_LANES), F32)],
        compiler_params=_params("arbitrary", "arbitrary"),
        name="delta_rule",
    )(kk0, *coefs, v, s0)


def _mix_kernel(x_ref, oa_ref, ga_ref, y_ref, bonus_ref, gb_ref, gn_ref, wg_ref, hw_ref, lnw_ref,
                lnb_ref, wa_ref, wb_ref, wo_ref, x1_ref):
    def sub_tile(rows):
        x = x_ref[rows, :]
        gate = jax.nn.sigmoid(_bdot(_rmsnorm(x, gn_ref[...]), wg_ref[...]))
        yield
        oa = oa_ref[rows, :]
        ms = _head_sum_lanes(oa * oa, A_HEAD) * (1.0 / A_HEAD)
        oa = oa * lax.rsqrt(ms + HGRN_NORM_EPS) * hw_ref[...] * jax.nn.silu(ga_ref[rows, :])
        ya = _bdot(oa, wa_ref[...])
        yield
        y = y_ref[rows, :]
        d = y - _head_sum_rolled(y, B_HEADS) * (1.0 / B_HEAD)
        var = _head_sum_rolled(d * d, B_HEADS) * (1.0 / B_HEAD)
        yn = d * lax.rsqrt(var + GN_EPS) * lnw_ref[...] + lnb_ref[...]
        ob = (yn + bonus_ref[rows, :]) * gb_ref[rows, :]
        yb = _bdot(ob, wb_ref[...])
        yield
        merged = gate[:, 0:D_MODEL] * ya + gate[:, D_MODEL:GATE_COLS] * yb
        x1_ref[rows, :] = x + _bdot(merged, wo_ref[...])

    _round_robin(sub_tile(rows) for rows in _sub_tile_rows(x_ref.shape[0]))


def _mix(x, oa, ua, y, bonus, gb, gn, wg, hw, lnw, lnb, wa, wb, wo):
    n = x.shape[0]
    tm = TOKEN_TILE * SUBTILES
    row = lambda c: pl.BlockSpec((tm, c), lambda i: (i, 0))
    vec = _full((1, A_WIDTH))
    return pl.pallas_call(
        _mix_kernel,
        grid=(n // tm,),
        in_specs=[row(D_MODEL), row(A_WIDTH), pl.BlockSpec((tm, A_WIDTH), lambda i: (i, 3)),
                  row(B_WIDTH), row(B_WIDTH), row(B_WIDTH), _full((1, D_MODEL)),
                  _full((D_MODEL, GATE_COLS)), vec, vec, vec,
                  _full((A_WIDTH, D_MODEL)), _full((B_WIDTH, D_MODEL)),
                  _full((D_MODEL, D_MODEL))],
        out_specs=row(D_MODEL),
        out_shape=jax.ShapeDtypeStruct((n, D_MODEL), F32),
        compiler_params=_params("arbitrary"),
        name="mix",
    )(x, oa, ua, y, bonus, gb, gn, wg, hw, lnw, lnb, wa, wb, wo)


def _mlp_kernel(x_ref, g_ref, wu_ref, wd_ref, gf_ref, o_ref):
    def sub_tile(rows):
        x1 = x_ref[rows, :]
        hb = _rmsnorm(x1, g_ref[...]).astype(BF16)
        yield
        act = jnp.square(jnp.maximum(jnp.dot(hb, wu_ref[...], preferred_element_type=F32), 0.0))
        yield
        x2 = x1 + _bdot(act, wd_ref[...])
        yield
        o_ref[rows, :] = _rmsnorm(x2, gf_ref[...])

    _round_robin(sub_tile(rows) for rows in _sub_tile_rows(x_ref.shape[0]))


def _mlp(x1, g, wu, wd, gf):
    n = x1.shape[0]
    tm = TOKEN_TILE * SUBTILES
    row = pl.BlockSpec((tm, D_MODEL), lambda i: (i, 0))
    return pl.pallas_call(
        _mlp_kernel,
        grid=(n // tm,),
        in_specs=[row, _full((1, D_MODEL)), _full((D_MODEL, D_FF)), _full((D_FF, D_MODEL)),
                  _full((1, D_MODEL))],
        out_specs=row,
        out_shape=jax.ShapeDtypeStruct((n, D_MODEL), F32),
        compiler_params=_params("arbitrary"),
        name="mlp",
    )(x1, g, wu, wd, gf)


class _SampleLayout:
    def __init__(self, batch, seq, heads, nj, ni, head_minor):
        assert batch == V7X_LANES
        self.b, self.t, self.h, self.nj, self.ni = batch, seq, heads, nj, ni
        self.head_minor = head_minor
        self.groups_per_coef = ni // ROWS_PER_GROUP
        self.nib = ROWS_PER_GROUP // V7X_SUBLANES

    def _to_chain(self, x, width):
        b, t, h = self.b, self.t, self.h
        if self.head_minor:
            return x.reshape(t, b, width, h).transpose(0, 3, 2, 1)
        return x.reshape(t, b, h, width).transpose(0, 2, 3, 1)

    def coef(self, x):
        return self._to_chain(x, self.nj)

    def rows_in(self, v):
        v = self._to_chain(v, self.ni)
        return v.reshape(self.t, self.h * self.groups_per_coef, self.nib, V7X_SUBLANES, V7X_LANES)

    def rows_out(self, y):
        b, t, h, ni = self.b, self.t, self.h, self.ni
        y = y.reshape(t, h, ni, b)
        y = y.transpose(0, 3, 2, 1) if self.head_minor else y.transpose(0, 3, 1, 2)
        return y.reshape(t * b, h * ni)

    def state_in(self, s, rows_last):
        b, h, nj = self.b, self.h, self.nj
        if not rows_last:
            s = s.transpose(0, 1, 3, 2)
        s = s.reshape(b, h, nj, self.groups_per_coef, self.nib, V7X_SUBLANES)
        s = s.transpose(1, 3, 4, 2, 5, 0)
        return s.reshape(h * self.groups_per_coef, self.nib, nj, V7X_SUBLANES, V7X_LANES)

    def state_out(self, s, rows_last):
        b, h, ni, nj = self.b, self.h, self.ni, self.nj
        s = s.reshape(h, self.groups_per_coef, self.nib, nj, V7X_SUBLANES, b)
        s = s.transpose(5, 0, 3, 1, 2, 4).reshape(b, h, nj, ni)
        return s if rows_last else s.transpose(0, 1, 3, 2)


def _prompt_state_to_chain(s, parts):
    b, h, ni, nj = s.shape
    nib = ni // parts // V7X_SUBLANES
    s = s.reshape(b, h, parts, nib, V7X_SUBLANES, nj).transpose(3, 5, 4, 2, 0, 1)
    return s.reshape(nib, nj, V7X_SUBLANES, V7X_LANES)


def _prompt_state_from_chain(s, batch, heads, parts):
    nib, nj = s.shape[0], s.shape[1]
    s = s.reshape(nib, nj, V7X_SUBLANES, parts, batch, heads).transpose(4, 5, 3, 0, 2, 1)
    return s.reshape(batch, heads, parts * nib * V7X_SUBLANES, nj)


def _same_head_matrix(width, heads):
    idx = np.arange(width) % heads
    return jnp.asarray(idx[:, None] == idx[None, :], BF16)


def _trunk(x, shift0, state_a, state_b, wts, *, time_major):
    batch, seq, _ = x.shape
    n = batch * seq
    if time_major:
        x2 = x.transpose(1, 0, 2).reshape(n, D_MODEL)
        time_stride = batch
    else:
        x2 = x.reshape(n, D_MODEL)
        time_stride = 1

    if time_major:
        ua, ub = _norm_proj(x2, wts["norm_mix_g"], (wts["w_in_a"], wts["w_in_b"]))
    else:
        ua, = _norm_proj(x2, wts["norm_mix_g"], (wts["w_in_a"],))
    p0 = _prev_proj(shift0, wts["w_in_b"])
    prep_params = (wts["mu_shift"], wts["w_lora"], wts["w_decay0"], wts["a0"], wts["k_k"],
                   wts["k_a"], wts["r_k"], wts["same_head_b"])

    if time_major:
        kk, w, bco, k2, r, v, gb, bonus = _rwkv_prep(ub, p0, prep_params,
                                                     time_stride=time_stride)
        lay_b = _SampleLayout(batch, seq, B_HEADS, B_HEAD, B_HEAD, head_minor=True)
        kk_c = lay_b.coef(kk)
        kk_next = jnp.concatenate([kk_c[1:], jnp.zeros_like(kk_c[:1])], axis=0)
        y_c, sb_c = _delta_rule(
            (kk_next, lay_b.coef(w), lay_b.coef(bco), lay_b.coef(k2), lay_b.coef(r)),
            lay_b.rows_in(v), lay_b.state_in(state_b, rows_last=False), kk0=kk_c[0],
            groups_per_coef=lay_b.groups_per_coef)
        y_b = lay_b.rows_out(y_c)
        new_wkv = lay_b.state_out(sb_c, rows_last=False)
    else:
        parts = V7X_LANES // (batch * B_HEADS)
        coef, v_c, gb, bonus = _rwkv_prep_chain(x2, wts["norm_mix_g"], wts["w_in_b"], p0,
                                                prep_params, batch=batch, seq=seq)
        gb = gb.reshape(n, B_WIDTH)
        bonus = bonus.reshape(n, B_WIDTH)
        nib = B_HEAD // parts // V7X_SUBLANES
        y_c, sb_c = _delta_rule_chain(
            coef, v_c.reshape(seq, nib, V7X_SUBLANES, V7X_LANES),
            _prompt_state_to_chain(state_b, parts))
        o_a, new_hgrn = _hgrn_chunked(ua, wts["lb_logits"], state_a, batch=batch, seq=seq)
        y_b = _rwkv_unpack(y_c.reshape(seq, nib * V7X_SUBLANES, V7X_LANES), batch=batch, seq=seq)
        new_wkv = _prompt_state_from_chain(sb_c, batch, B_HEADS, parts)

    if time_major:
        qs, fg, kc = _hgrn_prep(ua, wts["lb_logits"])
        lay_a = _SampleLayout(batch, seq, A_HEADS, A_HEAD, A_HEAD, head_minor=False)
        vi = lay_a.coef(ua[:, 2 * A_WIDTH:3 * A_WIDTH])
        o_c, new_hgrn = _decay_rule(
            (lay_a.coef(fg), lay_a.coef(kc), lay_a.coef(qs)),
            vi.reshape(seq, A_HEADS, A_HEAD // V7X_SUBLANES, V7X_SUBLANES, V7X_LANES), state_a)
        o_a = lay_a.rows_out(o_c)

    x1 = _mix(x2, o_a, ua, y_b, bonus, gb, wts["norm_mix_g"], wts["w_in_g"],
              wts["hgrn_norm_w"], wts["ln_x_w"],
              wts["ln_x_b"], wts["w_a_out"], wts["w_b_out"], wts["w_out"])
    y = _mlp(x1, wts["norm_mlp_g"], wts["w_up"], wts["w_down"], wts["norm_final_g"])
    if time_major:
        y = y.reshape(seq, batch, D_MODEL).transpose(1, 0, 2)
    else:
        y = y.reshape(batch, seq, D_MODEL)
    new_shift = _norm_rows(x[:, -1, :], wts["norm_mix_g"])
    return y, new_hgrn[None], new_wkv[None], new_shift[None]


def kernel(x_prompt, x_sample, state_hgrn, state_wkv, state_shift, norm_mix_g, w_in, mu_shift,
           w_decay0, w_decay_up, a0, w_aaa_up, w_gate_up, k_k, k_a, r_k, ln_x_w, ln_x_b,
           lb_logits, hgrn_norm_w, w_a_out, w_b_out, w_out, norm_mlp_g, w_up, w_down,
           norm_final_g):
    assert w_in.shape[0] == 1, "single-layer stack"
    def pcols(a):
        lead = a.shape[:-1]
        return a.reshape(*lead, B_HEADS, B_HEAD).swapaxes(-1, -2).reshape(*lead, B_WIDTH)

    prows = lambda a: a.reshape(B_HEADS, B_HEAD, -1).swapaxes(0, 1).reshape(B_WIDTH, -1)
    w_in0 = w_in[0]
    w_in_b = w_in0[:, A_COLS:A_COLS + B_COLS]
    w_in_b = jnp.concatenate(
        [pcols(w_in_b[:, s * B_WIDTH:(s + 1) * B_WIDTH]) for s in range(3)]
        + [w_in_b[:, 3 * B_WIDTH:]], axis=1)
    mu = mu_shift[0]
    mu = jnp.concatenate([pcols(mu[s * B_WIDTH:(s + 1) * B_WIDTH]) for s in range(3)]
                         + [mu[3 * B_WIDTH:]])
    w_lora = jnp.zeros((LORA_COLS, 3 * B_WIDTH), F32)
    w_lora = w_lora.at[:DECAY_LORA, :B_WIDTH].set(pcols(w_decay_up[0]))
    w_lora = w_lora.at[DECAY_LORA:DECAY_LORA + AAA_LORA, B_WIDTH:2 * B_WIDTH].set(
        pcols(w_aaa_up[0]))
    w_lora = w_lora.at[DECAY_LORA + AAA_LORA:, 2 * B_WIDTH:].set(pcols(w_gate_up[0]))
    row = lambda a: a.reshape(1, -1).astype(F32)
    prow = lambda a: row(pcols(a.reshape(-1)))
    wts = {
        "norm_mix_g": row(norm_mix_g[0]),
        "w_in_a": w_in0[:, :A_COLS].astype(BF16),
        "w_in_b": w_in_b.astype(BF16),
        "w_in_g": w_in0[:, A_COLS + B_COLS:].astype(BF16),
        "mu_shift": row(mu),
        "w_lora": w_lora.astype(BF16),
        "w_decay0": prow(w_decay0[0]),
        "a0": prow(a0[0]),
        "k_k": prow(k_k[0]),
        "k_a": prow(k_a[0]),
        "r_k": prow(r_k[0]),
        "ln_x_w": prow(ln_x_w[0]),
        "ln_x_b": prow(ln_x_b[0]),
        "lb_logits": lb_logits.astype(F32),
        "hgrn_norm_w": row(hgrn_norm_w[0]),
        "w_a_out": w_a_out[0].astype(BF16),
        "w_b_out": prows(w_b_out[0]).astype(BF16),
        "w_out": w_out[0].astype(BF16),
        "norm_mlp_g": row(norm_mlp_g[0]),
        "w_up": w_up[0].astype(BF16),
        "w_down": w_down[0].astype(BF16),
        "norm_final_g": row(norm_final_g),
        "same_head_b": _same_head_matrix(B_WIDTH, B_HEADS),
    }
    bp = x_prompt.shape[0]
    y_p, hgrn_p, wkv_p, shift_p = _trunk(
        x_prompt, jnp.zeros((bp, D_MODEL), F32),
        jnp.zeros((bp, A_HEADS, A_HEAD, A_HEAD), F32),
        jnp.zeros((bp, B_HEADS, B_HEAD, B_HEAD), F32), wts, time_major=False)
    y_s, hgrn_s, wkv_s, shift_s = _trunk(
        x_sample, state_shift[0], state_hgrn[0], state_wkv[0], wts, time_major=True)
    return (y_p, y_s, hgrn_p, wkv_p, shift_p, hgrn_s, wkv_s, shift_s)
```

```python
import functools

import jax
import jax.numpy as jnp
import numpy as np
from jax import lax
from jax.experimental import pallas as pl
from jax.experimental.pallas import tpu as pltpu

F32 = jnp.float32
BF16 = jnp.bfloat16

D_MODEL = 1024
A_WIDTH = 512
A_HEADS = 4
A_HEAD = 128
B_WIDTH = 512
B_HEADS = 8
B_HEAD = 64
DECAY_LORA = 64
AAA_LORA = 64
GATE_LORA = 128
LORA_COLS = DECAY_LORA + AAA_LORA + GATE_LORA
D_FF = 4 * D_MODEL
A_COLS = 4 * A_WIDTH
B_COLS = 3 * B_WIDTH + LORA_COLS
GATE_COLS = 2 * D_MODEL
NORM_EPS = 1e-6
HGRN_NORM_EPS = 1e-5
GN_EPS = 64e-5
DECAY_SCALE = 0.6065306597126334

V7X_LANES = 128
V7X_SUBLANES = 8
V7X_VMEM_LIMIT_BYTES = 56 * 1024 * 1024

TOKEN_TILE = 256
SUBTILES = 2
TIME_BLOCK = 64
ROWS_PER_GROUP = 64


def _params(*semantics):
    return pltpu.CompilerParams(dimension_semantics=semantics,
                                vmem_limit_bytes=V7X_VMEM_LIMIT_BYTES)


def _full(shape):
    return pl.BlockSpec(shape, lambda *_: (0,) * len(shape))


def _sub_tile_rows(rows):
    sub = rows // SUBTILES
    return [pl.ds(part * sub, sub) for part in range(SUBTILES)]


def _round_robin(streams):
    streams = list(streams)
    while streams:
        streams = [g for g in streams if next(g, StopIteration) is not StopIteration]


def _rmsnorm(x, g):
    return x * lax.rsqrt(jnp.mean(x * x, axis=-1, keepdims=True) + NORM_EPS) * g


def _bdot(a, w):
    return jnp.dot(a.astype(BF16), w, preferred_element_type=F32)


def _head_sum_rolled(x, heads):
    tiles = x.shape[1] // V7X_LANES
    t = x[:, 0:V7X_LANES]
    for c in range(1, tiles):
        t = t + x[:, c * V7X_LANES:(c + 1) * V7X_LANES]
    shift = heads
    while shift < V7X_LANES:
        t = t + pltpu.roll(t, shift, 1)
        shift *= 2
    return jnp.concatenate([t] * tiles, axis=1)


def _head_sum_lanes(x, head):
    out = []
    for h in range(x.shape[1] // head):
        seg = x[:, h * head:(h + 1) * head]
        out.append(jnp.broadcast_to(jnp.sum(seg, axis=-1, keepdims=True), seg.shape))
    return jnp.concatenate(out, axis=1)


def _head_sum(a, same_head):
    hi = a.astype(BF16)
    lo = (a - hi.astype(F32)).astype(BF16)
    return (jnp.dot(hi, same_head, preferred_element_type=F32)
            + jnp.dot(lo, same_head, preferred_element_type=F32))


def _norm_proj_kernel(x_ref, g_ref, *refs):
    w_refs, out_refs = refs[:len(refs) // 2], refs[len(refs) // 2:]

    def sub_tile(rows):
        hb = _rmsnorm(x_ref[rows, :], g_ref[...]).astype(BF16)
        for w_ref, out_ref in zip(w_refs, out_refs):
            yield
            out_ref[rows, :] = jnp.dot(hb, w_ref[...], preferred_element_type=F32)

    _round_robin(sub_tile(rows) for rows in _sub_tile_rows(x_ref.shape[0]))


def _norm_proj(x, g, weights):
    n = x.shape[0]
    tm = TOKEN_TILE * SUBTILES
    row = lambda c: pl.BlockSpec((tm, c), lambda i: (i, 0))
    return pl.pallas_call(
        _norm_proj_kernel,
        grid=(n // tm,),
        in_specs=[row(D_MODEL), _full((1, D_MODEL))] + [_full(w.shape) for w in weights],
        out_specs=[row(w.shape[1]) for w in weights],
        out_shape=[jax.ShapeDtypeStruct((n, w.shape[1]), F32) for w in weights],
        compiler_params=_params("arbitrary"),
        name="norm_proj",
    )(x, g, *weights)


def _norm_rows_kernel(x_ref, g_ref, o_ref):
    o_ref[...] = _rmsnorm(x_ref[...], g_ref[...])


def _norm_rows(x, g):
    return pl.pallas_call(
        _norm_rows_kernel,
        out_shape=jax.ShapeDtypeStruct(x.shape, F32),
        name="norm_rows",
    )(x, g)


def _prev_proj_kernel(h_ref, w_ref, o_ref):
    o_ref[...] = _bdot(h_ref[...], w_ref[...])


def _prev_proj(h_prev, wb):
    return pl.pallas_call(
        _prev_proj_kernel,
        out_shape=jax.ShapeDtypeStruct((h_prev.shape[0], B_COLS), F32),
        compiler_params=_params(),
        name="prev_proj",
    )(h_prev, wb)


def _rwkv_coefficients(ub, up, mu_ref, wlora_ref, wd0_ref, a0_ref, kk_w_ref, ka_ref, rk_ref,
                       same_head_ref):
    xm = ub + (up - ub) * mu_ref[...]
    r = xm[:, 0:B_WIDTH]
    k = xm[:, B_WIDTH:2 * B_WIDTH]
    v = xm[:, 2 * B_WIDTH:3 * B_WIDTH]
    lo = xm[:, 3 * B_WIDTH:]
    col = lax.broadcasted_iota(jnp.int32, lo.shape, 1)
    act = jnp.where(col < DECAY_LORA, jnp.tanh(lo),
                    jnp.where(col < DECAY_LORA + AAA_LORA, lo, jax.nn.sigmoid(lo)))
    yield
    up_proj = _bdot(act, wlora_ref[...])
    yield
    logw = -DECAY_SCALE * jax.nn.sigmoid(wd0_ref[...] + up_proj[:, 0:B_WIDTH])
    a = jax.nn.sigmoid(a0_ref[...] + up_proj[:, B_WIDTH:2 * B_WIDTH])
    same_head = same_head_ref[...]
    kk = k * kk_w_ref[...]
    norm2 = _head_sum(kk * kk, same_head)
    yield
    kk = kk / jnp.maximum(jnp.sqrt(norm2), 1e-12)
    k2 = k * (1.0 + (a - 1.0) * ka_ref[...])
    g = up_proj[:, 2 * B_WIDTH:]
    rk_sum = _head_sum(r * k2 * rk_ref[...], same_head)
    yield
    return (kk, logw, kk * a, k2, r, v), g, rk_sum * v


def _run_to_end(gen):
    while True:
        try:
            next(gen)
        except StopIteration as done:
            return done.value


def _rwkv_prep_kernel(ub_ref, p0_ref, mu_ref, wlora_ref, wd0_ref, a0_ref, kk_w_ref, ka_ref,
                      rk_ref, same_head_ref, kk_ref, w_ref, b_ref, k_ref, r_ref, v_ref, g_ref,
                      bonus_ref, *, time_stride):
    ub = ub_ref[...]
    up = jnp.concatenate([p0_ref[...], ub[:ub.shape[0] - time_stride, :]], axis=0)
    coefs, g, bonus = _run_to_end(_rwkv_coefficients(
        ub, up, mu_ref, wlora_ref, wd0_ref, a0_ref, kk_w_ref, ka_ref, rk_ref, same_head_ref))
    kk, logw, bco, k2, r, v = coefs
    for ref, val in zip((kk_ref, w_ref, b_ref, k_ref, r_ref, v_ref),
                        (kk, jnp.exp(logw), bco, k2, r, v)):
        ref[...] = val
    g_ref[...] = g
    bonus_ref[...] = bonus


def _rwkv_prep(ub, p0, prm, *, time_stride):
    n = ub.shape[0]
    return pl.pallas_call(
        functools.partial(_rwkv_prep_kernel, time_stride=time_stride),
        out_shape=[jax.ShapeDtypeStruct((n, B_WIDTH), F32)] * 8,
        compiler_params=_params(),
        name="rwkv_prep",
    )(ub, p0, *prm)


CHAIN_TILE = 128
N_COEF = 5
PREP_SEQS_PER_ITER = 4
PREP_KEY_SLICES = 4


def _block_prefix_ones(rows, block):
    t = np.arange(rows)
    return jnp.asarray((t[None, :] <= t[:, None]) & (t[None, :] // block == t[:, None] // block),
                       BF16)


def _rwkv_prep_chain_kernel(x_ref, gn_ref, wb_ref, p0_ref, mu_ref, wlora_ref, wd0_ref, a0_ref,
                            kk_w_ref, ka_ref, rk_ref, same_head_ref, prefix_ref, coef_ref, v_ref,
                            g_ref, bonus_ref, carry_ref, xt_ref, *, batch, parts):
    tile = pl.program_id(0)
    key_slice = pl.program_id(1)
    half_rows = B_HEAD // parts

    @pl.when(key_slice == 0)
    def _():
        def per_seq(b):
            ub = _bdot(_rmsnorm(x_ref[b], gn_ref[...]), wb_ref[...])
            yield
            first = jnp.where(tile == 0, p0_ref[pl.ds(b, 1), :], carry_ref[pl.ds(b, 1), :])
            row_id = lax.broadcasted_iota(jnp.int32, ub.shape, 0)
            up = jnp.where(row_id == 0, first, pltpu.roll(ub, 1, 0))
            carry_ref[pl.ds(b, 1), :] = ub[CHAIN_TILE - 1:CHAIN_TILE, :]
            coefs, g, bonus = yield from _rwkv_coefficients(
                ub, up, mu_ref, wlora_ref, wd0_ref, a0_ref, kk_w_ref, ka_ref, rk_ref,
                same_head_ref)
            g_ref[b] = g
            bonus_ref[b] = bonus
            kk, logw, bco, k2, r, v = coefs
            cum = sum(jnp.dot(prefix_ref[...], part, preferred_element_type=F32)
                      for part in _split3(logw))
            yield
            gamma = jnp.exp(cum)
            inv_gamma = jnp.exp(-cum)
            scaled = (kk * jnp.exp(cum - logw), gamma, bco * inv_gamma, k2 * inv_gamma,
                      r * gamma, v)
            for idx, val in enumerate(scaled):
                xt_ref[idx, b] = val.T
                yield

        def per_group(group, carry):
            _round_robin(per_seq(group * PREP_SEQS_PER_ITER + s)
                         for s in range(PREP_SEQS_PER_ITER))
            return carry

        lax.fori_loop(0, batch // PREP_SEQS_PER_ITER, per_group, 0)
        for i in range(half_rows):
            m = jnp.concatenate(
                [xt_ref[N_COEF, :, pl.ds((p * half_rows + i) * B_HEADS, B_HEADS), :]
                 .reshape(batch * B_HEADS, CHAIN_TILE) for p in range(parts)], axis=0)
            v_ref[:, i, :] = m.T

    keys_per_slice = B_HEAD // PREP_KEY_SLICES
    for which in range(N_COEF):
        for jl in range(keys_per_slice):
            rows = pl.ds(pl.multiple_of((key_slice * keys_per_slice + jl) * B_HEADS, B_HEADS),
                         B_HEADS)
            m = xt_ref[which, :, rows, :].reshape(batch * B_HEADS, CHAIN_TILE)
            coef_ref[which, jl] = jnp.concatenate([m] * parts, axis=0).T


def _rwkv_prep_chain(x, gn, wb, p0, prm, *, batch, seq):
    parts = V7X_LANES // (batch * B_HEADS)
    n_tiles = seq // CHAIN_TILE
    tok = pl.BlockSpec((batch, CHAIN_TILE, B_WIDTH), lambda t, a: (0, t, 0))
    prm = (gn, wb, p0) + tuple(prm) + (_block_prefix_ones(CHAIN_TILE, TIME_BLOCK),)
    return pl.pallas_call(
        functools.partial(_rwkv_prep_chain_kernel, batch=batch, parts=parts),
        grid=(n_tiles, PREP_KEY_SLICES),
        in_specs=[pl.BlockSpec((batch, CHAIN_TILE, D_MODEL), lambda t, a: (0, t, 0))]
        + [_full(p.shape) for p in prm],
        out_specs=[pl.BlockSpec((N_COEF, B_HEAD // PREP_KEY_SLICES, CHAIN_TILE, V7X_LANES),
                                lambda t, a: (0, a, t, 0)),
                   pl.BlockSpec((CHAIN_TILE, B_HEAD // parts, V7X_LANES), lambda t, a: (t, 0, 0)),
                   tok, tok],
        out_shape=[jax.ShapeDtypeStruct((N_COEF, B_HEAD, seq, V7X_LANES), F32),
                   jax.ShapeDtypeStruct((seq, B_HEAD // parts, V7X_LANES), F32),
                   jax.ShapeDtypeStruct((batch, seq, B_WIDTH), F32),
                   jax.ShapeDtypeStruct((batch, seq, B_WIDTH), F32)],
        scratch_shapes=[pltpu.VMEM((batch, B_COLS), F32),
                        pltpu.VMEM((N_COEF + 1, batch, B_WIDTH, CHAIN_TILE), F32)],
        compiler_params=_params("arbitrary", "arbitrary"),
        name="rwkv_prep_chain",
    )(x.reshape(batch, seq, D_MODEL), *prm)


def _rwkv_unpack_kernel(y_ref, o_ref, yt_ref, *, batch, parts):
    half_rows = B_HEAD // parts
    for i in range(half_rows):
        nt = y_ref[:, i, :].T
        for p in range(parts):
            rows = slice(p * batch * B_HEADS, (p + 1) * batch * B_HEADS)
            yt_ref[:, pl.ds((p * half_rows + i) * B_HEADS, B_HEADS), :] = (
                nt[rows].reshape(batch, B_HEADS, CHAIN_TILE))
    for b in range(batch):
        o_ref[b] = yt_ref[b].T


def _rwkv_unpack(y, *, batch, seq):
    parts = V7X_LANES // (batch * B_HEADS)
    out = pl.pallas_call(
        functools.partial(_rwkv_unpack_kernel, batch=batch, parts=parts),
        grid=(seq // CHAIN_TILE,),
        in_specs=[pl.BlockSpec((CHAIN_TILE, B_HEAD // parts, V7X_LANES), lambda t: (t, 0, 0))],
        out_specs=pl.BlockSpec((batch, CHAIN_TILE, B_WIDTH), lambda t: (0, t, 0)),
        out_shape=jax.ShapeDtypeStruct((batch, seq, B_WIDTH), F32),
        scratch_shapes=[pltpu.VMEM((batch, B_WIDTH, CHAIN_TILE), F32)],
        compiler_params=_params("arbitrary"),
        name="rwkv_unpack",
    )(y)
    return out.reshape(batch * seq, B_WIDTH)


def _hgrn_prep_kernel(q_ref, f_ref, lbl_ref, qs_ref, fg_ref, kc_ref):
    logits = lbl_ref[...]
    e = jnp.exp(logits - jnp.max(logits, axis=0, keepdims=True))
    lb = e[0:1, :] / jnp.sum(e, axis=0, keepdims=True)
    fz = f_ref[...]
    fg_ref[...] = lb + (1.0 - lb) * jax.nn.sigmoid(fz)
    kc_ref[...] = (1.0 - lb) * jax.nn.sigmoid(-fz)
    qs_ref[...] = jax.nn.silu(q_ref[...])


def _hgrn_prep(ua, lb_logits):
    n = ua.shape[0]
    tm = TOKEN_TILE
    col = lambda j: pl.BlockSpec((tm, A_WIDTH), lambda i: (i, j))
    return pl.pallas_call(
        _hgrn_prep_kernel,
        grid=(n // tm,),
        in_specs=[col(0), col(1), _full(lb_logits.shape)],
        out_specs=[col(0)] * 3,
        out_shape=[jax.ShapeDtypeStruct((n, A_WIDTH), F32)] * 3,
        compiler_params=_params("arbitrary"),
        name="hgrn_prep",
    )(ua, ua, lb_logits)


HGRN_CHUNK = 64


def _hgrn_tables(chunk):
    levels = chunk.bit_length() - 1
    t = np.arange(chunk)
    u, tt = t[None, :], t[:, None]
    rows = [u <= tt]
    masks = []
    for level in range(levels):
        m = 1 << level
        anchor = (t // (2 * m)) * (2 * m) + m - 1
        right = (t % (2 * m)) >= m
        if m < V7X_SUBLANES:
            rows.append(((u > anchor[:, None]) & (u <= tt) & right[:, None])
                        | ((u > tt) & (u <= anchor[:, None]) & ~right[:, None]))
        masks.append((tt // (2 * m) == u // (2 * m)) & right[:, None] & ~right[None, :])
    return (jnp.asarray(np.concatenate(rows, 0), BF16),
            jnp.asarray(np.stack(masks), F32), levels)


def _hgrn_anchor_sums(cum, level):
    m = 1 << level
    groups = []
    for g in range(cum.shape[0] // V7X_SUBLANES):
        start = g * V7X_SUBLANES
        anchor = (start // (2 * m)) * (2 * m) + m - 1
        block = cum[start:start + V7X_SUBLANES]
        ref_row = cum[anchor:anchor + 1]
        groups.append(block - ref_row if start % (2 * m) >= m else ref_row - block)
    return jnp.concatenate(groups, axis=0)


def _split3(x):
    hi = x.astype(BF16)
    r1 = x - hi.astype(F32)
    mid = r1.astype(BF16)
    lo = (r1 - mid.astype(F32)).astype(BF16)
    return hi, mid, lo


def _dot_nt(a, b):
    return lax.dot_general(a.astype(BF16), b.astype(BF16), (((1,), (1,)), ((), ())),
                           preferred_element_type=F32)


def _dot_tn(a, b):
    return lax.dot_general(a.astype(BF16), b.astype(BF16), (((0,), (0,)), ((), ())),
                           preferred_element_type=F32)


def _hgrn_lower_bound(lbl_ref):
    logits = lbl_ref[...]
    e = jnp.exp(logits - jnp.max(logits, axis=0, keepdims=True))
    return e[0:1, :] / jnp.sum(e, axis=0, keepdims=True)


def _hgrn_chunk_kernel(q_ref, f_ref, i_ref, lbl_ref, sums_ref, mask_ref, s0_ref, o_ref,
                       s_out_ref, st_ref, *, chunk, n_chunks, levels):
    n_seqs = st_ref.shape[0]

    @pl.when(pl.program_id(1) == 0)
    def _():
        for s in range(n_seqs):
            for h in range(A_HEADS):
                st_ref[s, h] = s0_ref[s, h].T

    lb = _hgrn_lower_bound(lbl_ref)
    sums = sums_ref[...]
    seg = lambda r: slice(r * chunk, (r + 1) * chunk)

    def one_head(s, h, rows):
        hs = slice(h * A_HEAD, (h + 1) * A_HEAD)
        fz = f_ref[s, rows, hs]
        lbh = lb[:, hs]
        logf = jnp.log(lbh + (1.0 - lbh) * jax.nn.sigmoid(fz))
        kh = (1.0 - lbh) * jax.nn.sigmoid(-fz)
        qh = jax.nn.silu(q_ref[s, rows, hs])
        vh = i_ref[s, rows, hs]
        yield
        sums_out = sum(jnp.dot(sums, part, preferred_element_type=F32)
                       for part in _split3(logf))
        yield
        cum = sums_out[seg(0)]
        from_start = jnp.exp(cum)
        to_end = jnp.exp(cum[chunk - 1:chunk] - cum)
        att = None
        for level in range(levels):
            if (1 << level) < V7X_SUBLANES:
                split = jnp.exp(sums_out[seg(1 + level)])
            else:
                split = jnp.exp(_hgrn_anchor_sums(cum, level))
            term = mask_ref[level] * _dot_nt(qh * split, kh * split)
            att = term if att is None else att + term
            if level % 2 == 1:
                yield
        st = st_ref[s, h]
        o_ref[s, rows, hs] = (_bdot(att, vh.astype(BF16))
                              + jnp.sum(qh * kh, axis=-1, keepdims=True) * vh
                              + _dot_nt(qh * from_start, st))
        yield
        st_ref[s, h] = st * from_start[chunk - 1:chunk] + _dot_tn(vh, kh * to_end)
        yield

    for c in range(n_chunks):
        _round_robin(one_head(s, h, pl.ds(c * chunk, chunk))
                     for h in range(A_HEADS) for s in range(n_seqs))

    @pl.when(pl.program_id(1) == pl.num_programs(1) - 1)
    def _():
        for s in range(n_seqs):
            for h in range(A_HEADS):
                s_out_ref[s, h] = st_ref[s, h].T


HGRN_SEQS_PER_STEP = 4


def _hgrn_chunked(ua, lb_logits, s0, *, batch, seq):
    chunk = HGRN_CHUNK
    tile = TOKEN_TILE
    n_seqs = HGRN_SEQS_PER_STEP
    sums, masks, levels = _hgrn_tables(chunk)
    col = lambda j: pl.BlockSpec((n_seqs, tile, A_WIDTH), lambda b, t: (b, t, j))
    state_spec = pl.BlockSpec((n_seqs, A_HEADS, A_HEAD, A_HEAD), lambda b, t: (b, 0, 0, 0))
    out, state = pl.pallas_call(
        functools.partial(_hgrn_chunk_kernel, chunk=chunk, n_chunks=tile // chunk, levels=levels),
        grid=(batch // n_seqs, seq // tile),
        in_specs=[col(0), col(1), col(2), _full(lb_logits.shape), _full(sums.shape),
                  _full(masks.shape), state_spec],
        out_specs=[col(0), state_spec],
        out_shape=[jax.ShapeDtypeStruct((batch, seq, A_WIDTH), F32),
                   jax.ShapeDtypeStruct(s0.shape, F32)],
        scratch_shapes=[pltpu.VMEM((n_seqs, A_HEADS, A_HEAD, A_HEAD), F32)],
        compiler_params=_params("arbitrary", "arbitrary"),
        name="hgrn_chunked",
    )(*[ua.reshape(batch, seq, A_COLS)] * 3, lb_logits, sums, masks, s0)
    return out.reshape(batch * seq, A_WIDTH), state


def _bcast_row(ref, t, j):
    return jnp.broadcast_to(ref[t, pl.ds(j, 1), :], (V7X_SUBLANES, V7X_LANES))


def _delta_rule_kernel(kk0_ref, kkn_ref, w_ref, b_ref, k_ref, r_ref, v_ref, s0_ref,
                       y_ref, s_ref, sk_ref, *, nib, nj, tb):
    @pl.when(pl.program_id(1) == 0)
    def _():
        s_ref[...] = s0_ref[...]
        for ib in range(nib):
            acc = None
            for j in range(nj):
                kk0 = jnp.broadcast_to(kk0_ref[pl.ds(j, 1), :], (V7X_SUBLANES, V7X_LANES))
                term = s0_ref[ib, j] * kk0
                acc = term if acc is None else acc + term
            sk_ref[ib] = acc

    def step(t, carry):
        sk = [sk_ref[ib] for ib in range(nib)]
        vv = [v_ref[t, ib] for ib in range(nib)]
        yacc = [None] * nib
        skn = [None] * nib
        for j in range(nj):
            wj = _bcast_row(w_ref, t, j)
            bj = _bcast_row(b_ref, t, j)
            kj = _bcast_row(k_ref, t, j)
            rj = _bcast_row(r_ref, t, j)
            nj_kk = _bcast_row(kkn_ref, t, j)
            for ib in range(nib):
                s = s_ref[ib, j] * wj - sk[ib] * bj + vv[ib] * kj
                s_ref[ib, j] = s
                yt = s * rj
                st = s * nj_kk
                yacc[ib] = yt if yacc[ib] is None else yacc[ib] + yt
                skn[ib] = st if skn[ib] is None else skn[ib] + st
        for ib in range(nib):
            y_ref[t, ib] = yacc[ib]
            sk_ref[ib] = skn[ib]
        return carry

    lax.fori_loop(0, tb, step, 0)


def _delta_rule_chain_kernel(kk_ref, gamma_ref, b_ref, k_ref, r_ref, kk_head_ref, v_ref, s0_ref,
                             y_ref, s_ref, sk_ref, *, nib, nj, tb):
    full = (V7X_SUBLANES, V7X_LANES)
    row = lambda ref, j, t: jnp.broadcast_to(ref[j, pl.ds(t, 1), :], full)

    @pl.when(pl.program_id(0) == 0)
    def _():
        s_ref[...] = s0_ref[...]
        for ib in range(nib):
            acc = None
            for j in range(nj):
                term = s0_ref[ib, j] * row(kk_ref, j, 0)
                acc = term if acc is None else acc + term
            sk_ref[ib] = acc

    def step(t, last):
        sk = [sk_ref[ib] for ib in range(nib)]
        vv = [v_ref[t, ib] for ib in range(nib)]
        yacc = [None] * nib
        skn = [None] * nib
        for j in range(nj):
            bj, kj, rj = (row(ref, j, t) for ref in (b_ref, k_ref, r_ref))
            if last:
                total = row(gamma_ref, j, t)
                kkn = row(kk_head_ref, j, 0) * total
            else:
                kkn = row(kk_ref, j, t + 1)
            for ib in range(nib):
                s = s_ref[ib, j] + (vv[ib] * kj - sk[ib] * bj)
                s_ref[ib, j] = s * total if last else s
                yt = s * rj
                st = s * kkn
                yacc[ib] = yt if yacc[ib] is None else yacc[ib] + yt
                skn[ib] = st if skn[ib] is None else skn[ib] + st
        for ib in range(nib):
            y_ref[t, ib] = yacc[ib]
            sk_ref[ib] = skn[ib]

    def body(t, carry):
        step(t, last=False)
        return carry

    lax.fori_loop(0, tb - 1, body, 0)
    step(tb - 1, last=True)


def _delta_rule_chain(coef, v, s0):
    _, nj, t_len, _ = coef.shape
    nib = v.shape[1]
    tb = TIME_BLOCK
    n_blocks = t_len // tb
    head_rows = V7X_SUBLANES
    coef_spec = lambda a: pl.BlockSpec((None, nj, tb, V7X_LANES), lambda t: (a, 0, t, 0))
    head_spec = pl.BlockSpec(
        (None, nj, head_rows, V7X_LANES),
        lambda t: (0, 0, jnp.minimum(t + 1, n_blocks - 1) * (tb // head_rows), 0))
    row_spec = pl.BlockSpec((tb, nib, V7X_SUBLANES, V7X_LANES), lambda t: (t, 0, 0, 0))
    return pl.pallas_call(
        functools.partial(_delta_rule_chain_kernel, nib=nib, nj=nj, tb=tb),
        grid=(n_blocks,),
        in_specs=[coef_spec(a) for a in range(N_COEF)]
        + [head_spec, row_spec, _full(s0.shape)],
        out_specs=[row_spec, _full(s0.shape)],
        out_shape=[jax.ShapeDtypeStruct(v.shape, F32), jax.ShapeDtypeStruct(s0.shape, F32)],
        scratch_shapes=[pltpu.VMEM((nib, V7X_SUBLANES, V7X_LANES), F32)],
        compiler_params=_params("arbitrary"),
        name="delta_rule_chain",
    )(coef, coef, coef, coef, coef, coef, v, s0)


def _decay_rule_kernel(w_ref, k_ref, r_ref, v_ref, s0_ref, y_ref, s_ref, st_ref,
                       *, nib, nj, t_len):
    full = (V7X_SUBLANES, V7X_LANES)
    for j in range(nj):
        tile = s0_ref[:, j * V7X_LANES:(j + 1) * V7X_LANES]
        st_ref[j] = tile.T.reshape(nib, V7X_SUBLANES, V7X_LANES)
    for t in range(t_len):
        vv = [v_ref[t, ib] for ib in range(nib)]

        def one_key(j, yacc):
            wj, kj, rj = (jnp.broadcast_to(ref[t, pl.ds(j, 1), :], full)
                          for ref in (w_ref, k_ref, r_ref))
            out = []
            for ib in range(nib):
                s = st_ref[j, ib] * wj + vv[ib] * kj
                st_ref[j, ib] = s
                out.append(yacc[ib] + s * rj)
            return tuple(out)

        yacc = lax.fori_loop(0, nj, one_key, tuple(jnp.zeros(full, F32) for _ in range(nib)))
        for ib in range(nib):
            y_ref[t, ib] = yacc[ib]
    for j in range(nj):
        s_ref[:, j * V7X_LANES:(j + 1) * V7X_LANES] = (
            st_ref[j].reshape(nib * V7X_SUBLANES, V7X_LANES).T)


def _decay_rule(coefs, v, s0):
    t_len, heads, nib = v.shape[0], v.shape[1], v.shape[2]
    nj = coefs[0].shape[2]
    chains = s0.shape[0]
    assert chains == V7X_LANES and nib * V7X_SUBLANES == V7X_LANES
    width = nj * V7X_LANES
    coef_spec = pl.BlockSpec((t_len, None, nj, V7X_LANES), lambda h: (0, h, 0, 0))
    row_spec = pl.BlockSpec((t_len, None, nib, V7X_SUBLANES, V7X_LANES),
                            lambda h: (0, h, 0, 0, 0))
    state_spec = pl.BlockSpec((chains, width), lambda h: (0, h))
    y, state = pl.pallas_call(
        functools.partial(_decay_rule_kernel, nib=nib, nj=nj, t_len=t_len),
        grid=(heads,),
        in_specs=[coef_spec] * 3 + [row_spec, state_spec],
        out_specs=[row_spec, state_spec],
        out_shape=[jax.ShapeDtypeStruct(v.shape, F32),
                   jax.ShapeDtypeStruct((chains, heads * width), F32)],
        scratch_shapes=[pltpu.VMEM((nj, nib, V7X_SUBLANES, V7X_LANES), F32)],
        compiler_params=_params("arbitrary"),
        name="decay_rule",
    )(*coefs, v, s0.reshape(chains, heads * width))
    return y, state.reshape(s0.shape)


def _delta_rule(coefs, v, s0, *, kk0, groups_per_coef):
    t_len, q, nib = v.shape[0], v.shape[1], v.shape[2]
    nj = coefs[0].shape[2]
    tb = min(TIME_BLOCK, t_len)
    coef_spec = pl.BlockSpec((tb, None, nj, V7X_LANES),
                             lambda g, t: (t, g // groups_per_coef, 0, 0))
    row_spec = pl.BlockSpec((tb, None, nib, V7X_SUBLANES, V7X_LANES),
                            lambda g, t: (t, g, 0, 0, 0))
    state_spec = pl.BlockSpec((None, nib, nj, V7X_SUBLANES, V7X_LANES),
                              lambda g, t: (g, 0, 0, 0, 0))
    kk0_spec = pl.BlockSpec((None, nj, V7X_LANES), lambda g, t: (g // groups_per_coef, 0, 0))
    return pl.pallas_call(
        functools.partial(_delta_rule_kernel, nib=nib, nj=nj, tb=tb),
        grid=(q, t_len // tb),
        in_specs=[kk0_spec] + [coef_spec] * 5 + [row_spec, state_spec],
        out_specs=[row_spec, state_spec],
        out_shape=[jax.ShapeDtypeStruct(v.shape, F32), jax.ShapeDtypeStruct(s0.shape, F32)],
        scratch_shapes=[pltpu.VMEM((nib, V7X_SUBLANES, V7X_LANES), F32)],
        compiler_params=_params("arbitrary", "arbitrary"),
        name="delta_rule",
    )(kk0, *coefs, v, s0)


def _mix_kernel(x_ref, oa_ref, ga_ref, y_ref, bonus_ref, gb_ref, gn_ref, wg_ref, hw_ref, lnw_ref,
                lnb_ref, wa_ref, wb_ref, wo_ref, x1_ref):
    def sub_tile(rows):
        x = x_ref[rows, :]
        gate = jax.nn.sigmoid(_bdot(_rmsnorm(x, gn_ref[...]), wg_ref[...]))
        yield
        oa = oa_ref[rows, :]
        ms = _head_sum_lanes(oa * oa, A_HEAD) * (1.0 / A_HEAD)
        oa = oa * lax.rsqrt(ms + HGRN_NORM_EPS) * hw_ref[...] * jax.nn.silu(ga_ref[rows, :])
        ya = _bdot(oa, wa_ref[...])
        yield
        y = y_ref[rows, :]
        d = y - _head_sum_rolled(y, B_HEADS) * (1.0 / B_HEAD)
        var = _head_sum_rolled(d * d, B_HEADS) * (1.0 / B_HEAD)
        yn = d * lax.rsqrt(var + GN_EPS) * lnw_ref[...] + lnb_ref[...]
        ob = (yn + bonus_ref[rows, :]) * gb_ref[rows, :]
        yb = _bdot(ob, wb_ref[...])
        yield
        merged = gate[:, 0:D_MODEL] * ya + gate[:, D_MODEL:GATE_COLS] * yb
        x1_ref[rows, :] = x + _bdot(merged, wo_ref[...])

    _round_robin(sub_tile(rows) for rows in _sub_tile_rows(x_ref.shape[0]))


def _mix(x, oa, ua, y, bonus, gb, gn, wg, hw, lnw, lnb, wa, wb, wo):
    n = x.shape[0]
    tm = TOKEN_TILE * SUBTILES
    row = lambda c: pl.BlockSpec((tm, c), lambda i: (i, 0))
    vec = _full((1, A_WIDTH))
    return pl.pallas_call(
        _mix_kernel,
        grid=(n // tm,),
        in_specs=[row(D_MODEL), row(A_WIDTH), pl.BlockSpec((tm, A_WIDTH), lambda i: (i, 3)),
                  row(B_WIDTH), row(B_WIDTH), row(B_WIDTH), _full((1, D_MODEL)),
                  _full((D_MODEL, GATE_COLS)), vec, vec, vec,
                  _full((A_WIDTH, D_MODEL)), _full((B_WIDTH, D_MODEL)),
                  _full((D_MODEL, D_MODEL))],
        out_specs=row(D_MODEL),
        out_shape=jax.ShapeDtypeStruct((n, D_MODEL), F32),
        compiler_params=_params("arbitrary"),
        name="mix",
    )(x, oa, ua, y, bonus, gb, gn, wg, hw, lnw, lnb, wa, wb, wo)


def _mlp_kernel(x_ref, g_ref, wu_ref, wd_ref, gf_ref, o_ref):
    def sub_tile(rows):
        x1 = x_ref[rows, :]
        hb = _rmsnorm(x1, g_ref[...]).astype(BF16)
        yield
        act = jnp.square(jnp.maximum(jnp.dot(hb, wu_ref[...], preferred_element_type=F32), 0.0))
        yield
        x2 = x1 + _bdot(act, wd_ref[...])
        yield
        o_ref[rows, :] = _rmsnorm(x2, gf_ref[...])

    _round_robin(sub_tile(rows) for rows in _sub_tile_rows(x_ref.shape[0]))


def _mlp(x1, g, wu, wd, gf):
    n = x1.shape[0]
    tm = TOKEN_TILE * SUBTILES
    row = pl.BlockSpec((tm, D_MODEL), lambda i: (i, 0))
    return pl.pallas_call(
        _mlp_kernel,
        grid=(n // tm,),
        in_specs=[row, _full((1, D_MODEL)), _full((D_MODEL, D_FF)), _full((D_FF, D_MODEL)),
                  _full((1, D_MODEL))],
        out_specs=row,
        out_shape=jax.ShapeDtypeStruct((n, D_MODEL), F32),
        compiler_params=_params("arbitrary"),
        name="mlp",
    )(x1, g, wu, wd, gf)


class _SampleLayout:
    def __init__(self, batch, seq, heads, nj, ni, head_minor):
        assert batch == V7X_LANES
        self.b, self.t, self.h, self.nj, self.ni = batch, seq, heads, nj, ni
        self.head_minor = head_minor
        self.groups_per_coef = ni // ROWS_PER_GROUP
        self.nib = ROWS_PER_GROUP // V7X_SUBLANES

    def _to_chain(self, x, width):
        b, t, h = self.b, self.t, self.h
        if self.head_minor:
            return x.reshape(t, b, width, h).transpose(0, 3, 2, 1)
        return x.reshape(t, b, h, width).transpose(0, 2, 3, 1)

    def coef(self, x):
        return self._to_chain(x, self.nj)

    def rows_in(self, v):
        v = self._to_chain(v, self.ni)
        return v.reshape(self.t, self.h * self.groups_per_coef, self.nib, V7X_SUBLANES, V7X_LANES)

    def rows_out(self, y):
        b, t, h, ni = self.b, self.t, self.h, self.ni
        y = y.reshape(t, h, ni, b)
        y = y.transpose(0, 3, 2, 1) if self.head_minor else y.transpose(0, 3, 1, 2)
        return y.reshape(t * b, h * ni)

    def state_in(self, s, rows_last):
        b, h, nj = self.b, self.h, self.nj
        if not rows_last:
            s = s.transpose(0, 1, 3, 2)
        s = s.reshape(b, h, nj, self.groups_per_coef, self.nib, V7X_SUBLANES)
        s = s.transpose(1, 3, 4, 2, 5, 0)
        return s.reshape(h * self.groups_per_coef, self.nib, nj, V7X_SUBLANES, V7X_LANES)

    def state_out(self, s, rows_last):
        b, h, ni, nj = self.b, self.h, self.ni, self.nj
        s = s.reshape(h, self.groups_per_coef, self.nib, nj, V7X_SUBLANES, b)
        s = s.transpose(5, 0, 3, 1, 2, 4).reshape(b, h, nj, ni)
        return s if rows_last else s.transpose(0, 1, 3, 2)


def _prompt_state_to_chain(s, parts):
    b, h, ni, nj = s.shape
    nib = ni // parts // V7X_SUBLANES
    s = s.reshape(b, h, parts, nib, V7X_SUBLANES, nj).transpose(3, 5, 4, 2, 0, 1)
    return s.reshape(nib, nj, V7X_SUBLANES, V7X_LANES)


def _prompt_state_from_chain(s, batch, heads, parts):
    nib, nj = s.shape[0], s.shape[1]
    s = s.reshape(nib, nj, V7X_SUBLANES, parts, batch, heads).transpose(4, 5, 3, 0, 2, 1)
    return s.reshape(batch, heads, parts * nib * V7X_SUBLANES, nj)


def _same_head_matrix(width, heads):
    idx = np.arange(width) % heads
    return jnp.asarray(idx[:, None] == idx[None, :], BF16)


def _trunk(x, shift0, state_a, state_b, wts, *, time_major):
    batch, seq, _ = x.shape
    n = batch * seq
    if time_major:
        x2 = x.transpose(1, 0, 2).reshape(n, D_MODEL)
        time_stride = batch
    else:
        x2 = x.reshape(n, D_MODEL)
        time_stride = 1

    if time_major:
        ua, ub = _norm_proj(x2, wts["norm_mix_g"], (wts["w_in_a"], wts["w_in_b"]))
    else:
        ua, = _norm_proj(x2, wts["norm_mix_g"], (wts["w_in_a"],))
    p0 = _prev_proj(shift0, wts["w_in_b"])
    prep_params = (wts["mu_shift"], wts["w_lora"], wts["w_decay0"], wts["a0"], wts["k_k"],
                   wts["k_a"], wts["r_k"], wts["same_head_b"])

    if time_major:
        kk, w, bco, k2, r, v, gb, bonus = _rwkv_prep(ub, p0, prep_params,
                                                     time_stride=time_stride)
        lay_b = _SampleLayout(batch, seq, B_HEADS, B_HEAD, B_HEAD, head_minor=True)
        kk_c = lay_b.coef(kk)
        kk_next = jnp.concatenate([kk_c[1:], jnp.zeros_like(kk_c[:1])], axis=0)
        y_c, sb_c = _delta_rule(
            (kk_next, lay_b.coef(w), lay_b.coef(bco), lay_b.coef(k2), lay_b.coef(r)),
            lay_b.rows_in(v), lay_b.state_in(state_b, rows_last=False), kk0=kk_c[0],
            groups_per_coef=lay_b.groups_per_coef)
        y_b = lay_b.rows_out(y_c)
        new_wkv = lay_b.state_out(sb_c, rows_last=False)
    else:
        parts = V7X_LANES // (batch * B_HEADS)
        coef, v_c, gb, bonus = _rwkv_prep_chain(x2, wts["norm_mix_g"], wts["w_in_b"], p0,
                                                prep_params, batch=batch, seq=seq)
        gb = gb.reshape(n, B_WIDTH)
        bonus = bonus.reshape(n, B_WIDTH)
        nib = B_HEAD // parts // V7X_SUBLANES
        y_c, sb_c = _delta_rule_chain(
            coef, v_c.reshape(seq, nib, V7X_SUBLANES, V7X_LANES),
            _prompt_state_to_chain(state_b, parts))
        o_a, new_hgrn = _hgrn_chunked(ua, wts["lb_logits"], state_a, batch=batch, seq=seq)
        y_b = _rwkv_unpack(y_c.reshape(seq, nib * V7X_SUBLANES, V7X_LANES), batch=batch, seq=seq)
        new_wkv = _prompt_state_from_chain(sb_c, batch, B_HEADS, parts)

    if time_major:
        qs, fg, kc = _hgrn_prep(ua, wts["lb_logits"])
        lay_a = _SampleLayout(batch, seq, A_HEADS, A_HEAD, A_HEAD, head_minor=False)
        vi = lay_a.coef(ua[:, 2 * A_WIDTH:3 * A_WIDTH])
        o_c, new_hgrn = _decay_rule(
            (lay_a.coef(fg), lay_a.coef(kc), lay_a.coef(qs)),
            vi.reshape(seq, A_HEADS, A_HEAD // V7X_SUBLANES, V7X_SUBLANES, V7X_LANES), state_a)
        o_a = lay_a.rows_out(o_c)

    x1 = _mix(x2, o_a, ua, y_b, bonus, gb, wts["norm_mix_g"], wts["w_in_g"],
              wts["hgrn_norm_w"], wts["ln_x_w"],
              wts["ln_x_b"], wts["w_a_out"], wts["w_b_out"], wts["w_out"])
    y = _mlp(x1, wts["norm_mlp_g"], wts["w_up"], wts["w_down"], wts["norm_final_g"])
    if time_major:
        y = y.reshape(seq, batch, D_MODEL).transpose(1, 0, 2)
    else:
        y = y.reshape(batch, seq, D_MODEL)
    new_shift = _norm_rows(x[:, -1, :], wts["norm_mix_g"])
    return y, new_hgrn[None], new_wkv[None], new_shift[None]


def kernel(x_prompt, x_sample, state_hgrn, state_wkv, state_shift, norm_mix_g, w_in, mu_shift,
           w_decay0, w_decay_up, a0, w_aaa_up, w_gate_up, k_k, k_a, r_k, ln_x_w, ln_x_b,
           lb_logits, hgrn_norm_w, w_a_out, w_b_out, w_out, norm_mlp_g, w_up, w_down,
           norm_final_g):
    assert w_in.shape[0] == 1, "single-layer stack"
    def pcols(a):
        lead = a.shape[:-1]
        return a.reshape(*lead, B_HEADS, B_HEAD).swapaxes(-1, -2).reshape(*lead, B_WIDTH)

    prows = lambda a: a.reshape(B_HEADS, B_HEAD, -1).swapaxes(0, 1).reshape(B_WIDTH, -1)
    w_in0 = w_in[0]
    w_in_b = w_in0[:, A_COLS:A_COLS + B_COLS]
    w_in_b = jnp.concatenate(
        [pcols(w_in_b[:, s * B_WIDTH:(s + 1) * B_WIDTH]) for s in range(3)]
        + [w_in_b[:, 3 * B_WIDTH:]], axis=1)
    mu = mu_shift[0]
    mu = jnp.concatenate([pcols(mu[s * B_WIDTH:(s + 1) * B_WIDTH]) for s in range(3)]
                         + [mu[3 * B_WIDTH:]])
    w_lora = jnp.zeros((LORA_COLS, 3 * B_WIDTH), F32)
    w_lora = w_lora.at[:DECAY_LORA, :B_WIDTH].set(pcols(w_decay_up[0]))
    w_lora = w_lora.at[DECAY_LORA:DECAY_LORA + AAA_LORA, B_WIDTH:2 * B_WIDTH].set(
        pcols(w_aaa_up[0]))
    w_lora = w_lora.at[DECAY_LORA + AAA_LORA:, 2 * B_WIDTH:].set(pcols(w_gate_up[0]))
    row = lambda a: a.reshape(1, -1).astype(F32)
    prow = lambda a: row(pcols(a.reshape(-1)))
    wts = {
        "norm_mix_g": row(norm_mix_g[0]),
        "w_in_a": w_in0[:, :A_COLS].astype(BF16),
        "w_in_b": w_in_b.astype(BF16),
        "w_in_g": w_in0[:, A_COLS + B_COLS:].astype(BF16),
        "mu_shift": row(mu),
        "w_lora": w_lora.astype(BF16),
        "w_decay0": prow(w_decay0[0]),
        "a0": prow(a0[0]),
        "k_k": prow(k_k[0]),
        "k_a": prow(k_a[0]),
        "r_k": prow(r_k[0]),
        "ln_x_w": prow(ln_x_w[0]),
        "ln_x_b": prow(ln_x_b[0]),
        "lb_logits": lb_logits.astype(F32),
        "hgrn_norm_w": row(hgrn_norm_w[0]),
        "w_a_out": w_a_out[0].astype(BF16),
        "w_b_out": prows(w_b_out[0]).astype(BF16),
        "w_out": w_out[0].astype(BF16),
        "norm_mlp_g": row(norm_mlp_g[0]),
        "w_up": w_up[0].astype(BF16),
        "w_down": w_down[0].astype(BF16),
        "norm_final_g": row(norm_final_g),
        "same_head_b": _same_head_matrix(B_WIDTH, B_HEADS),
    }
    bp = x_prompt.shape[0]
    y_p, hgrn_p, wkv_p, shift_p = _trunk(
        x_prompt, jnp.zeros((bp, D_MODEL), F32),
        jnp.zeros((bp, A_HEADS, A_HEAD, A_HEAD), F32),
        jnp.zeros((bp, B_HEADS, B_HEAD, B_HEAD), F32), wts, time_major=False)
    y_s, hgrn_s, wkv_s, shift_s = _trunk(
        x_sample, state_shift[0], state_hgrn[0], state_wkv[0], wts, time_major=True)
    return (y_p, y_s, hgrn_p, wkv_p, shift_p, hgrn_s, wkv_s, shift_s)
```
